```python
import math
import jax, jax.numpy as jnp
from jax import lax
import numpy as np

D_MODEL = 1024
BATCH = 2
SEQ = 8192
DEPTH = 4
DEC_BATCH = 128
DEC_SEQ = 4
PAST_LEN = 8192
PAGE_SIZE = 128

N_EVEN = (DEPTH + 1) // 2
N_ODD = DEPTH // 2
NORM_EPS = 1e-6
NEG_INF = -1e30

SWA_HEADS = 8
SWA_KV_HEADS = 2
SWA_HEAD_DIM = 64
SWA_GROUP = SWA_HEADS // SWA_KV_HEADS
SWA_WIDTH = SWA_HEADS * SWA_HEAD_DIM
SWA_KV_WIDTH = SWA_KV_HEADS * SWA_HEAD_DIM
WINDOW = 128
SWA_BLOCK = WINDOW
REL_BUCKETS = 32
REL_MAX_DIST = 128

GMLP_GROUPS = 4
GMLP_CHUNK = 128
GMLP_WIDTH = D_MODEL // 2
GMLP_GROUP_DIM = GMLP_WIDTH // GMLP_GROUPS

EVEN_SIZES = (SWA_WIDTH, SWA_KV_WIDTH, SWA_KV_WIDTH, SWA_WIDTH, GMLP_WIDTH, GMLP_WIDTH, GMLP_WIDTH)
EVEN_IN = 3 * SWA_WIDTH // 1 - SWA_WIDTH + 2 * SWA_KV_WIDTH + 3 * GMLP_WIDTH
EVEN_MIX = SWA_WIDTH + GMLP_WIDTH

RET_HEADS = D_MODEL // 256
RET_KEY_DIM = 256
RET_VALUE_DIM = 2 * RET_KEY_DIM
RET_QK_WIDTH = RET_HEADS * RET_KEY_DIM
RET_V_WIDTH = RET_HEADS * RET_VALUE_DIM
RET_CHUNK = 128
ODD_SIZES = (RET_QK_WIDTH, RET_QK_WIDTH, RET_V_WIDTH, RET_V_WIDTH)
ODD_IN = 2 * RET_QK_WIDTH + 2 * RET_V_WIDTH

kernel_name = "hybrid_swa_gmlp_retention_step"


def _split(z, sizes):
    out, start = [], 0
    for s in sizes:
        out.append(z[..., start:start + s])
        start += s
    return out


def rmsnorm(x, g):
    x32 = x.astype(jnp.float32)
    y = x32 * lax.rsqrt(jnp.mean(x32 * x32, axis=-1, keepdims=True) + NORM_EPS)
    return (y * g.astype(jnp.float32)).astype(x.dtype)


def t5_bucket(dist):
    n = jnp.maximum(dist, 0)
    max_exact = REL_BUCKETS // 2
    nf = jnp.maximum(n, 1).astype(jnp.float32)
    large = max_exact + (jnp.log(nf / max_exact) / math.log(REL_MAX_DIST / max_exact)
                         * (REL_BUCKETS - max_exact)).astype(jnp.int32)
    large = jnp.minimum(large, REL_BUCKETS - 1)
    return jnp.where(n < max_exact, n, large)


def rel_bias(dist, table):
    b = table[t5_bucket(dist)].astype(jnp.float32)
    b = jnp.moveaxis(b, -1, 0)
    return b.reshape(SWA_KV_HEADS, SWA_GROUP, *dist.shape)


def sink_softmax(scores, sink):
    sink = sink.astype(jnp.float32)
    m = jnp.maximum(scores.max(axis=-1, keepdims=True), sink)
    p = jnp.exp(scores - m)
    return p / (p.sum(axis=-1, keepdims=True) + jnp.exp(sink - m))


def swa_prompt(q, k, v, sinks, table):
    B, S = q.shape[:2]
    nb = S // SWA_BLOCK
    qb = q.reshape(B, nb, SWA_BLOCK, SWA_KV_HEADS, SWA_GROUP, SWA_HEAD_DIM)
    kb = k.reshape(B, nb, SWA_BLOCK, SWA_KV_HEADS, SWA_HEAD_DIM)
    vb = v.reshape(B, nb, SWA_BLOCK, SWA_KV_HEADS, SWA_HEAD_DIM)
    shift = lambda a: jnp.concatenate([jnp.zeros_like(a[:, :1]), a[:, :-1]], axis=1)
    kk = jnp.concatenate([shift(kb), kb], axis=2)
    vv = jnp.concatenate([shift(vb), vb], axis=2)
    scores = jnp.einsum('bnqkgd,bnskd->bnkgqs', qb, kk).astype(jnp.float32) * (SWA_HEAD_DIM ** -0.5)
    qi = jnp.arange(SWA_BLOCK)[:, None]
    sj = jnp.arange(2 * SWA_BLOCK)[None, :]
    dist = qi + SWA_BLOCK - sj
    kpos = jnp.arange(nb)[:, None, None] * SWA_BLOCK - SWA_BLOCK + sj[None]
    valid = (dist >= 0) & (dist < WINDOW) & (kpos >= 0)
    bias = rel_bias(dist, table)
    scores = jnp.where(valid[None, :, None, None], scores + bias[None, None], NEG_INF)
    sink = sinks.reshape(SWA_KV_HEADS, SWA_GROUP)[None, None, :, :, None, None]
    probs = sink_softmax(scores, sink)
    out = jnp.einsum('bnkgqs,bnskd->bnqkgd', probs.astype(v.dtype), vv)
    return out.reshape(B, S, SWA_WIDTH)


def swa_sample(q, k, v, cache_k, cache_v, sinks, table):
    DB, T = q.shape[:2]
    kk = jnp.concatenate([cache_k.astype(k.dtype), k], axis=1)
    vv = jnp.concatenate([cache_v.astype(v.dtype), v], axis=1)
    scores = jnp.einsum('btkgd,bskd->bkgts', q, kk).astype(jnp.float32) * (SWA_HEAD_DIM ** -0.5)
    dist = jnp.arange(T)[:, None] + WINDOW - jnp.arange(WINDOW + T)[None, :]
    valid = (dist >= 0) & (dist < WINDOW)
    bias = rel_bias(dist, table)
    scores = jnp.where(valid, scores + bias[None], NEG_INF)
    sink = sinks.reshape(SWA_KV_HEADS, SWA_GROUP)[None, :, :, None, None]
    probs = sink_softmax(scores, sink)
    out = jnp.einsum('bkgts,bskd->btkgd', probs.astype(v.dtype), vv)
    return out.reshape(DB, T, SWA_WIDTH), kk[:, -WINDOW:], vv[:, -WINDOW:]


def gmlp_spatial(u, vb, ws, bs, ln_gain):
    B, T = u.shape[:2]
    L = min(T, GMLP_CHUNK)
    nc = T // L
    v32 = vb.astype(jnp.float32)
    mu = jnp.mean(v32, axis=-1, keepdims=True)
    var = jnp.mean(jnp.square(v32 - mu), axis=-1, keepdims=True)
    vn = ((v32 - mu) * lax.rsqrt(var + NORM_EPS) * ln_gain.astype(jnp.float32)).astype(u.dtype)
    vc = vn.reshape(B, nc, L, GMLP_GROUPS, GMLP_GROUP_DIM)
    wm = jnp.tril(ws[:, :L, :L]).astype(vc.dtype)
    s = jnp.einsum('gpq,bnqgc->bnpgc', wm, vc) + bs[:, :L].T[None, None, :, :, None]
    return u * s.reshape(B, T, GMLP_WIDTH), vn


def even_layer(x, g_norm, w_in, w_out, sinks, table, ws, bs, ln_gain, cache_k, cache_v):
    B, T = x.shape[:2]
    z = rmsnorm(x, g_norm) @ w_in
    q, k, v, ga, u, vb, gb = _split(z, EVEN_SIZES)
    q = q.reshape(B, T, SWA_KV_HEADS, SWA_GROUP, SWA_HEAD_DIM)
    k = k.reshape(B, T, SWA_KV_HEADS, SWA_HEAD_DIM)
    v = v.reshape(B, T, SWA_KV_HEADS, SWA_HEAD_DIM)
    if cache_k is None:
        attn = swa_prompt(q, k, v, sinks, table)
        new_k, new_v = k[:, -WINDOW:], v[:, -WINDOW:]
    else:
        attn, new_k, new_v = swa_sample(q, k, v, cache_k, cache_v, sinks, table)
    sg, vn = gmlp_spatial(u, vb, ws, bs, ln_gain)
    mix = jnp.concatenate([jax.nn.silu(ga) * attn.astype(x.dtype), jax.nn.silu(gb) * sg], axis=-1)
    return x + mix @ w_out, new_k, new_v, vn


def xpos_rotate(x, pos):
    angle = 1.0 / (10000.0 ** jnp.linspace(0.0, 1.0, RET_KEY_DIM // 2, dtype=jnp.float32))
    ang = pos.astype(jnp.float32)[:, None] * angle[None, :]
    sin = jnp.sin(ang)[:, None, :]
    cos = jnp.cos(ang)[:, None, :]
    x32 = x.astype(jnp.float32)
    x0, x1 = x32[..., 0::2], x32[..., 1::2]
    return jnp.stack([x0 * cos - x1 * sin, x1 * cos + x0 * sin], axis=-1).reshape(x.shape)


def retention_scan(q, k, v, S0, chunk):
    B, T, H, _ = q.shape
    nc = T // chunk
    lg = jnp.log(1.0 - 2.0 ** (-5.0 - jnp.arange(RET_HEADS, dtype=jnp.float32)))
    idx = jnp.arange(chunk, dtype=jnp.float32)
    diff = idx[:, None] - idx[None, :]
    decay = jnp.where(diff >= 0, jnp.exp(lg[:, None, None] * jnp.maximum(diff, 0.0)), 0.0)
    q_dec = jnp.exp(lg[None, :] * (idx[:, None] + 1.0))
    k_dec = jnp.exp(lg[None, :] * (chunk - 1.0 - idx)[:, None])
    c_dec = jnp.exp(lg * chunk)

    def step(S, inp):
        qc, kc, vc = inp
        att = jnp.einsum('bihd,bjhd->bhij', qc, kc) * decay
        intra = jnp.einsum('bhij,bjhe->bihe', att, vc)
        cross = jnp.einsum('bihd,bhde->bihe', qc, S) * q_dec[None, :, :, None]
        S = c_dec[None, :, None, None] * S + jnp.einsum('bjhd,bjhe->bhde', kc * k_dec[None, :, :, None], vc)
        return S, intra + cross

    blk = lambda a: a.reshape(B, nc, chunk, H, a.shape[-1]).swapaxes(0, 1)
    S, o = lax.scan(step, S0, (blk(q), blk(k), blk(v)))
    return o.swapaxes(0, 1).reshape(B, T, H, RET_VALUE_DIM), S


def odd_layer(x, g_norm, w_in, w_out, S0, pos0):
    B, T = x.shape[:2]
    z = rmsnorm(x, g_norm) @ w_in
    q, k, v, g = _split(z, ODD_SIZES)
    pos = pos0 + jnp.arange(T, dtype=jnp.int32)
    q = xpos_rotate(q.reshape(B, T, RET_HEADS, RET_KEY_DIM), pos)
    k = xpos_rotate(k.reshape(B, T, RET_HEADS, RET_KEY_DIM), pos) * (RET_KEY_DIM ** -0.5)
    v = v.reshape(B, T, RET_HEADS, RET_VALUE_DIM).astype(jnp.float32)
    o, S = retention_scan(q, k, v, S0.astype(jnp.float32), min(T, RET_CHUNK))
    mu = jnp.mean(o, axis=-1, keepdims=True)
    var = jnp.mean(jnp.square(o - mu), axis=-1, keepdims=True)
    o = ((o - mu) * lax.rsqrt(var + NORM_EPS)).reshape(B, T, RET_V_WIDTH).astype(x.dtype)
    return x + (jax.nn.silu(g) * o) @ w_out, S


def setup_inputs(seed: int = 0) -> dict:
    key = jax.random.key(seed)
    ks = jax.random.split(key, 18)
    f32 = jnp.float32
    nrm = lambda k, shape, s: jax.random.normal(k, shape, f32) * s
    resid = (2.0 * DEPTH) ** -0.5
    return {
        'x_prompt': nrm(ks[0], (BATCH, SEQ, D_MODEL), 1.0),
        'x_sample': nrm(ks[1], (DEC_BATCH, DEC_SEQ, D_MODEL), 1.0),
        'cache_swa_k': nrm(ks[2], (N_EVEN, DEC_BATCH, WINDOW, SWA_KV_HEADS, SWA_HEAD_DIM), 1.0),
        'cache_swa_v': nrm(ks[3], (N_EVEN, DEC_BATCH, WINDOW, SWA_KV_HEADS, SWA_HEAD_DIM), 1.0),
        'state_ret': nrm(ks[4], (N_ODD, DEC_BATCH, RET_HEADS, RET_KEY_DIM, RET_VALUE_DIM), 0.3),
        'norm_gain': 1.0 + nrm(ks[5], (DEPTH, D_MODEL), 0.05),
        'final_norm_gain': 1.0 + nrm(ks[6], (D_MODEL,), 0.05),
        'rel_bias_table': nrm(ks[7], (REL_BUCKETS, SWA_HEADS), 0.5),
        'even_w_in': nrm(ks[8], (N_EVEN, D_MODEL, EVEN_IN), D_MODEL ** -0.5),
        'even_w_out': nrm(ks[9], (N_EVEN, EVEN_MIX, D_MODEL), EVEN_MIX ** -0.5 * resid),
        'swa_sinks': nrm(ks[10], (N_EVEN, SWA_HEADS), 1.0),
        'gmlp_ws': nrm(ks[11], (N_EVEN, GMLP_GROUPS, GMLP_CHUNK, GMLP_CHUNK), GMLP_CHUNK ** -0.5),
        'gmlp_bs': 1.0 + nrm(ks[12], (N_EVEN, GMLP_GROUPS, GMLP_CHUNK), 0.1),
        'gmlp_ln_gain': 1.0 + nrm(ks[13], (N_EVEN, GMLP_WIDTH), 0.05),
        'odd_w_in': nrm(ks[14], (N_ODD, D_MODEL, ODD_IN), D_MODEL ** -0.5),
        'odd_w_out': nrm(ks[15], (N_ODD, RET_V_WIDTH, D_MODEL), RET_V_WIDTH ** -0.5 * resid),
    }


def reference(x_prompt, x_sample, cache_swa_k, cache_swa_v, state_ret, norm_gain, final_norm_gain,
              rel_bias_table, even_w_in, even_w_out, swa_sinks, gmlp_ws, gmlp_bs, gmlp_ln_gain,
              odd_w_in, odd_w_out):
    yp, ys = x_prompt, x_sample
    kp_l, vp_l, ks_l, vs_l, sp_l, ss_l, gv_l = [], [], [], [], [], [], []
    for layer in range(DEPTH):
        if layer % 2 == 0:
            e = layer // 2
            yp, kp, vp, _ = even_layer(yp, norm_gain[layer], even_w_in[e], even_w_out[e], swa_sinks[e],
                                       rel_bias_table, gmlp_ws[e], gmlp_bs[e], gmlp_ln_gain[e], None, None)
            ys, kn, vn_, gv = even_layer(ys, norm_gain[layer], even_w_in[e], even_w_out[e], swa_sinks[e],
                                         rel_bias_table, gmlp_ws[e], gmlp_bs[e], gmlp_ln_gain[e],
                                         cache_swa_k[e], cache_swa_v[e])
            kp_l.append(kp); vp_l.append(vp); ks_l.append(kn); vs_l.append(vn_); gv_l.append(gv)
        else:
            o = layer // 2
            S0 = jnp.zeros((yp.shape[0], RET_HEADS, RET_KEY_DIM, RET_VALUE_DIM), jnp.float32)
            yp, sp = odd_layer(yp, norm_gain[layer], odd_w_in[o], odd_w_out[o], S0, 0)
            ys, ss = odd_layer(ys, norm_gain[layer], odd_w_in[o], odd_w_out[o], state_ret[o], PAST_LEN)
            sp_l.append(sp); ss_l.append(ss)
    y_prompt = rmsnorm(yp, final_norm_gain)
    y_sample = rmsnorm(ys, final_norm_gain)
    return (y_prompt, y_sample, jnp.stack(kp_l), jnp.stack(vp_l), jnp.stack(ks_l), jnp.stack(vs_l),
            jnp.stack(sp_l), jnp.stack(ss_l), jnp.stack(gv_l))
```

```python
import functools
import math

import jax
import jax.numpy as jnp
from jax import lax
from jax.experimental import pallas as pl
from jax.experimental.pallas import tpu as pltpu

D_MODEL = 1024
DEPTH = 4
PAST_LEN = 8192
NORM_EPS = 1e-6
NEG_INF = -1e30

SWA_HEADS = 8
SWA_KV_HEADS = 2
SWA_HEAD_DIM = 64
SWA_GROUP = SWA_HEADS // SWA_KV_HEADS
SWA_WIDTH = SWA_HEADS * SWA_HEAD_DIM
SWA_KV_WIDTH = SWA_KV_HEADS * SWA_HEAD_DIM
WINDOW = 128
REL_BUCKETS = 32
REL_MAX_DIST = 128

GMLP_GROUPS = 4
GMLP_CHUNK = 128
GMLP_WIDTH = D_MODEL // 2
GMLP_GROUP_DIM = GMLP_WIDTH // GMLP_GROUPS

E_Q = 0
E_K = E_Q + SWA_WIDTH
E_V = E_K + SWA_KV_WIDTH
E_GA = E_V + SWA_KV_WIDTH
E_U = E_GA + SWA_WIDTH
E_VB = E_U + GMLP_WIDTH
E_GB = E_VB + GMLP_WIDTH
EVEN_IN = E_GB + GMLP_WIDTH
EVEN_MIX = SWA_WIDTH + GMLP_WIDTH

RET_HEADS = 4
RET_KEY_DIM = 256
RET_VALUE_DIM = 512
RET_QK_WIDTH = RET_HEADS * RET_KEY_DIM
RET_V_WIDTH = RET_HEADS * RET_VALUE_DIM
RET_CHUNK = 128
H_Q = 0
H_K = H_Q + RET_KEY_DIM
H_V = H_K + RET_KEY_DIM
H_G = H_V + RET_VALUE_DIM
HEAD_IN = H_G + RET_VALUE_DIM

LANES = 128
BLK = 128
VMEM_LIMIT = 56 * 1024 * 1024

F32 = jnp.float32
BF16 = jnp.bfloat16


def _ret_log_gamma(h):
    return math.log(1.0 - 2.0 ** (-5.0 - h))


def _dot(a, b):
    return jnp.dot(a, b, preferred_element_type=F32)


def _dot_nt(a, b):
    return lax.dot_general(a, b, (((1,), (1,)), ((), ())), preferred_element_type=F32)


def _dot_tn(a, b):
    return lax.dot_general(a, b, (((0,), (0,)), ((), ())), preferred_element_type=F32)


def _silu(x):
    return x * (1.0 / (1.0 + jnp.exp(-x)))


def _rmsnorm(x, g):
    ms = jnp.mean(x * x, axis=-1, keepdims=True)
    return x * lax.rsqrt(ms + NORM_EPS) * g


def _layernorm_nogain(x):
    mu = jnp.mean(x, axis=-1, keepdims=True)
    d = x - mu
    var = jnp.mean(d * d, axis=-1, keepdims=True)
    return d * lax.rsqrt(var + NORM_EPS)


def _iota(shape, dim):
    return lax.broadcasted_iota(jnp.int32, shape, dim)


def _t5_bias(dist, table_ref, head):
    n = jnp.maximum(dist, 0)
    max_exact = REL_BUCKETS // 2
    nf = jnp.maximum(n, 1).astype(F32)
    large = max_exact + (jnp.log(nf / max_exact) / math.log(REL_MAX_DIST / max_exact)
                         * (REL_BUCKETS - max_exact)).astype(jnp.int32)
    large = jnp.minimum(large, REL_BUCKETS - 1)
    bucket = jnp.where(n < max_exact, n, large)
    acc = jnp.zeros(dist.shape, F32)
    for b in range(REL_BUCKETS):
        acc = jnp.where(bucket == b, table_ref[b * SWA_HEADS + head], acc)
    return acc


def _split_heads_kv(x):
    lo = _iota(x.shape, 1) < SWA_HEAD_DIM
    xr = pltpu.roll(x, SWA_HEAD_DIM, 1)
    a0 = jnp.where(lo, x, 0.0).astype(BF16)
    b0 = jnp.where(lo, 0.0, xr).astype(BF16)
    a1 = jnp.where(lo, xr, 0.0).astype(BF16)
    b1 = jnp.where(lo, 0.0, x).astype(BF16)
    return (a0, b0), (a1, b1)


def _sink_softmax_parts(parts, sink):
    m = sink
    for t in parts:
        m = jnp.maximum(m, jnp.max(t, axis=-1, keepdims=True))
    es = [jnp.exp(t - m) for t in parts]
    den = jnp.exp(sink - m)
    for e in es:
        den = den + jnp.sum(e, axis=-1, keepdims=True)
    inv = 1.0 / den
    return [e * inv for e in es]


def _xpos_rotate(x, cos_f, sin_s):
    even = (_iota((x.shape[0], LANES), 1) % 2) == 0
    outs = []
    for c in range(x.shape[1] // LANES):
        sl = slice(c * LANES, (c + 1) * LANES)
        xs = x[:, sl]
        swapped = jnp.where(even, pltpu.roll(xs, LANES - 1, 1), pltpu.roll(xs, 1, 1))
        outs.append(xs * cos_f[:, sl] + swapped * sin_s[:, sl])
    return jnp.concatenate(outs, axis=1)


def _even_prompt_kernel(sinks_ref, table_ref, x_ref, g_ref, win_ref, wout_ref, ws_ref, bst_ref, lng_ref,
                        y_ref, newk_ref, newv_ref,
                        z_ref, mix_ref, ks_ref, vs_ref, bias_ref, wm_ref, *, tb):
    b = pl.program_id(0)
    j = pl.program_id(1)
    nsub = tb // BLK

    @pl.when(jnp.logical_and(b == 0, j == 0))
    def _init_tables():
        qi = _iota((BLK, 2 * BLK), 0)
        sj = _iota((BLK, 2 * BLK), 1)
        dist = qi + BLK - sj
        for h in range(SWA_HEADS):
            bias_ref[h] = _t5_bias(dist, table_ref, h)
        causal = _iota((BLK, BLK), 0) >= _iota((BLK, BLK), 1)
        for g in range(GMLP_GROUPS):
            wm_ref[g] = jnp.where(causal, ws_ref[g], 0.0).astype(BF16)

    @pl.when(j == 0)
    def _zero_prev():
        zeros = jnp.zeros((BLK, LANES), BF16)
        for h in range(SWA_KV_HEADS):
            for r in (0, 2 * BLK):
                ks_ref[h, r:r + BLK, :] = zeros
                vs_ref[h, r:r + BLK, :] = zeros

    x = x_ref[0]
    xn = _rmsnorm(x, g_ref[...]).astype(BF16)
    z_ref[...] = _dot(xn, win_ref[...])

    def sub(s, carry):
        r0 = pl.multiple_of(s * BLK, BLK)
        rows = pl.ds(r0, BLK)
        kparts = _split_heads_kv(z_ref[rows, E_K:E_K + SWA_KV_WIDTH])
        vparts = _split_heads_kv(z_ref[rows, E_V:E_V + SWA_KV_WIDTH])
        for h in range(SWA_KV_HEADS):
            ks_ref[h, BLK:2 * BLK, :] = kparts[h][0]
            ks_ref[h, 3 * BLK:4 * BLK, :] = kparts[h][1]
            vs_ref[h, BLK:2 * BLK, :] = vparts[h][0]
            vs_ref[h, 3 * BLK:4 * BLK, :] = vparts[h][1]

        qi = _iota((BLK, 2 * BLK), 0)
        sj = _iota((BLK, 2 * BLK), 1)
        dist = qi + BLK - sj
        kpos = (j * nsub + s - 1) * BLK + sj
        valid = (dist >= 0) & (dist < WINDOW) & (kpos >= 0)

        for h in range(SWA_KV_HEADS):
            kst = ks_ref[h]
            vst = vs_ref[h]
            for p in range(SWA_GROUP // 2):
                c0 = h * SWA_GROUP * SWA_HEAD_DIM + p * LANES
                qp = (z_ref[rows, E_Q + c0:E_Q + c0 + LANES] * (SWA_HEAD_DIM ** -0.5)).astype(BF16)
                sc = _dot_nt(qp, kst)
                probs = []
                for gi in range(2):
                    hd = h * SWA_GROUP + p * 2 + gi
                    t = sc[:, gi * 2 * BLK:(gi + 1) * 2 * BLK] + bias_ref[hd]
                    t = jnp.where(valid, t, NEG_INF)
                    (pr,) = _sink_softmax_parts([t], sinks_ref[hd])
                    probs.append(pr.astype(BF16))
                o = _dot(jnp.concatenate(probs, axis=1), vst)
                ga = z_ref[rows, E_GA + c0:E_GA + c0 + LANES]
                mix_ref[rows, c0:c0 + LANES] = (_silu(ga) * o).astype(BF16)

        for h in range(SWA_KV_HEADS):
            ks_ref[h, 0:BLK, :] = ks_ref[h, BLK:2 * BLK, :]
            ks_ref[h, 2 * BLK:3 * BLK, :] = ks_ref[h, 3 * BLK:4 * BLK, :]
            vs_ref[h, 0:BLK, :] = vs_ref[h, BLK:2 * BLK, :]
            vs_ref[h, 2 * BLK:3 * BLK, :] = vs_ref[h, 3 * BLK:4 * BLK, :]

        vn = (_layernorm_nogain(z_ref[rows, E_VB:E_VB + GMLP_WIDTH]) * lng_ref[...]).astype(BF16)
        for g in range(GMLP_GROUPS):
            gs = slice(g * GMLP_GROUP_DIM, (g + 1) * GMLP_GROUP_DIM)
            sp = _dot(wm_ref[g], vn[:, gs]) + bst_ref[g]
            u = z_ref[rows, E_U + g * GMLP_GROUP_DIM:E_U + (g + 1) * GMLP_GROUP_DIM]
            gb = z_ref[rows, E_GB + g * GMLP_GROUP_DIM:E_GB + (g + 1) * GMLP_GROUP_DIM]
            mix_ref[rows, SWA_WIDTH + g * GMLP_GROUP_DIM:SWA_WIDTH + (g + 1) * GMLP_GROUP_DIM] = (
                _silu(gb) * (u * sp)).astype(BF16)
        return carry

    lax.fori_loop(0, nsub, sub, 0)

    y_ref[0] = x + _dot(mix_ref[...], wout_ref[...])

    @pl.when(j == pl.num_programs(1) - 1)
    def _emit_cache():
        newk_ref[0] = z_ref[tb - WINDOW:tb, E_K:E_K + SWA_KV_WIDTH]
        newv_ref[0] = z_ref[tb - WINDOW:tb, E_V:E_V + SWA_KV_WIDTH]


def _const_spec(shape):
    nd = len(shape)
    return pl.BlockSpec(shape, lambda *_: (0,) * nd)


def _smem_spec():
    return pl.BlockSpec(memory_space=pltpu.SMEM)


def _even_prompt(x, g, win, wout, sinks, table_flat, ws, bst, lng, *, tb):
    B, S, D = x.shape
    kern = functools.partial(_even_prompt_kernel, tb=tb)
    return pl.pallas_call(
        kern,
        grid=(B, S // tb),
        in_specs=[
            _smem_spec(), _smem_spec(),
            pl.BlockSpec((1, tb, D), lambda b, j: (b, j, 0)),
            _const_spec((1, D)),
            _const_spec((D, EVEN_IN)),
            _const_spec((EVEN_MIX, D)),
            _const_spec((GMLP_GROUPS, GMLP_CHUNK, GMLP_CHUNK)),
            _const_spec((GMLP_GROUPS, GMLP_CHUNK, 1)),
            _const_spec((1, GMLP_WIDTH)),
        ],
        out_specs=[
            pl.BlockSpec((1, tb, D), lambda b, j: (b, j, 0)),
            pl.BlockSpec((1, WINDOW, SWA_KV_WIDTH), lambda b, j: (b, 0, 0)),
            pl.BlockSpec((1, WINDOW, SWA_KV_WIDTH), lambda b, j: (b, 0, 0)),
        ],
        out_shape=[
            jax.ShapeDtypeStruct((B, S, D), F32),
            jax.ShapeDtypeStruct((B, WINDOW, SWA_KV_WIDTH), F32),
            jax.ShapeDtypeStruct((B, WINDOW, SWA_KV_WIDTH), F32),
        ],
        scratch_shapes=[
            pltpu.VMEM((tb, EVEN_IN), F32),
            pltpu.VMEM((tb, EVEN_MIX), BF16),
            pltpu.VMEM((SWA_KV_HEADS, 4 * BLK, LANES), BF16),
            pltpu.VMEM((SWA_KV_HEADS, 4 * BLK, LANES), BF16),
            pltpu.VMEM((SWA_HEADS, BLK, 2 * BLK), F32),
            pltpu.VMEM((GMLP_GROUPS, GMLP_CHUNK, GMLP_CHUNK), BF16),
        ],
        compiler_params=pltpu.CompilerParams(
            dimension_semantics=("arbitrary", "arbitrary"), vmem_limit_bytes=VMEM_LIMIT),
        name="even_prompt",
    )(sinks, table_flat, x, g, win, wout, ws, bst, lng)


def _odd_prompt_kernel(x_ref, g_ref, win_ref, wout_ref, cos_ref, sin_ref, fg_ref,
                       y_ref, sout_ref,
                       xn_ref, zh_ref, s_ref, gated_ref, acc_ref, decay_ref, qdec_ref, kdec_ref,
                       *, tb, final_norm):
    b = pl.program_id(0)
    j = pl.program_id(1)
    nchunk = tb // BLK

    @pl.when(jnp.logical_and(b == 0, j == 0))
    def _init_tables():
        ii = _iota((BLK, BLK), 0)
        jj = _iota((BLK, BLK), 1)
        diff = (ii - jj).astype(F32)
        idx = _iota((BLK, 1), 0).astype(F32)
        for h in range(RET_HEADS):
            lg = _ret_log_gamma(h)
            decay_ref[h] = jnp.where(diff >= 0, jnp.exp(lg * jnp.maximum(diff, 0.0)), 0.0)
            qdec_ref[h] = jnp.exp(lg * (idx + 1.0))
            kdec_ref[h] = jnp.exp(lg * (BLK - 1.0 - idx))

    @pl.when(j == 0)
    def _zero_state():
        s_ref[...] = jnp.zeros(s_ref.shape, F32)

    x = x_ref[0]
    xn_ref[...] = _rmsnorm(x, g_ref[...]).astype(BF16)
    acc_ref[...] = x

    for h in range(RET_HEADS):
        cdec = math.exp(_ret_log_gamma(h) * BLK)
        zh_ref[...] = _dot(xn_ref[...], win_ref[h])

        def chunk(c, carry, h=h, cdec=cdec):
            rows = pl.ds(pl.multiple_of(c * BLK, BLK), BLK)
            cs = cos_ref[rows, :]
            sn = sin_ref[rows, :]
            qr = _xpos_rotate(zh_ref[rows, H_Q:H_Q + RET_KEY_DIM], cs, sn)
            kr = _xpos_rotate(zh_ref[rows, H_K:H_K + RET_KEY_DIM], cs, sn) * (RET_KEY_DIM ** -0.5)
            qb = qr.astype(BF16)
            kb = kr.astype(BF16)
            kd = (kr * kdec_ref[h]).astype(BF16)
            vb = zh_ref[rows, H_V:H_V + RET_VALUE_DIM].astype(BF16)
            att = _dot_nt(qb, kb) * decay_ref[h]
            intra = _dot(att.astype(BF16), vb)
            st = s_ref[h]
            cross = _dot(qb, st.astype(BF16)) * qdec_ref[h]
            s_ref[h] = cdec * st + _dot_tn(kd, vb)
            on = _layernorm_nogain(intra + cross)
            gate = zh_ref[rows, H_G:H_G + RET_VALUE_DIM]
            gated_ref[rows, :] = (_silu(gate) * on).astype(BF16)
            return carry

        lax.fori_loop(0, nchunk, chunk, 0)
        acc_ref[...] += _dot(gated_ref[...], wout_ref[h])

    y = acc_ref[...]
    if final_norm:
        y = _rmsnorm(y, fg_ref[...])
    y_ref[0] = y

    @pl.when(j == pl.num_programs(1) - 1)
    def _emit_state():
        sout_ref[0] = s_ref[...]


def _odd_prompt(x, g, win_h, wout_h, cos_f, sin_s, fg, *, tb, final_norm):
    B, S, D = x.shape
    kern = functools.partial(_odd_prompt_kernel, tb=tb, final_norm=final_norm)
    return pl.pallas_call(
        kern,
        grid=(B, S // tb),
        in_specs=[
            pl.BlockSpec((1, tb, D), lambda b, j: (b, j, 0)),
            _const_spec((1, D)),
            pl.BlockSpec((RET_HEADS, D, HEAD_IN), lambda b, j: (0, 0, 0), pipeline_mode=pl.Buffered(1)),
            pl.BlockSpec((RET_HEADS, RET_VALUE_DIM, D), lambda b, j: (0, 0, 0), pipeline_mode=pl.Buffered(1)),
            pl.BlockSpec((tb, RET_KEY_DIM), lambda b, j: (j, 0)),
            pl.BlockSpec((tb, RET_KEY_DIM), lambda b, j: (j, 0)),
            _const_spec((1, D)),
        ],
        out_specs=[
            pl.BlockSpec((1, tb, D), lambda b, j: (b, j, 0)),
            pl.BlockSpec((1, RET_HEADS, RET_KEY_DIM, RET_VALUE_DIM), lambda b, j: (b, 0, 0, 0)),
        ],
        out_shape=[
            jax.ShapeDtypeStruct((B, S, D), F32),
            jax.ShapeDtypeStruct((B, RET_HEADS, RET_KEY_DIM, RET_VALUE_DIM), F32),
        ],
        scratch_shapes=[
            pltpu.VMEM((tb, D), BF16),
            pltpu.VMEM((tb, HEAD_IN), F32),
            pltpu.VMEM((RET_HEADS, RET_KEY_DIM, RET_VALUE_DIM), F32),
            pltpu.VMEM((tb, RET_VALUE_DIM), BF16),
            pltpu.VMEM((tb, D), F32),
            pltpu.VMEM((RET_HEADS, BLK, BLK), F32),
            pltpu.VMEM((RET_HEADS, BLK, 1), F32),
            pltpu.VMEM((RET_HEADS, BLK, 1), F32),
        ],
        compiler_params=pltpu.CompilerParams(
            dimension_semantics=("arbitrary", "arbitrary"), vmem_limit_bytes=VMEM_LIMIT),
        name="odd_prompt",
    )(x, g, win_h, wout_h, cos_f, sin_s, fg)


def _same_batch_causal(shape, row0=0, col0=0, steps=4):
    r = _iota(shape, 0) + row0
    c = _iota(shape, 1) + col0
    return (r // steps == c // steps) & (c % steps <= r % steps)


def _even_sample_a_kernel(x_ref, g_ref, win_ref, wt_ref, bst_ref, lng_ref,
                          q_ref, k_ref, v_ref, sga_ref, mixb_ref, vn_ref):
    x = x_ref[...]
    xn = _rmsnorm(x, g_ref[...]).astype(BF16)
    z = _dot(xn, win_ref[...])
    q_ref[...] = z[:, E_Q:E_Q + SWA_WIDTH] * (SWA_HEAD_DIM ** -0.5)
    k_ref[...] = z[:, E_K:E_K + SWA_KV_WIDTH]
    v_ref[...] = z[:, E_V:E_V + SWA_KV_WIDTH]
    sga_ref[...] = _silu(z[:, E_GA:E_GA + SWA_WIDTH])
    vn = _layernorm_nogain(z[:, E_VB:E_VB + GMLP_WIDTH]) * lng_ref[...]
    vn_ref[...] = vn
    vnb = vn.astype(BF16)
    mask = _same_batch_causal((BLK, BLK))
    for g in range(GMLP_GROUPS):
        gs = slice(g * GMLP_GROUP_DIM, (g + 1) * GMLP_GROUP_DIM)
        wk = jnp.where(mask, wt_ref[g], 0.0).astype(BF16)
        sp = _dot(wk, vnb[:, gs]) + bst_ref[g]
        u = z[:, E_U + g * GMLP_GROUP_DIM:E_U + (g + 1) * GMLP_GROUP_DIM]
        gb = z[:, E_GB + g * GMLP_GROUP_DIM:E_GB + (g + 1) * GMLP_GROUP_DIM]
        mixb_ref[:, gs] = (_silu(gb) * (u * sp)).astype(BF16)


def _even_sample_a(x, g, win, wt, bst, lng):
    R, D = x.shape
    row = lambda w: pl.BlockSpec((BLK, w), lambda i: (i, 0))
    return pl.pallas_call(
        _even_sample_a_kernel,
        grid=(R // BLK,),
        in_specs=[row(D), _const_spec((1, D)), _const_spec((D, EVEN_IN)),
                  _const_spec((GMLP_GROUPS, BLK, BLK)), _const_spec((GMLP_GROUPS, BLK, 1)),
                  _const_spec((1, GMLP_WIDTH))],
        out_specs=[row(SWA_WIDTH), row(SWA_KV_WIDTH), row(SWA_KV_WIDTH), row(SWA_WIDTH),
                   row(GMLP_WIDTH), row(GMLP_WIDTH)],
        out_shape=[
            jax.ShapeDtypeStruct((R, SWA_WIDTH), F32),
            jax.ShapeDtypeStruct((R, SWA_KV_WIDTH), F32),
            jax.ShapeDtypeStruct((R, SWA_KV_WIDTH), F32),
            jax.ShapeDtypeStruct((R, SWA_WIDTH), F32),
            jax.ShapeDtypeStruct((R, GMLP_WIDTH), BF16),
            jax.ShapeDtypeStruct((R, GMLP_WIDTH), F32),
        ],
        compiler_params=pltpu.CompilerParams(
            dimension_semantics=("arbitrary",), vmem_limit_bytes=VMEM_LIMIT),
        name="even_sample_a",
    )(x, g, win, wt, bst, lng)


SAMPLE_BB = 32
PAIR_ROWS = 8


def _even_sample_b_kernel(sinks_ref, table_ref, q_ref, kn_ref, vn_ref, ck_ref, cv_ref,
                          attn_ref, newk_ref, newv_ref,
                          sn_ref, kns_ref, vns_ref, biasc_ref, biasn_ref, *, steps):
    kparts = _split_heads_kv(kn_ref[...])
    vparts = _split_heads_kv(vn_ref[...])
    for h in range(SWA_KV_HEADS):
        kns_ref[h] = jnp.concatenate(kparts[h], axis=0)
        vns_ref[h] = jnp.concatenate(vparts[h], axis=0)
        c0 = h * SWA_GROUP * SWA_HEAD_DIM
        lhs = jnp.concatenate([q_ref[:, c0:c0 + LANES], q_ref[:, c0 + LANES:c0 + 2 * LANES]], axis=0)
        sn_ref[h] = _dot_nt(lhs.astype(BF16), kns_ref[h])

    step_r = _iota((PAIR_ROWS, BLK), 0) % steps
    lane = _iota((PAIR_ROWS, BLK), 1)
    dist_c = step_r + WINDOW - lane
    valid_c = (dist_c >= 0) & (dist_c < WINDOW)
    dist_n = step_r - lane % steps
    first_local = _iota((PAIR_ROWS, 1), 0) < steps
    for hd in range(SWA_HEADS):
        biasc_ref[hd] = _t5_bias(dist_c, table_ref, hd)
        biasn_ref[hd] = _t5_bias(dist_n, table_ref, hd)

    def pair(i, carry):
        r0 = pl.multiple_of(i * PAIR_ROWS, PAIR_ROWS)
        rows = pl.ds(r0, PAIR_ROWS)
        valid_n = _same_batch_causal((PAIR_ROWS, BLK), row0=r0, steps=steps)
        kc = [_split_heads_kv(ck_ref[2 * i + e]) for e in range(2)]
        vc = [_split_heads_kv(cv_ref[2 * i + e]) for e in range(2)]
        for h in range(SWA_KV_HEADS):
            c0 = h * SWA_GROUP * SWA_HEAD_DIM
            lhs = jnp.concatenate([q_ref[rows, c0:c0 + LANES], q_ref[rows, c0 + LANES:c0 + 2 * LANES]],
                                  axis=0).astype(BF16)
            first16 = jnp.concatenate([first_local, first_local], axis=0)
            sc_e = [_dot_nt(lhs, jnp.concatenate(kc[e][h], axis=0)) for e in range(2)]
            sc = jnp.where(first16, sc_e[0], sc_e[1])
            pc_rows, pn_rows = [], []
            for p in range(2):
                pr = slice(p * PAIR_ROWS, (p + 1) * PAIR_ROWS)
                pc_cols, pn_cols = [], []
                for gi in range(2):
                    hd = h * SWA_GROUP + p * 2 + gi
                    gc = slice(gi * BLK, (gi + 1) * BLK)
                    tc = jnp.where(valid_c, sc[pr, gc] + biasc_ref[hd], NEG_INF)
                    tn_raw = sn_ref[h, pl.ds(p * BLK + r0, PAIR_ROWS), gc]
                    tn = jnp.where(valid_n, tn_raw + biasn_ref[hd], NEG_INF)
                    pc, pn = _sink_softmax_parts([tc, tn], sinks_ref[hd])
                    pc_cols.append(pc)
                    pn_cols.append(pn)
                pc_rows.append(jnp.concatenate(pc_cols, axis=1))
                pn_rows.append(jnp.concatenate(pn_cols, axis=1))
            pc16 = jnp.concatenate(pc_rows, axis=0).astype(BF16)
            pn16 = jnp.concatenate(pn_rows, axis=0).astype(BF16)
            o_e = [_dot(pc16, jnp.concatenate(vc[e][h], axis=0)) for e in range(2)]
            o = jnp.where(first16, o_e[0], o_e[1]) + _dot(pn16, vns_ref[h])
            attn_ref[rows, c0:c0 + LANES] = o[0:PAIR_ROWS]
            attn_ref[rows, c0 + LANES:c0 + 2 * LANES] = o[PAIR_ROWS:2 * PAIR_ROWS]
        for e in range(2):
            be = 2 * i + e
            newk_ref[be, 0:WINDOW - steps, :] = ck_ref[be, steps:WINDOW, :]
            newv_ref[be, 0:WINDOW - steps, :] = cv_ref[be, steps:WINDOW, :]
            newk_ref[be, WINDOW - steps:WINDOW, :] = kn_ref[pl.ds(r0 + e * steps, steps), :]
            newv_ref[be, WINDOW - steps:WINDOW, :] = vn_ref[pl.ds(r0 + e * steps, steps), :]
        return carry

    lax.fori_loop(0, SAMPLE_BB // 2, pair, 0)


def _even_sample_b(q, kn, vn, ck, cv, sinks, table_flat, *, steps):
    R = q.shape[0]
    DB = ck.shape[0]
    row = lambda w: pl.BlockSpec((BLK, w), lambda i: (i, 0))
    cache = pl.BlockSpec((SAMPLE_BB, WINDOW, SWA_KV_WIDTH), lambda i: (i, 0, 0))
    kern = functools.partial(_even_sample_b_kernel, steps=steps)
    return pl.pallas_call(
        kern,
        grid=(R // BLK,),
        in_specs=[_smem_spec(), _smem_spec(), row(SWA_WIDTH), row(SWA_KV_WIDTH), row(SWA_KV_WIDTH), cache, cache],
        out_specs=[row(SWA_WIDTH), cache, cache],
        out_shape=[
            jax.ShapeDtypeStruct((R, SWA_WIDTH), F32),
            jax.ShapeDtypeStruct((DB, WINDOW, SWA_KV_WIDTH), F32),
            jax.ShapeDtypeStruct((DB, WINDOW, SWA_KV_WIDTH), F32),
        ],
        scratch_shapes=[
            pltpu.VMEM((SWA_KV_HEADS, 2 * BLK, 2 * BLK), F32),
            pltpu.VMEM((SWA_KV_HEADS, 2 * BLK, LANES), BF16),
            pltpu.VMEM((SWA_KV_HEADS, 2 * BLK, LANES), BF16),
            pltpu.VMEM((SWA_HEADS, PAIR_ROWS, BLK), F32),
            pltpu.VMEM((SWA_HEADS, PAIR_ROWS, BLK), F32),
        ],
        compiler_params=pltpu.CompilerParams(
            dimension_semantics=("arbitrary",), vmem_limit_bytes=VMEM_LIMIT),
        name="even_sample_b",
    )(sinks, table_flat, q, kn, vn, ck, cv)


def _even_sample_c_kernel(x_ref, sga_ref, attn_ref, mixb_ref, wout_ref, y_ref):
    mixa = (sga_ref[...] * attn_ref[...]).astype(BF16)
    mix = jnp.concatenate([mixa, mixb_ref[...]], axis=1)
    y_ref[...] = x_ref[...] + _dot(mix, wout_ref[...])


def _even_sample_c(x, sga, attn, mixb, wout):
    R, D = x.shape
    row = lambda w: pl.BlockSpec((BLK, w), lambda i: (i, 0))
    return pl.pallas_call(
        _even_sample_c_kernel,
        grid=(R // BLK,),
        in_specs=[row(D), row(SWA_WIDTH), row(SWA_WIDTH), row(GMLP_WIDTH), _const_spec((EVEN_MIX, D))],
        out_specs=row(D),
        out_shape=jax.ShapeDtypeStruct((R, D), F32),
        compiler_params=pltpu.CompilerParams(
            dimension_semantics=("arbitrary",), vmem_limit_bytes=VMEM_LIMIT),
        name="even_sample_c",
    )(x, sga, attn, mixb, wout)


def _odd_sample_a_kernel(x_ref, g_ref, win_ref, cos_ref, sin_ref,
                         q_ref, kdt_ref, v_ref, intra_ref, gate_ref, *, steps):
    x = x_ref[...]
    xn = _rmsnorm(x, g_ref[...]).astype(BF16)
    cs = cos_ref[...]
    sn = sin_ref[...]
    rr = _iota((BLK, BLK), 0)
    cc = _iota((BLK, BLK), 1)
    mask = _same_batch_causal((BLK, BLK), steps=steps)
    diff = (rr % steps - cc % steps).astype(F32)
    step_col = (_iota((BLK, 1), 0) % steps).astype(F32)
    for h in range(RET_HEADS):
        lg = _ret_log_gamma(h)
        z = _dot(xn, win_ref[h])
        qr = _xpos_rotate(z[:, H_Q:H_Q + RET_KEY_DIM], cs, sn)
        kr = _xpos_rotate(z[:, H_K:H_K + RET_KEY_DIM], cs, sn) * (RET_KEY_DIM ** -0.5)
        qb = qr.astype(BF16)
        vb = z[:, H_V:H_V + RET_VALUE_DIM].astype(BF16)
        decay = jnp.where(mask, jnp.exp(lg * jnp.maximum(diff, 0.0)), 0.0)
        att = _dot_nt(qb, kr.astype(BF16)) * decay
        intra_ref[:, h * RET_VALUE_DIM:(h + 1) * RET_VALUE_DIM] = _dot(att.astype(BF16), vb)
        kd = kr * jnp.exp(lg * (steps - 1.0 - step_col))
        kdt_ref[h] = kd.T.astype(BF16)
        q_ref[:, h * RET_KEY_DIM:(h + 1) * RET_KEY_DIM] = qb
        v_ref[:, h * RET_VALUE_DIM:(h + 1) * RET_VALUE_DIM] = vb
        gate_ref[:, h * RET_VALUE_DIM:(h + 1) * RET_VALUE_DIM] = _silu(z[:, H_G:H_G + RET_VALUE_DIM])


def _odd_sample_a(x, g, win_h, cos_f, sin_s, *, steps):
    R, D = x.shape
    row = lambda w: pl.BlockSpec((BLK, w), lambda i: (i, 0))
    kern = functools.partial(_odd_sample_a_kernel, steps=steps)
    return pl.pallas_call(
        kern,
        grid=(R // BLK,),
        in_specs=[row(D), _const_spec((1, D)),
                  pl.BlockSpec((RET_HEADS, D, HEAD_IN), lambda i: (0, 0, 0), pipeline_mode=pl.Buffered(1)),
                  row(RET_KEY_DIM), row(RET_KEY_DIM)],
        out_specs=[row(RET_QK_WIDTH),
                   pl.BlockSpec((RET_HEADS, RET_KEY_DIM, BLK), lambda i: (0, 0, i)),
                   row(RET_V_WIDTH), row(RET_V_WIDTH), row(RET_V_WIDTH)],
        out_shape=[
            jax.ShapeDtypeStruct((R, RET_QK_WIDTH), BF16),
            jax.ShapeDtypeStruct((RET_HEADS, RET_KEY_DIM, R), BF16),
            jax.ShapeDtypeStruct((R, RET_V_WIDTH), BF16),
            jax.ShapeDtypeStruct((R, RET_V_WIDTH), F32),
            jax.ShapeDtypeStruct((R, RET_V_WIDTH), F32),
        ],
        compiler_params=pltpu.CompilerParams(
            dimension_semantics=("arbitrary",), vmem_limit_bytes=VMEM_LIMIT),
        name="odd_sample_a",
    )(x, g, win_h, cos_f, sin_s)


STATE_BB = 4
STATE_HB = 2


def _odd_sample_b_kernel(q_ref, kdt_ref, v_ref, s_ref, cross_ref, snew_ref, *, steps):
    blk = pl.program_id(0)
    hg = pl.program_id(1)
    rows16 = STATE_BB * steps
    lane_base = (blk % (BLK // rows16)) * rows16
    lane = _iota((RET_KEY_DIM, BLK), 1)
    row16 = _iota((rows16, 1), 0)
    v = v_ref[...]
    for hl in range(STATE_HB):
        lg = jnp.float32(_ret_log_gamma(hl))
        for g in range(1, RET_HEADS // STATE_HB):
            lg = jnp.where(hg == g, jnp.float32(_ret_log_gamma(g * STATE_HB + hl)), lg)
        qdec = jnp.exp(lg * ((row16 % steps).astype(F32) + 1.0))
        cdec = jnp.exp(jnp.full((1, 1), lg * steps, F32))
        q = q_ref[:, hl * RET_KEY_DIM:(hl + 1) * RET_KEY_DIM]
        kdt = kdt_ref[hl]
        vh = v[:, hl * RET_VALUE_DIM:(hl + 1) * RET_VALUE_DIM]
        cross = jnp.zeros((rows16, RET_VALUE_DIM), F32)
        for bl in range(STATE_BB):
            st = s_ref[bl, hl]
            cr = _dot(q, st.astype(BF16))
            cross = jnp.where(row16 // steps == bl, cr, cross)
            mine = (lane - lane_base) // steps == bl
            upd = _dot(jnp.where(mine, kdt, jnp.zeros_like(kdt)), vh)
            snew_ref[bl, hl] = cdec * st + upd
        cross_ref[:, hl * RET_VALUE_DIM:(hl + 1) * RET_VALUE_DIM] = cross * qdec


def _odd_sample_b(q, kdt, v, state, *, steps):
    R = q.shape[0]
    DB = state.shape[0]
    rows16 = STATE_BB * steps
    per_lane_blk = BLK // rows16
    kern = functools.partial(_odd_sample_b_kernel, steps=steps)
    st_spec = pl.BlockSpec((STATE_BB, STATE_HB, RET_KEY_DIM, RET_VALUE_DIM), lambda i, h: (i, h, 0, 0))
    return pl.pallas_call(
        kern,
        grid=(DB // STATE_BB, RET_HEADS // STATE_HB),
        in_specs=[
            pl.BlockSpec((rows16, STATE_HB * RET_KEY_DIM), lambda i, h: (i, h)),
            pl.BlockSpec((STATE_HB, RET_KEY_DIM, BLK), lambda i, h: (h, 0, i // per_lane_blk)),
            pl.BlockSpec((BLK, STATE_HB * RET_VALUE_DIM), lambda i, h: (i // per_lane_blk, h)),
            st_spec,
        ],
        out_specs=[
            pl.BlockSpec((rows16, STATE_HB * RET_VALUE_DIM), lambda i, h: (i, h)),
            st_spec,
        ],
        out_shape=[
            jax.ShapeDtypeStruct((R, RET_V_WIDTH), F32),
            jax.ShapeDtypeStruct(state.shape, F32),
        ],
        compiler_params=pltpu.CompilerParams(
            dimension_semantics=("arbitrary", "arbitrary"), vmem_limit_bytes=VMEM_LIMIT),
        name="odd_sample_b",
    )(q, kdt, v, state)


def _odd_sample_c_kernel(x_ref, intra_ref, cross_ref, gate_ref, wout_ref, fg_ref, y_ref, *, final_norm):
    y = x_ref[...]
    for h in range(RET_HEADS):
        hs = slice(h * RET_VALUE_DIM, (h + 1) * RET_VALUE_DIM)
        on = _layernorm_nogain(intra_ref[:, hs] + cross_ref[:, hs])
        y = y + _dot((gate_ref[:, hs] * on).astype(BF16), wout_ref[h])
    if final_norm:
        y = _rmsnorm(y, fg_ref[...])
    y_ref[...] = y


def _odd_sample_c(x, intra, cross, gate, wout_h, fg, *, final_norm):
    R, D = x.shape
    row = lambda w: pl.BlockSpec((BLK, w), lambda i: (i, 0))
    kern = functools.partial(_odd_sample_c_kernel, final_norm=final_norm)
    return pl.pallas_call(
        kern,
        grid=(R // BLK,),
        in_specs=[row(D), row(RET_V_WIDTH), row(RET_V_WIDTH), row(RET_V_WIDTH),
                  _const_spec((RET_HEADS, RET_VALUE_DIM, D)), _const_spec((1, D))],
        out_specs=row(D),
        out_shape=jax.ShapeDtypeStruct((R, D), F32),
        compiler_params=pltpu.CompilerParams(
            dimension_semantics=("arbitrary",), vmem_limit_bytes=VMEM_LIMIT),
        name="odd_sample_c",
    )(x, intra, cross, gate, wout_h, fg)


def _xpos_tables(pos):
    angle = 1.0 / (10000.0 ** jnp.linspace(0.0, 1.0, RET_KEY_DIM // 2, dtype=F32))
    ang = pos.astype(F32)[:, None] * angle[None, :]
    sin = jnp.sin(ang)
    cos = jnp.cos(ang)
    cos_f = jnp.repeat(cos, 2, axis=-1)
    sin_s = jnp.stack([-sin, sin], axis=-1).reshape(pos.shape[0], RET_KEY_DIM)
    return cos_f, sin_s


def _regroup_odd_w_in(w):
    q = w[:, 0:RET_QK_WIDTH].reshape(D_MODEL, RET_HEADS, RET_KEY_DIM)
    k = w[:, RET_QK_WIDTH:2 * RET_QK_WIDTH].reshape(D_MODEL, RET_HEADS, RET_KEY_DIM)
    v = w[:, 2 * RET_QK_WIDTH:2 * RET_QK_WIDTH + RET_V_WIDTH].reshape(D_MODEL, RET_HEADS, RET_VALUE_DIM)
    g = w[:, 2 * RET_QK_WIDTH + RET_V_WIDTH:].reshape(D_MODEL, RET_HEADS, RET_VALUE_DIM)
    return jnp.concatenate([q, k, v, g], axis=-1).transpose(1, 0, 2).astype(BF16)


PROMPT_TB_EVEN = 512
PROMPT_TB_ODD = 512


def kernel(x_prompt, x_sample, cache_swa_k, cache_swa_v, state_ret, norm_gain, final_norm_gain,
           rel_bias_table, even_w_in, even_w_out, swa_sinks, gmlp_ws, gmlp_bs, gmlp_ln_gain,
           odd_w_in, odd_w_out):
    B, S, D = x_prompt.shape
    DB, T, _ = x_sample.shape
    R = DB * T
    n_even = even_w_in.shape[0]

    table_flat = rel_bias_table.reshape(-1)
    fg = final_norm_gain.reshape(1, D)
    cos_p, sin_p = _xpos_tables(jnp.arange(S, dtype=jnp.int32))
    cos_s, sin_s = _xpos_tables(PAST_LEN + jnp.arange(T, dtype=jnp.int32))
    cos_s = jnp.tile(cos_s, (DB, 1))
    sin_s = jnp.tile(sin_s, (DB, 1))

    yp = x_prompt
    ys = x_sample.reshape(R, D)
    kp_l, vp_l, ks_l, vs_l, sp_l, ss_l, gv_l = [], [], [], [], [], [], []
    for layer in range(DEPTH):
        g = norm_gain[layer].reshape(1, D)
        if layer % 2 == 0:
            e = layer // 2
            win = even_w_in[e].astype(BF16)
            wout = even_w_out[e].astype(BF16)
            lng = gmlp_ln_gain[e].reshape(1, GMLP_WIDTH)
            yp, kp, vp = _even_prompt(yp, g, win, wout, swa_sinks[e], table_flat, gmlp_ws[e],
                                      gmlp_bs[e][:, :, None], lng, tb=PROMPT_TB_EVEN)
            wt = jnp.tile(gmlp_ws[e][:, :T, :T], (1, BLK // T, BLK // T))
            bst = jnp.tile(gmlp_bs[e][:, :T], (1, BLK // T))[:, :, None]
            q, kn, vn_, sga, mixb, gv = _even_sample_a(ys, g, win, wt, bst, lng)
            attn, nk, nv = _even_sample_b(q, kn, vn_,
                                          cache_swa_k[e].reshape(DB, WINDOW, SWA_KV_WIDTH),
                                          cache_swa_v[e].reshape(DB, WINDOW, SWA_KV_WIDTH),
                                          swa_sinks[e], table_flat, steps=T)
            ys = _even_sample_c(ys, sga, attn, mixb, wout)
            kp_l.append(kp.reshape(B, WINDOW, SWA_KV_HEADS, SWA_HEAD_DIM))
            vp_l.append(vp.reshape(B, WINDOW, SWA_KV_HEADS, SWA_HEAD_DIM))
            ks_l.append(nk.reshape(DB, WINDOW, SWA_KV_HEADS, SWA_HEAD_DIM))
            vs_l.append(nv.reshape(DB, WINDOW, SWA_KV_HEADS, SWA_HEAD_DIM))
            gv_l.append(gv.reshape(DB, T, GMLP_WIDTH))
        else:
            o = layer // 2
            last = layer == DEPTH - 1
            win_h = _regroup_odd_w_in(odd_w_in[o])
            wout_h = odd_w_out[o].reshape(RET_HEADS, RET_VALUE_DIM, D).astype(BF16)
            yp, sp = _odd_prompt(yp, g, win_h, wout_h, cos_p, sin_p, fg, tb=PROMPT_TB_ODD, final_norm=last)
            q, kdt, v, intra, gate = _odd_sample_a(ys, g, win_h, cos_s, sin_s, steps=T)
            cross, ss = _odd_sample_b(q, kdt, v, state_ret[o], steps=T)
            ys = _odd_sample_c(ys, intra, cross, gate, wout_h, fg, final_norm=last)
            sp_l.append(sp)
            ss_l.append(ss)
    return (yp, ys.reshape(DB, T, D), jnp.stack(kp_l), jnp.stack(vp_l), jnp.stack(ks_l), jnp.stack(vs_l),
            jnp.stack(sp_l), jnp.stack(ss_l), jnp.stack(gv_l))
```

```python
import functools
import math

import jax
import jax.numpy as jnp
from jax import lax
from jax.experimental import pallas as pl
from jax.experimental.pallas import tpu as pltpu

D_MODEL = 1024
DEPTH = 4
PAST_LEN = 8192
NORM_EPS = 1e-6
NEG_INF = -1e30

SWA_HEADS = 8
SWA_KV_HEADS = 2
SWA_HEAD_DIM = 64
SWA_GROUP = SWA_HEADS // SWA_KV_HEADS
SWA_WIDTH = SWA_HEADS * SWA_HEAD_DIM
SWA_KV_WIDTH = SWA_KV_HEADS * SWA_HEAD_DIM
WINDOW = 128
REL_BUCKETS = 32
REL_MAX_DIST = 128

GMLP_GROUPS = 4
GMLP_CHUNK = 128
GMLP_WIDTH = D_MODEL // 2
GMLP_GROUP_DIM = GMLP_WIDTH // GMLP_GROUPS

E_Q = 0
E_K = E_Q + SWA_WIDTH
E_V = E_K + SWA_KV_WIDTH
E_GA = E_V + SWA_KV_WIDTH
E_U = E_GA + SWA_WIDTH
E_VB = E_U + GMLP_WIDTH
E_GB = E_VB + GMLP_WIDTH
EVEN_IN = E_GB + GMLP_WIDTH
EVEN_MIX = SWA_WIDTH + GMLP_WIDTH

RET_HEADS = 4
RET_KEY_DIM = 256
RET_VALUE_DIM = 512
RET_QK_WIDTH = RET_HEADS * RET_KEY_DIM
RET_V_WIDTH = RET_HEADS * RET_VALUE_DIM
RET_CHUNK = 128
H_Q = 0
H_K = H_Q + RET_KEY_DIM
H_V = H_K + RET_KEY_DIM
H_G = H_V + RET_VALUE_DIM
HEAD_IN = H_G + RET_VALUE_DIM

LANES = 128
BLK = 128
VMEM_LIMIT = 56 * 1024 * 1024

F32 = jnp.float32
BF16 = jnp.bfloat16


def _ret_log_gamma(h):
    return math.log(1.0 - 2.0 ** (-5.0 - h))


def _dot(a, b):
    return jnp.dot(a, b, preferred_element_type=F32)


def _dot_nt(a, b):
    return lax.dot_general(a, b, (((1,), (1,)), ((), ())), preferred_element_type=F32)


def _dot_tn(a, b):
    return lax.dot_general(a, b, (((0,), (0,)), ((), ())), preferred_element_type=F32)


def _silu(x):
    return x * (1.0 / (1.0 + jnp.exp(-x)))


def _rmsnorm(x, g):
    ms = jnp.mean(x * x, axis=-1, keepdims=True)
    return x * lax.rsqrt(ms + NORM_EPS) * g


def _layernorm_nogain(x):
    mu = jnp.mean(x, axis=-1, keepdims=True)
    d = x - mu
    var = jnp.mean(d * d, axis=-1, keepdims=True)
    return d * lax.rsqrt(var + NORM_EPS)


def _iota(shape, dim):
    return lax.broadcasted_iota(jnp.int32, shape, dim)


def _t5_bias(dist, table_ref, head):
    n = jnp.maximum(dist, 0)
    max_exact = REL_BUCKETS // 2
    nf = jnp.maximum(n, 1).astype(F32)
    large = max_exact + (jnp.log(nf / max_exact) / math.log(REL_MAX_DIST / max_exact)
                         * (REL_BUCKETS - max_exact)).astype(jnp.int32)
    large = jnp.minimum(large, REL_BUCKETS - 1)
    bucket = jnp.where(n < max_exact, n, large)
    acc = jnp.zeros(dist.shape, F32)
    for b in range(REL_BUCKETS):
        acc = jnp.where(bucket == b, table_ref[b * SWA_HEADS + head], acc)
    return acc


def _split_heads_kv(x):
    lo = _iota(x.shape, 1) < SWA_HEAD_DIM
    xr = pltpu.roll(x, SWA_HEAD_DIM, 1)
    a0 = jnp.where(lo, x, 0.0).astype(BF16)
    b0 = jnp.where(lo, 0.0, xr).astype(BF16)
    a1 = jnp.where(lo, xr, 0.0).astype(BF16)
    b1 = jnp.where(lo, 0.0, x).astype(BF16)
    return (a0, b0), (a1, b1)


def _sink_softmax_parts(parts, sink):
    m = sink
    for t in parts:
        m = jnp.maximum(m, jnp.max(t, axis=-1, keepdims=True))
    es = [jnp.exp(t - m) for t in parts]
    den = jnp.exp(sink - m)
    for e in es:
        den = den + jnp.sum(e, axis=-1, keepdims=True)
    inv = 1.0 / den
    return [e * inv for e in es]


def _xpos_rotate(x, cos_f, sin_s):
    even = (_iota((x.shape[0], LANES), 1) % 2) == 0
    outs = []
    for c in range(x.shape[1] // LANES):
        sl = slice(c * LANES, (c + 1) * LANES)
        xs = x[:, sl]
        swapped = jnp.where(even, pltpu.roll(xs, LANES - 1, 1), pltpu.roll(xs, 1, 1))
        outs.append(xs * cos_f[:, sl] + swapped * sin_s[:, sl])
    return jnp.concatenate(outs, axis=1)


def _even_prompt_kernel(sinks_ref, table_ref, x_ref, g_ref, win_ref, wout_ref, ws_ref, bst_ref, lng_ref,
                        y_ref, newk_ref, newv_ref,
                        z_ref, mix_ref, ks_ref, vs_ref, bias_ref, wm_ref, *, tb):
    b = pl.program_id(0)
    j = pl.program_id(1)
    nsub = tb // BLK

    @pl.when(jnp.logical_and(b == 0, j == 0))
    def _init_tables():
        qi = _iota((BLK, 2 * BLK), 0)
        sj = _iota((BLK, 2 * BLK), 1)
        dist = qi + BLK - sj
        for h in range(SWA_HEADS):
            bias_ref[h] = _t5_bias(dist, table_ref, h)
        causal = _iota((BLK, BLK), 0) >= _iota((BLK, BLK), 1)
        for g in range(GMLP_GROUPS):
            wm_ref[g] = jnp.where(causal, ws_ref[g], 0.0).astype(BF16)

    @pl.when(j == 0)
    def _zero_prev():
        zeros = jnp.zeros((BLK, LANES), BF16)
        for h in range(SWA_KV_HEADS):
            for r in (0, 2 * BLK):
                ks_ref[h, r:r + BLK, :] = zeros
                vs_ref[h, r:r + BLK, :] = zeros

    x = x_ref[0]
    xn = _rmsnorm(x, g_ref[...]).astype(BF16)
    z_ref[...] = _dot(xn, win_ref[...])

    def sub(s, carry):
        r0 = pl.multiple_of(s * BLK, BLK)
        rows = pl.ds(r0, BLK)
        kparts = _split_heads_kv(z_ref[rows, E_K:E_K + SWA_KV_WIDTH])
        vparts = _split_heads_kv(z_ref[rows, E_V:E_V + SWA_KV_WIDTH])
        for h in range(SWA_KV_HEADS):
            ks_ref[h, BLK:2 * BLK, :] = kparts[h][0]
            ks_ref[h, 3 * BLK:4 * BLK, :] = kparts[h][1]
            vs_ref[h, BLK:2 * BLK, :] = vparts[h][0]
            vs_ref[h, 3 * BLK:4 * BLK, :] = vparts[h][1]

        qi = _iota((BLK, 2 * BLK), 0)
        sj = _iota((BLK, 2 * BLK), 1)
        dist = qi + BLK - sj
        kpos = (j * nsub + s - 1) * BLK + sj
        valid = (dist >= 0) & (dist < WINDOW) & (kpos >= 0)

        for h in range(SWA_KV_HEADS):
            kst = ks_ref[h]
            vst = vs_ref[h]
            for p in range(SWA_GROUP // 2):
                c0 = h * SWA_GROUP * SWA_HEAD_DIM + p * LANES
                qp = (z_ref[rows, E_Q + c0:E_Q + c0 + LANES] * (SWA_HEAD_DIM ** -0.5)).astype(BF16)
                sc = _dot_nt(qp, kst)
                probs = []
                for gi in range(2):
                    hd = h * SWA_GROUP + p * 2 + gi
                    t = sc[:, gi * 2 * BLK:(gi + 1) * 2 * BLK] + bias_ref[hd]
                    t = jnp.where(valid, t, NEG_INF)
                    (pr,) = _sink_softmax_parts([t], sinks_ref[hd])
                    probs.append(pr.astype(BF16))
                o = _dot(jnp.concatenate(probs, axis=1), vst)
                ga = z_ref[rows, E_GA + c0:E_GA + c0 + LANES]
                mix_ref[rows, c0:c0 + LANES] = (_silu(ga) * o).astype(BF16)

        for h in range(SWA_KV_HEADS):
            ks_ref[h, 0:BLK, :] = ks_ref[h, BLK:2 * BLK, :]
            ks_ref[h, 2 * BLK:3 * BLK, :] = ks_ref[h, 3 * BLK:4 * BLK, :]
            vs_ref[h, 0:BLK, :] = vs_ref[h, BLK:2 * BLK, :]
            vs_ref[h, 2 * BLK:3 * BLK, :] = vs_ref[h, 3 * BLK:4 * BLK, :]

        vn = (_layernorm_nogain(z_ref[rows, E_VB:E_VB + GMLP_WIDTH]) * lng_ref[...]).astype(BF16)
        for g in range(GMLP_GROUPS):
            gs = slice(g * GMLP_GROUP_DIM, (g + 1) * GMLP_GROUP_DIM)
            sp = _dot(wm_ref[g], vn[:, gs]) + bst_ref[g]
            u = z_ref[rows, E_U + g * GMLP_GROUP_DIM:E_U + (g + 1) * GMLP_GROUP_DIM]
            gb = z_ref[rows, E_GB + g * GMLP_GROUP_DIM:E_GB + (g + 1) * GMLP_GROUP_DIM]
            mix_ref[rows, SWA_WIDTH + g * GMLP_GROUP_DIM:SWA_WIDTH + (g + 1) * GMLP_GROUP_DIM] = (
                _silu(gb) * (u * sp)).astype(BF16)
        return carry

    lax.fori_loop(0, nsub, sub, 0)

    y_ref[0] = x + _dot(mix_ref[...], wout_ref[...])

    @pl.when(j == pl.num_programs(1) - 1)
    def _emit_cache():
        newk_ref[0] = z_ref[tb - WINDOW:tb, E_K:E_K + SWA_KV_WIDTH]
        newv_ref[0] = z_ref[tb - WINDOW:tb, E_V:E_V + SWA_KV_WIDTH]


def _const_spec(shape):
    nd = len(shape)
    return pl.BlockSpec(shape, lambda *_: (0,) * nd)


def _smem_spec():
    return pl.BlockSpec(memory_space=pltpu.SMEM)


def _even_prompt(x, g, win, wout, sinks, table_flat, ws, bst, lng, *, tb):
    B, S, D = x.shape
    kern = functools.partial(_even_prompt_kernel, tb=tb)
    return pl.pallas_call(
        kern,
        grid=(B, S // tb),
        in_specs=[
            _smem_spec(), _smem_spec(),
            pl.BlockSpec((1, tb, D), lambda b, j: (b, j, 0)),
            _const_spec((1, D)),
            _const_spec((D, EVEN_IN)),
            _const_spec((EVEN_MIX, D)),
            _const_spec((GMLP_GROUPS, GMLP_CHUNK, GMLP_CHUNK)),
            _const_spec((GMLP_GROUPS, GMLP_CHUNK, 1)),
            _const_spec((1, GMLP_WIDTH)),
        ],
        out_specs=[
            pl.BlockSpec((1, tb, D), lambda b, j: (b, j, 0)),
            pl.BlockSpec((1, WINDOW, SWA_KV_WIDTH), lambda b, j: (b, 0, 0)),
            pl.BlockSpec((1, WINDOW, SWA_KV_WIDTH), lambda b, j: (b, 0, 0)),
        ],
        out_shape=[
            jax.ShapeDtypeStruct((B, S, D), F32),
            jax.ShapeDtypeStruct((B, WINDOW, SWA_KV_WIDTH), F32),
            jax.ShapeDtypeStruct((B, WINDOW, SWA_KV_WIDTH), F32),
        ],
        scratch_shapes=[
            pltpu.VMEM((tb, EVEN_IN), F32),
            pltpu.VMEM((tb, EVEN_MIX), BF16),
            pltpu.VMEM((SWA_KV_HEADS, 4 * BLK, LANES), BF16),
            pltpu.VMEM((SWA_KV_HEADS, 4 * BLK, LANES), BF16),
            pltpu.VMEM((SWA_HEADS, BLK, 2 * BLK), F32),
            pltpu.VMEM((GMLP_GROUPS, GMLP_CHUNK, GMLP_CHUNK), BF16),
        ],
        compiler_params=pltpu.CompilerParams(
            dimension_semantics=("arbitrary", "arbitrary"), vmem_limit_bytes=VMEM_LIMIT),
        name="even_prompt",
    )(sinks, table_flat, x, g, win, wout, ws, bst, lng)


def _xpos_rotate_split(x, cos, sin):
    half = x.shape[1] // 2
    x0 = x[:, :half]
    x1 = x[:, half:]
    return jnp.concatenate([x0 * cos - x1 * sin, x1 * cos + x0 * sin], axis=1)


def _odd_prompt_kernel(x_ref, g_ref, win_ref, wout_ref, cos_ref, sin_ref, fg_ref,
                       y_ref, sout_ref,
                       xn_ref, z_ref, s_ref, gated_ref, decay_ref, qdec_ref, kdec_ref,
                       *, tb, final_norm):
    b = pl.program_id(0)
    j = pl.program_id(1)
    nchunk = tb // BLK

    @pl.when(jnp.logical_and(b == 0, j == 0))
    def _init_tables():
        ii = _iota((BLK, BLK), 0)
        jj = _iota((BLK, BLK), 1)
        diff = (ii - jj).astype(F32)
        idx = _iota((BLK, 1), 0).astype(F32)
        for h in range(RET_HEADS):
            lg = _ret_log_gamma(h)
            decay_ref[h] = jnp.where(diff >= 0, jnp.exp(lg * jnp.maximum(diff, 0.0)), 0.0)
            qdec_ref[h] = jnp.exp(lg * (idx + 1.0))
            kdec_ref[h] = jnp.exp(lg * (BLK - 1.0 - idx))

    @pl.when(j == 0)
    def _zero_state():
        s_ref[...] = jnp.zeros(s_ref.shape, F32)

    xn_ref[...] = _rmsnorm(x_ref[0], g_ref[...]).astype(BF16)
    for h in range(RET_HEADS):
        z_ref[h] = _dot(xn_ref[...], win_ref[h])

    def chunk(c, carry):
        rows = pl.ds(pl.multiple_of(c * BLK, BLK), BLK)
        cs = cos_ref[rows, :]
        sn = sin_ref[rows, :]
        for h in range(RET_HEADS):
            cdec = math.exp(_ret_log_gamma(h) * BLK)
            qr = _xpos_rotate_split(z_ref[h, rows, H_Q:H_Q + RET_KEY_DIM], cs, sn)
            kr = _xpos_rotate_split(z_ref[h, rows, H_K:H_K + RET_KEY_DIM], cs, sn) * (RET_KEY_DIM ** -0.5)
            qb = qr.astype(BF16)
            kb = kr.astype(BF16)
            kd = (kr * kdec_ref[h]).astype(BF16)
            vb = z_ref[h, rows, H_V:H_V + RET_VALUE_DIM].astype(BF16)
            att = _dot_nt(qb, kb) * decay_ref[h]
            intra = _dot(att.astype(BF16), vb)
            st = s_ref[h]
            cross = _dot(qb, st.astype(BF16)) * qdec_ref[h]
            s_ref[h] = cdec * st + _dot_tn(kd, vb)
            on = _layernorm_nogain(intra + cross)
            gate = z_ref[h, rows, H_G:H_G + RET_VALUE_DIM]
            gated_ref[rows, h * RET_VALUE_DIM:(h + 1) * RET_VALUE_DIM] = (_silu(gate) * on).astype(BF16)
        return carry

    lax.fori_loop(0, nchunk, chunk, 0)

    y = x_ref[0] + _dot(gated_ref[...], wout_ref[...])
    if final_norm:
        y = _rmsnorm(y, fg_ref[...])
    y_ref[0] = y

    @pl.when(j == pl.num_programs(1) - 1)
    def _emit_state():
        sout_ref[0] = s_ref[...]


def _odd_prompt(x, g, win_h, wout, cos_h, sin_h, fg, *, tb, final_norm):
    B, S, D = x.shape
    kern = functools.partial(_odd_prompt_kernel, tb=tb, final_norm=final_norm)
    return pl.pallas_call(
        kern,
        grid=(B, S // tb),
        in_specs=[
            pl.BlockSpec((1, tb, D), lambda b, j: (b, j, 0)),
            _const_spec((1, D)),
            pl.BlockSpec((RET_HEADS, D, HEAD_IN), lambda b, j: (0, 0, 0), pipeline_mode=pl.Buffered(1)),
            pl.BlockSpec((RET_V_WIDTH, D), lambda b, j: (0, 0), pipeline_mode=pl.Buffered(1)),
            pl.BlockSpec((tb, RET_KEY_DIM // 2), lambda b, j: (j, 0)),
            pl.BlockSpec((tb, RET_KEY_DIM // 2), lambda b, j: (j, 0)),
            _const_spec((1, D)),
        ],
        out_specs=[
            pl.BlockSpec((1, tb, D), lambda b, j: (b, j, 0)),
            pl.BlockSpec((1, RET_HEADS, RET_KEY_DIM, RET_VALUE_DIM), lambda b, j: (b, 0, 0, 0)),
        ],
        out_shape=[
            jax.ShapeDtypeStruct((B, S, D), F32),
            jax.ShapeDtypeStruct((B, RET_HEADS, RET_KEY_DIM, RET_VALUE_DIM), F32),
        ],
        scratch_shapes=[
            pltpu.VMEM((tb, D), BF16),
            pltpu.VMEM((RET_HEADS, tb, HEAD_IN), F32),
            pltpu.VMEM((RET_HEADS, RET_KEY_DIM, RET_VALUE_DIM), F32),
            pltpu.VMEM((tb, RET_V_WIDTH), BF16),
            pltpu.VMEM((RET_HEADS, BLK, BLK), F32),
            pltpu.VMEM((RET_HEADS, BLK, 1), F32),
            pltpu.VMEM((RET_HEADS, BLK, 1), F32),
        ],
        compiler_params=pltpu.CompilerParams(
            dimension_semantics=("arbitrary", "arbitrary"), vmem_limit_bytes=VMEM_LIMIT),
        name="odd_prompt",
    )(x, g, win_h, wout, cos_h, sin_h, fg)


def _same_batch_causal(shape, row0=0, col0=0, steps=4):
    r = _iota(shape, 0) + row0
    c = _iota(shape, 1) + col0
    return (r // steps == c // steps) & (c % steps <= r % steps)


def _even_sample_a_kernel(x_ref, g_ref, win_ref, wt_ref, bst_ref, lng_ref,
                          q_ref, k_ref, v_ref, sga_ref, mixb_ref, vn_ref):
    x = x_ref[...]
    xn = _rmsnorm(x, g_ref[...]).astype(BF16)
    z = _dot(xn, win_ref[...])
    q_ref[...] = z[:, E_Q:E_Q + SWA_WIDTH] * (SWA_HEAD_DIM ** -0.5)
    k_ref[...] = z[:, E_K:E_K + SWA_KV_WIDTH]
    v_ref[...] = z[:, E_V:E_V + SWA_KV_WIDTH]
    sga_ref[...] = _silu(z[:, E_GA:E_GA + SWA_WIDTH])
    vn = _layernorm_nogain(z[:, E_VB:E_VB + GMLP_WIDTH]) * lng_ref[...]
    vn_ref[...] = vn
    vnb = vn.astype(BF16)
    mask = _same_batch_causal((BLK, BLK))
    for g in range(GMLP_GROUPS):
        gs = slice(g * GMLP_GROUP_DIM, (g + 1) * GMLP_GROUP_DIM)
        wk = jnp.where(mask, wt_ref[g], 0.0).astype(BF16)
        sp = _dot(wk, vnb[:, gs]) + bst_ref[g]
        u = z[:, E_U + g * GMLP_GROUP_DIM:E_U + (g + 1) * GMLP_GROUP_DIM]
        gb = z[:, E_GB + g * GMLP_GROUP_DIM:E_GB + (g + 1) * GMLP_GROUP_DIM]
        mixb_ref[:, gs] = (_silu(gb) * (u * sp)).astype(BF16)


def _even_sample_a(x, g, win, wt, bst, lng):
    R, D = x.shape
    row = lambda w: pl.BlockSpec((BLK, w), lambda i: (i, 0))
    return pl.pallas_call(
        _even_sample_a_kernel,
        grid=(R // BLK,),
        in_specs=[row(D), _const_spec((1, D)), _const_spec((D, EVEN_IN)),
                  _const_spec((GMLP_GROUPS, BLK, BLK)), _const_spec((GMLP_GROUPS, BLK, 1)),
                  _const_spec((1, GMLP_WIDTH))],
        out_specs=[row(SWA_WIDTH), row(SWA_KV_WIDTH), row(SWA_KV_WIDTH), row(SWA_WIDTH),
                   row(GMLP_WIDTH), row(GMLP_WIDTH)],
        out_shape=[
            jax.ShapeDtypeStruct((R, SWA_WIDTH), F32),
            jax.ShapeDtypeStruct((R, SWA_KV_WIDTH), F32),
            jax.ShapeDtypeStruct((R, SWA_KV_WIDTH), F32),
            jax.ShapeDtypeStruct((R, SWA_WIDTH), F32),
            jax.ShapeDtypeStruct((R, GMLP_WIDTH), BF16),
            jax.ShapeDtypeStruct((R, GMLP_WIDTH), F32),
        ],
        compiler_params=pltpu.CompilerParams(
            dimension_semantics=("arbitrary",), vmem_limit_bytes=VMEM_LIMIT),
        name="even_sample_a",
    )(x, g, win, wt, bst, lng)


SAMPLE_BB = 32
PAIR_ROWS = 8


def _even_sample_b_kernel(sinks_ref, table_ref, q_ref, kn_ref, vn_ref, ck_ref, cv_ref,
                          attn_ref, newk_ref, newv_ref,
                          sn_ref, kns_ref, vns_ref, biasc_ref, biasn_ref, *, steps):
    kparts = _split_heads_kv(kn_ref[...])
    vparts = _split_heads_kv(vn_ref[...])
    for h in range(SWA_KV_HEADS):
        kns_ref[h] = jnp.concatenate(kparts[h], axis=0)
        vns_ref[h] = jnp.concatenate(vparts[h], axis=0)
        c0 = h * SWA_GROUP * SWA_HEAD_DIM
        lhs = jnp.concatenate([q_ref[:, c0:c0 + LANES], q_ref[:, c0 + LANES:c0 + 2 * LANES]], axis=0)
        sn_ref[h] = _dot_nt(lhs.astype(BF16), kns_ref[h])

    step_r = _iota((PAIR_ROWS, BLK), 0) % steps
    lane = _iota((PAIR_ROWS, BLK), 1)
    dist_c = step_r + WINDOW - lane
    valid_c = (dist_c >= 0) & (dist_c < WINDOW)
    dist_n = step_r - lane % steps
    first_local = _iota((PAIR_ROWS, 1), 0) < steps
    for hd in range(SWA_HEADS):
        biasc_ref[hd] = _t5_bias(dist_c, table_ref, hd)
        biasn_ref[hd] = _t5_bias(dist_n, table_ref, hd)

    def pair(i, carry):
        r0 = pl.multiple_of(i * PAIR_ROWS, PAIR_ROWS)
        rows = pl.ds(r0, PAIR_ROWS)
        valid_n = _same_batch_causal((PAIR_ROWS, BLK), row0=r0, steps=steps)
        kc = [_split_heads_kv(ck_ref[2 * i + e]) for e in range(2)]
        vc = [_split_heads_kv(cv_ref[2 * i + e]) for e in range(2)]
        for h in range(SWA_KV_HEADS):
            c0 = h * SWA_GROUP * SWA_HEAD_DIM
            lhs = jnp.concatenate([q_ref[rows, c0:c0 + LANES], q_ref[rows, c0 + LANES:c0 + 2 * LANES]],
                                  axis=0).astype(BF16)
            first16 = jnp.concatenate([first_local, first_local], axis=0)
            sc_e = [_dot_nt(lhs, jnp.concatenate(kc[e][h], axis=0)) for e in range(2)]
            sc = jnp.where(first16, sc_e[0], sc_e[1])
            pc_rows, pn_rows = [], []
            for p in range(2):
                pr = slice(p * PAIR_ROWS, (p + 1) * PAIR_ROWS)
                pc_cols, pn_cols = [], []
                for gi in range(2):
                    hd = h * SWA_GROUP + p * 2 + gi
                    gc = slice(gi * BLK, (gi + 1) * BLK)
                    tc = jnp.where(valid_c, sc[pr, gc] + biasc_ref[hd], NEG_INF)
                    tn_raw = sn_ref[h, pl.ds(p * BLK + r0, PAIR_ROWS), gc]
                    tn = jnp.where(valid_n, tn_raw + biasn_ref[hd], NEG_INF)
                    pc, pn = _sink_softmax_parts([tc, tn], sinks_ref[hd])
                    pc_cols.append(pc)
                    pn_cols.append(pn)
                pc_rows.append(jnp.concatenate(pc_cols, axis=1))
                pn_rows.append(jnp.concatenate(pn_cols, axis=1))
            pc16 = jnp.concatenate(pc_rows, axis=0).astype(BF16)
            pn16 = jnp.concatenate(pn_rows, axis=0).astype(BF16)
            o_e = [_dot(pc16, jnp.concatenate(vc[e][h], axis=0)) for e in range(2)]
            o = jnp.where(first16, o_e[0], o_e[1]) + _dot(pn16, vns_ref[h])
            attn_ref[rows, c0:c0 + LANES] = o[0:PAIR_ROWS]
            attn_ref[rows, c0 + LANES:c0 + 2 * LANES] = o[PAIR_ROWS:2 * PAIR_ROWS]
        for e in range(2):
            be = 2 * i + e
            newk_ref[be, 0:WINDOW - steps, :] = ck_ref[be, steps:WINDOW, :]
            newv_ref[be, 0:WINDOW - steps, :] = cv_ref[be, steps:WINDOW, :]
            newk_ref[be, WINDOW - steps:WINDOW, :] = kn_ref[pl.ds(r0 + e * steps, steps), :]
            newv_ref[be, WINDOW - steps:WINDOW, :] = vn_ref[pl.ds(r0 + e * steps, steps), :]
        return carry

    lax.fori_loop(0, SAMPLE_BB // 2, pair, 0)


def _ignoring_refs(kernel_fn, start, count):
    if count == 0:
        return kernel_fn

    def wrapped(*refs):
        return kernel_fn(*refs[:start], *refs[start + count:])
    return wrapped


def _layer_slab_args(prev_outs, first_out_index, n_fixed_inputs):
    if prev_outs is None:
        return [], [], {}
    specs = [pl.BlockSpec(memory_space=pl.ANY) for _ in prev_outs]
    aliases = {n_fixed_inputs + k: first_out_index + k for k in range(len(prev_outs))}
    return list(prev_outs), specs, aliases


def _even_sample_b(q, kn, vn, ck_all, cv_all, layer, prev_outs, sinks, table_flat, *, steps):
    R = q.shape[0]
    row = lambda w: pl.BlockSpec((BLK, w), lambda i: (i, 0))
    cache = pl.BlockSpec((None, SAMPLE_BB, WINDOW, SWA_KV_WIDTH), lambda i: (layer, i, 0, 0))
    extra, extra_specs, aliases = _layer_slab_args(prev_outs, 1, 7)
    kern = _ignoring_refs(functools.partial(_even_sample_b_kernel, steps=steps), 7, len(extra))
    return pl.pallas_call(
        kern,
        grid=(R // BLK,),
        in_specs=[_smem_spec(), _smem_spec(), row(SWA_WIDTH), row(SWA_KV_WIDTH), row(SWA_KV_WIDTH), cache, cache]
        + extra_specs,
        out_specs=[row(SWA_WIDTH), cache, cache],
        out_shape=[
            jax.ShapeDtypeStruct((R, SWA_WIDTH), F32),
            jax.ShapeDtypeStruct(ck_all.shape, F32),
            jax.ShapeDtypeStruct(cv_all.shape, F32),
        ],
        input_output_aliases=aliases,
        scratch_shapes=[
            pltpu.VMEM((SWA_KV_HEADS, 2 * BLK, 2 * BLK), F32),
            pltpu.VMEM((SWA_KV_HEADS, 2 * BLK, LANES), BF16),
            pltpu.VMEM((SWA_KV_HEADS, 2 * BLK, LANES), BF16),
            pltpu.VMEM((SWA_HEADS, PAIR_ROWS, BLK), F32),
            pltpu.VMEM((SWA_HEADS, PAIR_ROWS, BLK), F32),
        ],
        compiler_params=pltpu.CompilerParams(
            dimension_semantics=("arbitrary",), vmem_limit_bytes=VMEM_LIMIT),
        name="even_sample_b",
    )(sinks, table_flat, q, kn, vn, ck_all, cv_all, *extra)


def _even_sample_c_kernel(x_ref, sga_ref, attn_ref, mixb_ref, wout_ref, y_ref):
    mixa = (sga_ref[...] * attn_ref[...]).astype(BF16)
    mix = jnp.concatenate([mixa, mixb_ref[...]], axis=1)
    y_ref[...] = x_ref[...] + _dot(mix, wout_ref[...])


def _even_sample_c(x, sga, attn, mixb, wout):
    R, D = x.shape
    row = lambda w: pl.BlockSpec((BLK, w), lambda i: (i, 0))
    return pl.pallas_call(
        _even_sample_c_kernel,
        grid=(R // BLK,),
        in_specs=[row(D), row(SWA_WIDTH), row(SWA_WIDTH), row(GMLP_WIDTH), _const_spec((EVEN_MIX, D))],
        out_specs=row(D),
        out_shape=jax.ShapeDtypeStruct((R, D), F32),
        compiler_params=pltpu.CompilerParams(
            dimension_semantics=("arbitrary",), vmem_limit_bytes=VMEM_LIMIT),
        name="even_sample_c",
    )(x, sga, attn, mixb, wout)


def _odd_sample_a_kernel(x_ref, g_ref, win_ref, cos_ref, sin_ref,
                         q_ref, kdt_ref, v_ref, intra_ref, gate_ref, *, steps):
    x = x_ref[...]
    xn = _rmsnorm(x, g_ref[...]).astype(BF16)
    cs = cos_ref[...]
    sn = sin_ref[...]
    rr = _iota((BLK, BLK), 0)
    cc = _iota((BLK, BLK), 1)
    mask = _same_batch_causal((BLK, BLK), steps=steps)
    diff = (rr % steps - cc % steps).astype(F32)
    step_col = (_iota((BLK, 1), 0) % steps).astype(F32)
    for h in range(RET_HEADS):
        lg = _ret_log_gamma(h)
        z = _dot(xn, win_ref[h])
        qr = _xpos_rotate(z[:, H_Q:H_Q + RET_KEY_DIM], cs, sn)
        kr = _xpos_rotate(z[:, H_K:H_K + RET_KEY_DIM], cs, sn) * (RET_KEY_DIM ** -0.5)
        qb = qr.astype(BF16)
        vb = z[:, H_V:H_V + RET_VALUE_DIM].astype(BF16)
        decay = jnp.where(mask, jnp.exp(lg * jnp.maximum(diff, 0.0)), 0.0)
        att = _dot_nt(qb, kr.astype(BF16)) * decay
        intra_ref[:, h * RET_VALUE_DIM:(h + 1) * RET_VALUE_DIM] = _dot(att.astype(BF16), vb)
        kd = kr * jnp.exp(lg * (steps - 1.0 - step_col))
        kdt_ref[h] = kd.T.astype(BF16)
        q_ref[:, h * RET_KEY_DIM:(h + 1) * RET_KEY_DIM] = qb
        v_ref[:, h * RET_VALUE_DIM:(h + 1) * RET_VALUE_DIM] = vb
        gate_ref[:, h * RET_VALUE_DIM:(h + 1) * RET_VALUE_DIM] = _silu(z[:, H_G:H_G + RET_VALUE_DIM])


def _odd_sample_a(x, g, win_h, cos_f, sin_s, *, steps):
    R, D = x.shape
    row = lambda w: pl.BlockSpec((BLK, w), lambda i: (i, 0))
    kern = functools.partial(_odd_sample_a_kernel, steps=steps)
    return pl.pallas_call(
        kern,
        grid=(R // BLK,),
        in_specs=[row(D), _const_spec((1, D)),
                  pl.BlockSpec((RET_HEADS, D, HEAD_IN), lambda i: (0, 0, 0), pipeline_mode=pl.Buffered(1)),
                  row(RET_KEY_DIM), row(RET_KEY_DIM)],
        out_specs=[row(RET_QK_WIDTH),
                   pl.BlockSpec((RET_HEADS, RET_KEY_DIM, BLK), lambda i: (0, 0, i)),
                   row(RET_V_WIDTH), row(RET_V_WIDTH), row(RET_V_WIDTH)],
        out_shape=[
            jax.ShapeDtypeStruct((R, RET_QK_WIDTH), BF16),
            jax.ShapeDtypeStruct((RET_HEADS, RET_KEY_DIM, R), BF16),
            jax.ShapeDtypeStruct((R, RET_V_WIDTH), BF16),
            jax.ShapeDtypeStruct((R, RET_V_WIDTH), F32),
            jax.ShapeDtypeStruct((R, RET_V_WIDTH), F32),
        ],
        compiler_params=pltpu.CompilerParams(
            dimension_semantics=("arbitrary",), vmem_limit_bytes=VMEM_LIMIT),
        name="odd_sample_a",
    )(x, g, win_h, cos_f, sin_s)


STATE_BB = 4
STATE_HB = 2


def _odd_sample_b_kernel(q_ref, kdt_ref, v_ref, s_ref, cross_ref, snew_ref, *, steps):
    blk = pl.program_id(0)
    hg = pl.program_id(1)
    rows16 = STATE_BB * steps
    lane_base = (blk % (BLK // rows16)) * rows16
    lane = _iota((RET_KEY_DIM, BLK), 1)
    row16 = _iota((rows16, 1), 0)
    v = v_ref[...]
    for hl in range(STATE_HB):
        lg = jnp.float32(_ret_log_gamma(hl))
        for g in range(1, RET_HEADS // STATE_HB):
            lg = jnp.where(hg == g, jnp.float32(_ret_log_gamma(g * STATE_HB + hl)), lg)
        qdec = jnp.exp(lg * ((row16 % steps).astype(F32) + 1.0))
        cdec = jnp.exp(jnp.full((1, 1), lg * steps, F32))
        q = q_ref[:, hl * RET_KEY_DIM:(hl + 1) * RET_KEY_DIM]
        kdt = kdt_ref[hl]
        vh = v[:, hl * RET_VALUE_DIM:(hl + 1) * RET_VALUE_DIM]
        cross = jnp.zeros((rows16, RET_VALUE_DIM), F32)
        for bl in range(STATE_BB):
            st = s_ref[bl, hl]
            cr = _dot(q, st.astype(BF16))
            cross = jnp.where(row16 // steps == bl, cr, cross)
            mine = (lane - lane_base) // steps == bl
            upd = _dot(jnp.where(mine, kdt, jnp.zeros_like(kdt)), vh)
            snew_ref[bl, hl] = cdec * st + upd
        cross_ref[:, hl * RET_VALUE_DIM:(hl + 1) * RET_VALUE_DIM] = cross * qdec


def _odd_sample_b(q, kdt, v, state_all, layer, prev_outs, *, steps):
    R = q.shape[0]
    DB = state_all.shape[1]
    rows16 = STATE_BB * steps
    per_lane_blk = BLK // rows16
    extra, extra_specs, aliases = _layer_slab_args(prev_outs, 1, 4)
    kern = _ignoring_refs(functools.partial(_odd_sample_b_kernel, steps=steps), 4, len(extra))
    st_spec = pl.BlockSpec((None, STATE_BB, STATE_HB, RET_KEY_DIM, RET_VALUE_DIM),
                           lambda i, h: (layer, i, h, 0, 0))
    return pl.pallas_call(
        kern,
        grid=(DB // STATE_BB, RET_HEADS // STATE_HB),
        in_specs=[
            pl.BlockSpec((rows16, STATE_HB * RET_KEY_DIM), lambda i, h: (i, h)),
            pl.BlockSpec((STATE_HB, RET_KEY_DIM, BLK), lambda i, h: (h, 0, i // per_lane_blk)),
            pl.BlockSpec((BLK, STATE_HB * RET_VALUE_DIM), lambda i, h: (i // per_lane_blk, h)),
            st_spec,
        ] + extra_specs,
        out_specs=[
            pl.BlockSpec((rows16, STATE_HB * RET_VALUE_DIM), lambda i, h: (i, h)),
            st_spec,
        ],
        out_shape=[
            jax.ShapeDtypeStruct((R, RET_V_WIDTH), F32),
            jax.ShapeDtypeStruct(state_all.shape, F32),
        ],
        input_output_aliases=aliases,
        compiler_params=pltpu.CompilerParams(
            dimension_semantics=("arbitrary", "arbitrary"), vmem_limit_bytes=VMEM_LIMIT),
        name="odd_sample_b",
    )(q, kdt, v, state_all, *extra)


def _odd_sample_c_kernel(x_ref, intra_ref, cross_ref, gate_ref, wout_ref, fg_ref, y_ref, *, final_norm):
    y = x_ref[...]
    for h in range(RET_HEADS):
        hs = slice(h * RET_VALUE_DIM, (h + 1) * RET_VALUE_DIM)
        on = _layernorm_nogain(intra_ref[:, hs] + cross_ref[:, hs])
        y = y + _dot((gate_ref[:, hs] * on).astype(BF16), wout_ref[h])
    if final_norm:
        y = _rmsnorm(y, fg_ref[...])
    y_ref[...] = y


def _odd_sample_c(x, intra, cross, gate, wout_h, fg, *, final_norm):
    R, D = x.shape
    row = lambda w: pl.BlockSpec((BLK, w), lambda i: (i, 0))
    kern = functools.partial(_odd_sample_c_kernel, final_norm=final_norm)
    return pl.pallas_call(
        kern,
        grid=(R // BLK,),
        in_specs=[row(D), row(RET_V_WIDTH), row(RET_V_WIDTH), row(RET_V_WIDTH),
                  _const_spec((RET_HEADS, RET_VALUE_DIM, D)), _const_spec((1, D))],
        out_specs=row(D),
        out_shape=jax.ShapeDtypeStruct((R, D), F32),
        compiler_params=pltpu.CompilerParams(
            dimension_semantics=("arbitrary",), vmem_limit_bytes=VMEM_LIMIT),
        name="odd_sample_c",
    )(x, intra, cross, gate, wout_h, fg)


def _xpos_tables(pos):
    angle = 1.0 / (10000.0 ** jnp.linspace(0.0, 1.0, RET_KEY_DIM // 2, dtype=F32))
    ang = pos.astype(F32)[:, None] * angle[None, :]
    return jnp.cos(ang), jnp.sin(ang)


def _pair_expand(cos, sin):
    cos_f = jnp.repeat(cos, 2, axis=-1)
    sin_s = jnp.stack([-sin, sin], axis=-1).reshape(sin.shape[0], RET_KEY_DIM)
    return cos_f, sin_s


def _regroup_odd_w_in(w, split_pairs):
    def qk(cols):
        x = cols.reshape(D_MODEL, RET_HEADS, RET_KEY_DIM)
        if split_pairs:
            x = jnp.concatenate([x[:, :, 0::2], x[:, :, 1::2]], axis=-1)
        return x
    q = qk(w[:, 0:RET_QK_WIDTH])
    k = qk(w[:, RET_QK_WIDTH:2 * RET_QK_WIDTH])
    v = w[:, 2 * RET_QK_WIDTH:2 * RET_QK_WIDTH + RET_V_WIDTH].reshape(D_MODEL, RET_HEADS, RET_VALUE_DIM)
    g = w[:, 2 * RET_QK_WIDTH + RET_V_WIDTH:].reshape(D_MODEL, RET_HEADS, RET_VALUE_DIM)
    return jnp.concatenate([q, k, v, g], axis=-1).transpose(1, 0, 2).astype(BF16)


PROMPT_TB_EVEN = 512
PROMPT_TB_ODD = 512


def kernel(x_prompt, x_sample, cache_swa_k, cache_swa_v, state_ret, norm_gain, final_norm_gain,
           rel_bias_table, even_w_in, even_w_out, swa_sinks, gmlp_ws, gmlp_bs, gmlp_ln_gain,
           odd_w_in, odd_w_out):
    B, S, D = x_prompt.shape
    DB, T, _ = x_sample.shape
    R = DB * T
    n_even = even_w_in.shape[0]

    table_flat = rel_bias_table.reshape(-1)
    fg = final_norm_gain.reshape(1, D)
    cos_p, sin_p = _xpos_tables(jnp.arange(S, dtype=jnp.int32))
    cos_s, sin_s = _pair_expand(*_xpos_tables(PAST_LEN + jnp.arange(T, dtype=jnp.int32)))
    cos_s = jnp.tile(cos_s, (DB, 1))
    sin_s = jnp.tile(sin_s, (DB, 1))

    yp = x_prompt
    ys = x_sample.reshape(R, D)
    ck_all = cache_swa_k.reshape(n_even, DB, WINDOW, SWA_KV_WIDTH)
    cv_all = cache_swa_v.reshape(n_even, DB, WINDOW, SWA_KV_WIDTH)
    new_caches = None
    new_state = None
    kp_l, vp_l, sp_l, gv_l = [], [], [], []
    for layer in range(DEPTH):
        g = norm_gain[layer].reshape(1, D)
        if layer % 2 == 0:
            e = layer // 2
            win = even_w_in[e].astype(BF16)
            wout = even_w_out[e].astype(BF16)
            lng = gmlp_ln_gain[e].reshape(1, GMLP_WIDTH)
            yp, kp, vp = _even_prompt(yp, g, win, wout, swa_sinks[e], table_flat, gmlp_ws[e],
                                      gmlp_bs[e][:, :, None], lng, tb=PROMPT_TB_EVEN)
            wt = jnp.tile(gmlp_ws[e][:, :T, :T], (1, BLK // T, BLK // T))
            bst = jnp.tile(gmlp_bs[e][:, :T], (1, BLK // T))[:, :, None]
            q, kn, vn_, sga, mixb, gv = _even_sample_a(ys, g, win, wt, bst, lng)
            attn, nk, nv = _even_sample_b(q, kn, vn_, ck_all, cv_all, e, new_caches,
                                          swa_sinks[e], table_flat, steps=T)
            new_caches = (nk, nv)
            ys = _even_sample_c(ys, sga, attn, mixb, wout)
            kp_l.append(kp.reshape(B, WINDOW, SWA_KV_HEADS, SWA_HEAD_DIM))
            vp_l.append(vp.reshape(B, WINDOW, SWA_KV_HEADS, SWA_HEAD_DIM))
            gv_l.append(gv.reshape(DB, T, GMLP_WIDTH))
        else:
            o = layer // 2
            last = layer == DEPTH - 1
            win_h = _regroup_odd_w_in(odd_w_in[o], split_pairs=False)
            win_hp = _regroup_odd_w_in(odd_w_in[o], split_pairs=True)
            wout = odd_w_out[o].astype(BF16)
            wout_h = wout.reshape(RET_HEADS, RET_VALUE_DIM, D)
            yp, sp = _odd_prompt(yp, g, win_hp, wout, cos_p, sin_p, fg, tb=PROMPT_TB_ODD, final_norm=last)
            sp = sp.reshape(B, RET_HEADS, 2, RET_KEY_DIM // 2, RET_VALUE_DIM).swapaxes(2, 3).reshape(sp.shape)
            q, kdt, v, intra, gate = _odd_sample_a(ys, g, win_h, cos_s, sin_s, steps=T)
            cross, ss = _odd_sample_b(q, kdt, v, state_ret, o, None if new_state is None else (new_state,),
                                      steps=T)
            new_state = ss
            ys = _odd_sample_c(ys, intra, cross, gate, wout_h, fg, final_norm=last)
            sp_l.append(sp)
    cache_shape = (n_even, DB, WINDOW, SWA_KV_HEADS, SWA_HEAD_DIM)
    return (yp, ys.reshape(DB, T, D), jnp.stack(kp_l), jnp.stack(vp_l),
            new_caches[0].reshape(cache_shape), new_caches[1].reshape(cache_shape),
            jnp.stack(sp_l), new_state, jnp.stack(gv_l))
```

```python
import functools
import math

import jax
import jax.numpy as jnp
from jax import lax
from jax.experimental import pallas as pl
from jax.experimental.pallas import tpu as pltpu

D_MODEL = 1024
DEPTH = 4
PAST_LEN = 8192
NORM_EPS = 1e-6
NEG_INF = -1e30

SWA_HEADS = 8
SWA_KV_HEADS = 2
SWA_HEAD_DIM = 64
SWA_GROUP = SWA_HEADS // SWA_KV_HEADS
SWA_WIDTH = SWA_HEADS * SWA_HEAD_DIM
SWA_KV_WIDTH = SWA_KV_HEADS * SWA_HEAD_DIM
WINDOW = 128
REL_BUCKETS = 32
REL_MAX_DIST = 128

GMLP_GROUPS = 4
GMLP_CHUNK = 128
GMLP_WIDTH = D_MODEL // 2
GMLP_GROUP_DIM = GMLP_WIDTH // GMLP_GROUPS

E_Q = 0
E_K = E_Q + SWA_WIDTH
E_V = E_K + SWA_KV_WIDTH
E_GA = E_V + SWA_KV_WIDTH
E_U = E_GA + SWA_WIDTH
E_VB = E_U + GMLP_WIDTH
E_GB = E_VB + GMLP_WIDTH
EVEN_IN = E_GB + GMLP_WIDTH
EVEN_MIX = SWA_WIDTH + GMLP_WIDTH

RET_HEADS = 4
RET_KEY_DIM = 256
RET_VALUE_DIM = 512
RET_QK_WIDTH = RET_HEADS * RET_KEY_DIM
RET_V_WIDTH = RET_HEADS * RET_VALUE_DIM
RET_CHUNK = 128
ODD_IN = 2 * RET_QK_WIDTH + 2 * RET_V_WIDTH

LANES = 128
BLK = 128
VMEM_LIMIT = 56 * 1024 * 1024

F32 = jnp.float32
BF16 = jnp.bfloat16


def _ret_log_gamma(h):
    return math.log(1.0 - 2.0 ** (-5.0 - h))


def _dot(a, b):
    return jnp.dot(a, b, preferred_element_type=F32)


def _dot_nt(a, b):
    return lax.dot_general(a, b, (((1,), (1,)), ((), ())), preferred_element_type=F32)


def _dot_tn(a, b):
    return lax.dot_general(a, b, (((0,), (0,)), ((), ())), preferred_element_type=F32)


def _silu(x):
    return x * (1.0 / (1.0 + jnp.exp(-x)))


def _rmsnorm(x, g):
    ms = jnp.mean(x * x, axis=-1, keepdims=True)
    return x * lax.rsqrt(ms + NORM_EPS) * g


def _layernorm_nogain(x):
    mu = jnp.mean(x, axis=-1, keepdims=True)
    d = x - mu
    var = jnp.mean(d * d, axis=-1, keepdims=True)
    return d * lax.rsqrt(var + NORM_EPS)


def _iota(shape, dim):
    return lax.broadcasted_iota(jnp.int32, shape, dim)


def _t5_bias(dist, table_ref, head):
    n = jnp.maximum(dist, 0)
    max_exact = REL_BUCKETS // 2
    nf = jnp.maximum(n, 1).astype(F32)
    large = max_exact + (jnp.log(nf / max_exact) / math.log(REL_MAX_DIST / max_exact)
                         * (REL_BUCKETS - max_exact)).astype(jnp.int32)
    large = jnp.minimum(large, REL_BUCKETS - 1)
    bucket = jnp.where(n < max_exact, n, large)
    acc = jnp.zeros(dist.shape, F32)
    for b in range(REL_BUCKETS):
        acc = jnp.where(bucket == b, table_ref[b * SWA_HEADS + head], acc)
    return acc


def _split_heads_kv(x):
    lo = _iota(x.shape, 1) < SWA_HEAD_DIM
    xr = pltpu.roll(x, SWA_HEAD_DIM, 1)
    a0 = jnp.where(lo, x, 0.0).astype(BF16)
    b0 = jnp.where(lo, 0.0, xr).astype(BF16)
    a1 = jnp.where(lo, xr, 0.0).astype(BF16)
    b1 = jnp.where(lo, 0.0, x).astype(BF16)
    return (a0, b0), (a1, b1)


def _sink_softmax_parts(parts, sink):
    m = sink
    for t in parts:
        m = jnp.maximum(m, jnp.max(t, axis=-1, keepdims=True))
    es = [jnp.exp(t - m) for t in parts]
    den = jnp.exp(sink - m)
    for e in es:
        den = den + jnp.sum(e, axis=-1, keepdims=True)
    inv = 1.0 / den
    return [e * inv for e in es]


def _xpos_rotate_split(x, cos, sin):
    half = x.shape[1] // 2
    x0 = x[:, :half]
    x1 = x[:, half:]
    return jnp.concatenate([x0 * cos - x1 * sin, x1 * cos + x0 * sin], axis=1)


def _layer_spec(shape, layer, **kwargs):
    nd = len(shape)
    return pl.BlockSpec((None,) + tuple(shape), lambda *_: (layer,) + (0,) * nd, **kwargs)


def _const_spec(shape):
    nd = len(shape)
    return pl.BlockSpec(shape, lambda *_: (0,) * nd)


def _smem_spec():
    return pl.BlockSpec(memory_space=pltpu.SMEM)


def _even_prompt_kernel(sinks_ref, table_ref, x_ref, g_ref, win_ref, wout_ref, ws_ref, bst_ref, lng_ref,
                        y_ref, newk_ref, newv_ref,
                        z_ref, mix_ref, ks_ref, vs_ref, bias_ref, wm_ref, *, tb, layer):
    b = pl.program_id(0)
    j = pl.program_id(1)
    nsub = tb // BLK

    @pl.when(jnp.logical_and(b == 0, j == 0))
    def _init_tables():
        qi = _iota((BLK, BLK), 0)
        c = _iota((BLK, BLK), 1)
        own = c <= qi
        dist = jnp.where(own, qi - c, qi + BLK - c)
        for h in range(SWA_HEADS):
            bias = _t5_bias(dist, table_ref, h)
            bias_ref[0, h] = bias
            bias_ref[1, h] = jnp.where(own, bias, NEG_INF)
        causal = _iota((BLK, BLK), 0) >= _iota((BLK, BLK), 1)
        for g in range(GMLP_GROUPS):
            wm_ref[g] = jnp.where(causal, ws_ref[g], 0.0).astype(BF16)

    @pl.when(j == 0)
    def _zero_prev():
        zeros = jnp.zeros((BLK, LANES), BF16)
        for h in range(SWA_KV_HEADS):
            for r in (0, 2 * BLK):
                ks_ref[h, r:r + BLK, :] = zeros
                vs_ref[h, r:r + BLK, :] = zeros

    x = x_ref[0]
    xn = _rmsnorm(x, g_ref[...]).astype(BF16)
    z_ref[...] = _dot(xn, win_ref[...])

    def sub(s, carry):
        r0 = pl.multiple_of(s * BLK, BLK)
        rows = pl.ds(r0, BLK)
        kparts = _split_heads_kv(z_ref[rows, E_K:E_K + SWA_KV_WIDTH])
        vparts = _split_heads_kv(z_ref[rows, E_V:E_V + SWA_KV_WIDTH])
        for h in range(SWA_KV_HEADS):
            ks_ref[h, BLK:2 * BLK, :] = kparts[h][0]
            ks_ref[h, 3 * BLK:4 * BLK, :] = kparts[h][1]
            vs_ref[h, BLK:2 * BLK, :] = vparts[h][0]
            vs_ref[h, 3 * BLK:4 * BLK, :] = vparts[h][1]

        own = _iota((BLK, BLK), 1) <= _iota((BLK, BLK), 0)
        first = jnp.where(j * nsub + s == 0, 1, 0)

        pairs = [(h, p) for h in range(SWA_KV_HEADS) for p in range(SWA_GROUP // 2)]
        col0 = lambda h, p: h * SWA_GROUP * SWA_HEAD_DIM + p * LANES
        logits, sinks = [], []
        for h, p in pairs:
            c0 = col0(h, p)
            qp = (z_ref[rows, E_Q + c0:E_Q + c0 + LANES] * (SWA_HEAD_DIM ** -0.5)).astype(BF16)
            sc = _dot_nt(qp, ks_ref[h])
            for gi in range(2):
                hd = h * SWA_GROUP + p * 2 + gi
                s_prev = sc[:, (2 * gi) * BLK:(2 * gi + 1) * BLK]
                s_cur = sc[:, (2 * gi + 1) * BLK:(2 * gi + 2) * BLK]
                logits.append(jnp.where(own, s_cur, s_prev) + bias_ref[first, hd])
                sinks.append(sinks_ref[layer, hd])
        maxes = [jnp.maximum(jnp.max(t, axis=-1, keepdims=True), sk) for t, sk in zip(logits, sinks)]
        exps = [jnp.exp(t - m) for t, m in zip(logits, maxes)]
        dens = [jnp.sum(e, axis=-1, keepdims=True) + jnp.exp(sk - m) for e, sk, m in zip(exps, sinks, maxes)]
        probs = [e * (1.0 / d) for e, d in zip(exps, dens)]
        for n, (h, p) in enumerate(pairs):
            c0 = col0(h, p)
            parts = []
            for pr in probs[2 * n:2 * n + 2]:
                parts.append(jnp.where(own, 0.0, pr).astype(BF16))
                parts.append(jnp.where(own, pr, 0.0).astype(BF16))
            o = _dot(jnp.concatenate(parts, axis=1), vs_ref[h])
            ga = z_ref[rows, E_GA + c0:E_GA + c0 + LANES]
            mix_ref[rows, c0:c0 + LANES] = (_silu(ga) * o).astype(BF16)

        for h in range(SWA_KV_HEADS):
            ks_ref[h, 0:BLK, :] = ks_ref[h, BLK:2 * BLK, :]
            ks_ref[h, 2 * BLK:3 * BLK, :] = ks_ref[h, 3 * BLK:4 * BLK, :]
            vs_ref[h, 0:BLK, :] = vs_ref[h, BLK:2 * BLK, :]
            vs_ref[h, 2 * BLK:3 * BLK, :] = vs_ref[h, 3 * BLK:4 * BLK, :]

        vn = (_layernorm_nogain(z_ref[rows, E_VB:E_VB + GMLP_WIDTH]) * lng_ref[...]).astype(BF16)
        for g in range(GMLP_GROUPS):
            gs = slice(g * GMLP_GROUP_DIM, (g + 1) * GMLP_GROUP_DIM)
            sp = _dot(wm_ref[g], vn[:, gs]) + bst_ref[g]
            u = z_ref[rows, E_U + g * GMLP_GROUP_DIM:E_U + (g + 1) * GMLP_GROUP_DIM]
            gb = z_ref[rows, E_GB + g * GMLP_GROUP_DIM:E_GB + (g + 1) * GMLP_GROUP_DIM]
            mix_ref[rows, SWA_WIDTH + g * GMLP_GROUP_DIM:SWA_WIDTH + (g + 1) * GMLP_GROUP_DIM] = (
                _silu(gb) * (u * sp)).astype(BF16)
        return carry

    lax.fori_loop(0, nsub, sub, 0)

    y_ref[0] = x + _dot(mix_ref[...], wout_ref[...])

    @pl.when(j == pl.num_programs(1) - 1)
    def _emit_cache():
        newk_ref[0] = z_ref[tb - WINDOW:tb, E_K:E_K + SWA_KV_WIDTH]
        newv_ref[0] = z_ref[tb - WINDOW:tb, E_V:E_V + SWA_KV_WIDTH]


def _even_prompt(x, gains, layer, e, win_all, wout_all, sinks, table_flat, ws_all, bst_all, lng_all, *, tb):
    B, S, D = x.shape
    kern = functools.partial(_even_prompt_kernel, tb=tb, layer=e)
    return pl.pallas_call(
        kern,
        grid=(B, S // tb),
        in_specs=[
            _smem_spec(), _smem_spec(),
            pl.BlockSpec((1, tb, D), lambda b, j: (b, j, 0)),
            _layer_spec((1, D), layer),
            _layer_spec((D, EVEN_IN), e),
            _layer_spec((EVEN_MIX, D), e),
            _layer_spec((GMLP_GROUPS, GMLP_CHUNK, GMLP_CHUNK), e),
            _layer_spec((GMLP_GROUPS, GMLP_CHUNK, 1), e),
            _layer_spec((1, GMLP_WIDTH), e),
        ],
        out_specs=[
            pl.BlockSpec((1, tb, D), lambda b, j: (b, j, 0)),
            pl.BlockSpec((1, WINDOW, SWA_KV_WIDTH), lambda b, j: (b, 0, 0)),
            pl.BlockSpec((1, WINDOW, SWA_KV_WIDTH), lambda b, j: (b, 0, 0)),
        ],
        out_shape=[
            jax.ShapeDtypeStruct((B, S, D), F32),
            jax.ShapeDtypeStruct((B, WINDOW, SWA_KV_WIDTH), F32),
            jax.ShapeDtypeStruct((B, WINDOW, SWA_KV_WIDTH), F32),
        ],
        scratch_shapes=[
            pltpu.VMEM((tb, EVEN_IN), F32),
            pltpu.VMEM((tb, EVEN_MIX), BF16),
            pltpu.VMEM((SWA_KV_HEADS, 4 * BLK, LANES), BF16),
            pltpu.VMEM((SWA_KV_HEADS, 4 * BLK, LANES), BF16),
            pltpu.VMEM((2, SWA_HEADS, BLK, BLK), F32),
            pltpu.VMEM((GMLP_GROUPS, GMLP_CHUNK, GMLP_CHUNK), BF16),
        ],
        compiler_params=pltpu.CompilerParams(
            dimension_semantics=("arbitrary", "arbitrary"), vmem_limit_bytes=VMEM_LIMIT),
        name="even_prompt",
    )(sinks, table_flat, x, gains, win_all, wout_all, ws_all, bst_all, lng_all)


def _odd_prompt_kernel(x_ref, g_ref, wqk_ref, wv_ref, wg_ref, wout_ref, cblk_ref, sblk_ref, crow_ref, srow_ref,
                       fg_ref, y_ref, sout_ref,
                       xn_ref, zqk_ref, qb_ref, kb_ref, kd_ref, vb_ref, sg_ref, s_ref, gated_ref, cos_ref, sin_ref,
                       decay_ref, qdec_ref, kdec_ref, *, tb, final_norm):
    b = pl.program_id(0)
    j = pl.program_id(1)
    nchunk = tb // BLK

    @pl.when(jnp.logical_and(b == 0, j == 0))
    def _init_tables():
        ii = _iota((BLK, BLK), 0)
        jj = _iota((BLK, BLK), 1)
        diff = (ii - jj).astype(F32)
        idx = _iota((BLK, 1), 0).astype(F32)
        idx_blk = (_iota((tb, 1), 0) % BLK).astype(F32)
        for h in range(RET_HEADS):
            lg = _ret_log_gamma(h)
            decay_ref[h] = jnp.where(diff >= 0, jnp.exp(lg * jnp.maximum(diff, 0.0)), 0.0)
            qdec_ref[h] = jnp.exp(lg * (idx + 1.0))
            kdec_ref[h] = jnp.exp(lg * (BLK - 1.0 - idx_blk))

    @pl.when(j == 0)
    def _zero_state():
        s_ref[...] = jnp.zeros(s_ref.shape, F32)

    cb = cblk_ref[pl.ds(j, 1), :]
    sb = sblk_ref[pl.ds(j, 1), :]
    cos_ref[...] = cb * crow_ref[...] - sb * srow_ref[...]
    sin_ref[...] = sb * crow_ref[...] + cb * srow_ref[...]

    heads = range(RET_HEADS)
    xn_ref[...] = _rmsnorm(x_ref[0], g_ref[...]).astype(BF16)
    zqk_ref[...] = _dot(xn_ref[...], wqk_ref[...])
    vb_ref[...] = _dot(xn_ref[...], wv_ref[...]).astype(BF16)
    sg_ref[...] = _silu(_dot(xn_ref[...], wg_ref[...]))
    cs = cos_ref[...]
    sn = sin_ref[...]
    for h in heads:
        qcols = slice(h * RET_KEY_DIM, (h + 1) * RET_KEY_DIM)
        kcols = slice(RET_QK_WIDTH + h * RET_KEY_DIM, RET_QK_WIDTH + (h + 1) * RET_KEY_DIM)
        qb_ref[:, qcols] = _xpos_rotate_split(zqk_ref[:, qcols], cs, sn).astype(BF16)
        kr = _xpos_rotate_split(zqk_ref[:, kcols], cs, sn) * (RET_KEY_DIM ** -0.5)
        kb_ref[:, qcols] = kr.astype(BF16)
        kd_ref[:, qcols] = (kr * kdec_ref[h]).astype(BF16)

    def chunk(c, carry):
        rows = pl.ds(pl.multiple_of(c * BLK, BLK), BLK)
        qcols = [slice(h * RET_KEY_DIM, (h + 1) * RET_KEY_DIM) for h in heads]
        vcols = [slice(h * RET_VALUE_DIM, (h + 1) * RET_VALUE_DIM) for h in heads]
        qb = [qb_ref[rows, qcols[h]] for h in heads]
        vb = [vb_ref[rows, vcols[h]] for h in heads]
        att = [(_dot_nt(qb[h], kb_ref[rows, qcols[h]]) * decay_ref[h]).astype(BF16) for h in heads]
        cross = [_dot(qb[h], s_ref[h].astype(BF16)) * qdec_ref[h] for h in heads]
        o = [_dot(att[h], vb[h]) + cross[h] for h in heads]
        for h in heads:
            s_ref[h] = (math.exp(_ret_log_gamma(h) * BLK) * s_ref[h]
                        + _dot_tn(kd_ref[rows, qcols[h]], vb[h]))
        on = [_layernorm_nogain(o[h]) for h in heads]
        for h in heads:
            gated_ref[rows, vcols[h]] = (sg_ref[rows, vcols[h]] * on[h]).astype(BF16)
        return carry

    lax.fori_loop(0, nchunk, chunk, 0)

    y = x_ref[0] + _dot(gated_ref[...], wout_ref[...])
    if final_norm:
        y = _rmsnorm(y, fg_ref[...])
    y_ref[0] = y

    @pl.when(j == pl.num_programs(1) - 1)
    def _emit_state():
        sout_ref[0] = s_ref[...]


def _odd_prompt(x, gains, layer, o, wqk_all, win_all, wout_all, tables, fg, *, tb, final_norm):
    B, S, D = x.shape
    cblk, sblk, crow, srow = tables
    kern = functools.partial(_odd_prompt_kernel, tb=tb, final_norm=final_norm)
    once = dict(pipeline_mode=pl.Buffered(1))
    half = RET_KEY_DIM // 2
    return pl.pallas_call(
        kern,
        grid=(B, S // tb),
        in_specs=[
            pl.BlockSpec((1, tb, D), lambda b, j: (b, j, 0)),
            _layer_spec((1, D), layer),
            _layer_spec((D, 2 * RET_QK_WIDTH), o, **once),
            pl.BlockSpec((None, D, RET_V_WIDTH), lambda b, j: (o, 0, 1), **once),
            pl.BlockSpec((None, D, RET_V_WIDTH), lambda b, j: (o, 0, 2), **once),
            _layer_spec((RET_V_WIDTH, D), o, **once),
            _const_spec((S // tb, half)), _const_spec((S // tb, half)),
            _const_spec((tb, half)), _const_spec((tb, half)),
            _const_spec((1, D)),
        ],
        out_specs=[
            pl.BlockSpec((1, tb, D), lambda b, j: (b, j, 0)),
            pl.BlockSpec((1, RET_HEADS, RET_KEY_DIM, RET_VALUE_DIM), lambda b, j: (b, 0, 0, 0)),
        ],
        out_shape=[
            jax.ShapeDtypeStruct((B, S, D), F32),
            jax.ShapeDtypeStruct((B, RET_HEADS, RET_KEY_DIM, RET_VALUE_DIM), F32),
        ],
        scratch_shapes=[
            pltpu.VMEM((tb, D), BF16),
            pltpu.VMEM((tb, 2 * RET_QK_WIDTH), F32),
            pltpu.VMEM((tb, RET_QK_WIDTH), BF16),
            pltpu.VMEM((tb, RET_QK_WIDTH), BF16),
            pltpu.VMEM((tb, RET_QK_WIDTH), BF16),
            pltpu.VMEM((tb, RET_V_WIDTH), BF16),
            pltpu.VMEM((tb, RET_V_WIDTH), F32),
            pltpu.VMEM((RET_HEADS, RET_KEY_DIM, RET_VALUE_DIM), F32),
            pltpu.VMEM((tb, RET_V_WIDTH), BF16),
            pltpu.VMEM((tb, half), F32),
            pltpu.VMEM((tb, half), F32),
            pltpu.VMEM((RET_HEADS, BLK, BLK), F32),
            pltpu.VMEM((RET_HEADS, BLK, 1), F32),
            pltpu.VMEM((RET_HEADS, tb, 1), F32),
        ],
        compiler_params=pltpu.CompilerParams(
            dimension_semantics=("arbitrary", "arbitrary"), vmem_limit_bytes=VMEM_LIMIT),
        name="odd_prompt",
    )(x, gains, wqk_all, win_all, win_all, wout_all, cblk, sblk, crow, srow, fg)


def _same_batch_causal(shape, row0=0, col0=0, steps=4):
    r = _iota(shape, 0) + row0
    c = _iota(shape, 1) + col0
    return (r // steps == c // steps) & (c % steps <= r % steps)


def _even_sample_a_kernel(ws4_ref, bs4_ref, x_ref, g_ref, win_ref, lng_ref,
                          q_ref, k_ref, v_ref, sga_ref, mixb_ref, vn_ref, *, steps, layer):
    x = x_ref[...]
    xn = _rmsnorm(x, g_ref[...]).astype(BF16)
    z = _dot(xn, win_ref[...])
    q_ref[...] = z[:, E_Q:E_Q + SWA_WIDTH] * (SWA_HEAD_DIM ** -0.5)
    k_ref[...] = z[:, E_K:E_K + SWA_KV_WIDTH]
    v_ref[...] = z[:, E_V:E_V + SWA_KV_WIDTH]
    sga_ref[...] = _silu(z[:, E_GA:E_GA + SWA_WIDTH])
    vn = _layernorm_nogain(z[:, E_VB:E_VB + GMLP_WIDTH]) * lng_ref[...]
    vn_ref[...] = vn
    vnb = vn.astype(BF16)
    same = _iota((BLK, BLK), 0) // steps == _iota((BLK, BLK), 1) // steps
    rstep = _iota((BLK, BLK), 0) % steps
    cstep = _iota((BLK, BLK), 1) % steps
    rstep_col = _iota((BLK, 1), 0) % steps
    for g in range(GMLP_GROUPS):
        gs = slice(g * GMLP_GROUP_DIM, (g + 1) * GMLP_GROUP_DIM)
        wk = jnp.zeros((BLK, BLK), F32)
        bcol = jnp.zeros((BLK, 1), F32)
        for p in range(steps):
            bcol = jnp.where(rstep_col == p, bs4_ref[(layer * GMLP_GROUPS + g) * steps + p], bcol)
            for q in range(p + 1):
                w = ws4_ref[((layer * GMLP_GROUPS + g) * steps + p) * steps + q]
                wk = jnp.where(same & (rstep == p) & (cstep == q), w, wk)
        sp = _dot(wk.astype(BF16), vnb[:, gs]) + bcol
        u = z[:, E_U + g * GMLP_GROUP_DIM:E_U + (g + 1) * GMLP_GROUP_DIM]
        gb = z[:, E_GB + g * GMLP_GROUP_DIM:E_GB + (g + 1) * GMLP_GROUP_DIM]
        mixb_ref[:, gs] = (_silu(gb) * (u * sp)).astype(BF16)


def _even_sample_a(x, gains, layer, e, win_all, ws4, bs4, lng_all, *, steps):
    R, D = x.shape
    row = lambda w: pl.BlockSpec((BLK, w), lambda i: (i, 0))
    kern = functools.partial(_even_sample_a_kernel, steps=steps, layer=e)
    return pl.pallas_call(
        kern,
        grid=(R // BLK,),
        in_specs=[_smem_spec(), _smem_spec(), row(D), _layer_spec((1, D), layer), _layer_spec((D, EVEN_IN), e),
                  _layer_spec((1, GMLP_WIDTH), e)],
        out_specs=[row(SWA_WIDTH), row(SWA_KV_WIDTH), row(SWA_KV_WIDTH), row(SWA_WIDTH),
                   row(GMLP_WIDTH), row(GMLP_WIDTH)],
        out_shape=[
            jax.ShapeDtypeStruct((R, SWA_WIDTH), F32),
            jax.ShapeDtypeStruct((R, SWA_KV_WIDTH), F32),
            jax.ShapeDtypeStruct((R, SWA_KV_WIDTH), F32),
            jax.ShapeDtypeStruct((R, SWA_WIDTH), F32),
            jax.ShapeDtypeStruct((R, GMLP_WIDTH), BF16),
            jax.ShapeDtypeStruct((R, GMLP_WIDTH), F32),
        ],
        compiler_params=pltpu.CompilerParams(
            dimension_semantics=("arbitrary",), vmem_limit_bytes=VMEM_LIMIT),
        name="even_sample_a",
    )(ws4, bs4, x, gains, win_all, lng_all)


SAMPLE_BB = 32
PAIR_ROWS = 8


def _even_sample_b_kernel(sinks_ref, table_ref, q_ref, kn_ref, vn_ref, ck_ref, cv_ref,
                          attn_ref, newk_ref, newv_ref,
                          sn_ref, kns_ref, vns_ref, biasc_ref, biasn_ref, *, steps, layer):
    kparts = _split_heads_kv(kn_ref[...])
    vparts = _split_heads_kv(vn_ref[...])
    for h in range(SWA_KV_HEADS):
        kns_ref[h] = jnp.concatenate(kparts[h], axis=0)
        vns_ref[h] = jnp.concatenate(vparts[h], axis=0)
        c0 = h * SWA_GROUP * SWA_HEAD_DIM
        lhs = jnp.concatenate([q_ref[:, c0:c0 + LANES], q_ref[:, c0 + LANES:c0 + 2 * LANES]], axis=0)
        sn_ref[h] = _dot_nt(lhs.astype(BF16), kns_ref[h])

    step_r = _iota((PAIR_ROWS, BLK), 0) % steps
    lane = _iota((PAIR_ROWS, BLK), 1)
    dist_c = step_r + WINDOW - lane
    valid_c = (dist_c >= 0) & (dist_c < WINDOW)
    dist_n = step_r - lane % steps
    first_local = _iota((PAIR_ROWS, 1), 0) < steps
    for hd in range(SWA_HEADS):
        biasc_ref[hd] = _t5_bias(dist_c, table_ref, hd)
        biasn_ref[hd] = _t5_bias(dist_n, table_ref, hd)

    def pair(i, carry):
        r0 = pl.multiple_of(i * PAIR_ROWS, PAIR_ROWS)
        rows = pl.ds(r0, PAIR_ROWS)
        valid_n = _same_batch_causal((PAIR_ROWS, BLK), row0=r0, steps=steps)
        kc = [_split_heads_kv(ck_ref[2 * i + e]) for e in range(2)]
        vc = [_split_heads_kv(cv_ref[2 * i + e]) for e in range(2)]
        for h in range(SWA_KV_HEADS):
            c0 = h * SWA_GROUP * SWA_HEAD_DIM
            lhs = jnp.concatenate([q_ref[rows, c0:c0 + LANES], q_ref[rows, c0 + LANES:c0 + 2 * LANES]],
                                  axis=0).astype(BF16)
            first16 = jnp.concatenate([first_local, first_local], axis=0)
            sc_e = [_dot_nt(lhs, jnp.concatenate(kc[e][h], axis=0)) for e in range(2)]
            sc = jnp.where(first16, sc_e[0], sc_e[1])
            pc_rows, pn_rows = [], []
            for p in range(2):
                pr = slice(p * PAIR_ROWS, (p + 1) * PAIR_ROWS)
                pc_cols, pn_cols = [], []
                for gi in range(2):
                    hd = h * SWA_GROUP + p * 2 + gi
                    gc = slice(gi * BLK, (gi + 1) * BLK)
                    tc = jnp.where(valid_c, sc[pr, gc] + biasc_ref[hd], NEG_INF)
                    tn_raw = sn_ref[h, pl.ds(p * BLK + r0, PAIR_ROWS), gc]
                    tn = jnp.where(valid_n, tn_raw + biasn_ref[hd], NEG_INF)
                    pc, pn = _sink_softmax_parts([tc, tn], sinks_ref[layer, hd])
                    pc_cols.append(pc)
                    pn_cols.append(pn)
                pc_rows.append(jnp.concatenate(pc_cols, axis=1))
                pn_rows.append(jnp.concatenate(pn_cols, axis=1))
            pc16 = jnp.concatenate(pc_rows, axis=0).astype(BF16)
            pn16 = jnp.concatenate(pn_rows, axis=0).astype(BF16)
            o_e = [_dot(pc16, jnp.concatenate(vc[e][h], axis=0)) for e in range(2)]
            o = jnp.where(first16, o_e[0], o_e[1]) + _dot(pn16, vns_ref[h])
            attn_ref[rows, c0:c0 + LANES] = o[0:PAIR_ROWS]
            attn_ref[rows, c0 + LANES:c0 + 2 * LANES] = o[PAIR_ROWS:2 * PAIR_ROWS]
        for e in range(2):
            be = 2 * i + e
            newk_ref[be, 0:WINDOW - steps, :] = ck_ref[be, steps:WINDOW, :]
            newv_ref[be, 0:WINDOW - steps, :] = cv_ref[be, steps:WINDOW, :]
            newk_ref[be, WINDOW - steps:WINDOW, :] = kn_ref[pl.ds(r0 + e * steps, steps), :]
            newv_ref[be, WINDOW - steps:WINDOW, :] = vn_ref[pl.ds(r0 + e * steps, steps), :]
        return carry

    lax.fori_loop(0, SAMPLE_BB // 2, pair, 0)


def _ignoring_refs(kernel_fn, start, count):
    if count == 0:
        return kernel_fn

    def wrapped(*refs):
        return kernel_fn(*refs[:start], *refs[start + count:])
    return wrapped


def _layer_slab_args(prev_outs, first_out_index, n_fixed_inputs):
    if prev_outs is None:
        return [], [], {}
    specs = [pl.BlockSpec(memory_space=pl.ANY) for _ in prev_outs]
    aliases = {n_fixed_inputs + k: first_out_index + k for k in range(len(prev_outs))}
    return list(prev_outs), specs, aliases


def _even_sample_b(q, kn, vn, ck_all, cv_all, layer, prev_outs, sinks, table_flat, *, steps):
    R = q.shape[0]
    row = lambda w: pl.BlockSpec((BLK, w), lambda i: (i, 0))
    cache = pl.BlockSpec((None, SAMPLE_BB, WINDOW, SWA_KV_WIDTH), lambda i: (layer, i, 0, 0))
    extra, extra_specs, aliases = _layer_slab_args(prev_outs, 1, 7)
    kern = _ignoring_refs(functools.partial(_even_sample_b_kernel, steps=steps, layer=layer), 7, len(extra))
    return pl.pallas_call(
        kern,
        grid=(R // BLK,),
        in_specs=[_smem_spec(), _smem_spec(), row(SWA_WIDTH), row(SWA_KV_WIDTH), row(SWA_KV_WIDTH), cache, cache]
        + extra_specs,
        out_specs=[row(SWA_WIDTH), cache, cache],
        out_shape=[
            jax.ShapeDtypeStruct((R, SWA_WIDTH), F32),
            jax.ShapeDtypeStruct(ck_all.shape, F32),
            jax.ShapeDtypeStruct(cv_all.shape, F32),
        ],
        input_output_aliases=aliases,
        scratch_shapes=[
            pltpu.VMEM((SWA_KV_HEADS, 2 * BLK, 2 * BLK), F32),
            pltpu.VMEM((SWA_KV_HEADS, 2 * BLK, LANES), BF16),
            pltpu.VMEM((SWA_KV_HEADS, 2 * BLK, LANES), BF16),
            pltpu.VMEM((SWA_HEADS, PAIR_ROWS, BLK), F32),
            pltpu.VMEM((SWA_HEADS, PAIR_ROWS, BLK), F32),
        ],
        compiler_params=pltpu.CompilerParams(
            dimension_semantics=("arbitrary",), vmem_limit_bytes=VMEM_LIMIT),
        name="even_sample_b",
    )(sinks, table_flat, q, kn, vn, ck_all, cv_all, *extra)


def _even_sample_c_kernel(x_ref, sga_ref, attn_ref, mixb_ref, wout_ref, y_ref):
    mixa = (sga_ref[...] * attn_ref[...]).astype(BF16)
    mix = jnp.concatenate([mixa, mixb_ref[...]], axis=1)
    y_ref[...] = x_ref[...] + _dot(mix, wout_ref[...])


def _even_sample_c(x, sga, attn, mixb, wout_all, e):
    R, D = x.shape
    row = lambda w: pl.BlockSpec((BLK, w), lambda i: (i, 0))
    return pl.pallas_call(
        _even_sample_c_kernel,
        grid=(R // BLK,),
        in_specs=[row(D), row(SWA_WIDTH), row(SWA_WIDTH), row(GMLP_WIDTH), _layer_spec((EVEN_MIX, D), e)],
        out_specs=row(D),
        out_shape=jax.ShapeDtypeStruct((R, D), F32),
        compiler_params=pltpu.CompilerParams(
            dimension_semantics=("arbitrary",), vmem_limit_bytes=VMEM_LIMIT),
        name="even_sample_c",
    )(x, sga, attn, mixb, wout_all)


def _pair_split_perm():
    n = jnp.arange(RET_KEY_DIM)
    half = RET_KEY_DIM // 2
    src = jnp.where(n < half, 2 * n, 2 * (n - half) + 1)
    return (jnp.arange(RET_KEY_DIM)[:, None] == src[None, :]).astype(F32)


def _odd_sample_a_kernel(x_ref, g_ref, wqk_ref, wv_ref, wg_ref, cos8_ref, sin8_ref, split_ref,
                         q_ref, kdt_ref, v_ref, intra_ref, gate_ref, *, steps):
    x = x_ref[...]
    xn = _rmsnorm(x, g_ref[...]).astype(BF16)
    reps = BLK // cos8_ref.shape[0]
    cs = jnp.concatenate([cos8_ref[...]] * reps, axis=0)
    sn = jnp.concatenate([sin8_ref[...]] * reps, axis=0)
    rr = _iota((BLK, BLK), 0)
    cc = _iota((BLK, BLK), 1)
    mask = _same_batch_causal((BLK, BLK), steps=steps)
    diff = (rr % steps - cc % steps).astype(F32)
    step_col = (_iota((BLK, 1), 0) % steps).astype(F32)
    split = split_ref[...]
    zqk = _dot(xn, wqk_ref[...])
    zv = _dot(xn, wv_ref[...])
    gate_ref[...] = _silu(_dot(xn, wg_ref[...]))
    v_ref[...] = zv.astype(BF16)
    for h in range(RET_HEADS):
        lg = _ret_log_gamma(h)
        qcols = slice(h * RET_KEY_DIM, (h + 1) * RET_KEY_DIM)
        kcols = slice(RET_QK_WIDTH + h * RET_KEY_DIM, RET_QK_WIDTH + (h + 1) * RET_KEY_DIM)
        vcols = slice(h * RET_VALUE_DIM, (h + 1) * RET_VALUE_DIM)
        qr = _xpos_rotate_split(zqk[:, qcols], cs, sn)
        kr = _xpos_rotate_split(zqk[:, kcols], cs, sn) * (RET_KEY_DIM ** -0.5)
        qb = qr.astype(BF16)
        decay = jnp.where(mask, jnp.exp(lg * jnp.maximum(diff, 0.0)), 0.0)
        att = _dot_nt(qb, kr.astype(BF16)) * decay
        intra_ref[:, vcols] = _dot(att.astype(BF16), zv[:, vcols].astype(BF16))
        kd = (kr * jnp.exp(lg * (steps - 1.0 - step_col))).astype(BF16)
        q_ref[:, qcols] = _dot_nt(qb, split).astype(BF16)
        kdt_ref[h] = _dot_nt(split, kd).astype(BF16)


def _odd_sample_a(x, gains, layer, o, wqk_all, win_all, cos8, sin8, split, *, steps):
    R, D = x.shape
    row = lambda w: pl.BlockSpec((BLK, w), lambda i: (i, 0))
    kern = functools.partial(_odd_sample_a_kernel, steps=steps)
    once = dict(pipeline_mode=pl.Buffered(1))
    return pl.pallas_call(
        kern,
        grid=(R // BLK,),
        in_specs=[row(D), _layer_spec((1, D), layer),
                  _layer_spec((D, 2 * RET_QK_WIDTH), o, **once),
                  pl.BlockSpec((None, D, RET_V_WIDTH), lambda i: (o, 0, 1), **once),
                  pl.BlockSpec((None, D, RET_V_WIDTH), lambda i: (o, 0, 2), **once),
                  _const_spec(cos8.shape), _const_spec(sin8.shape), _const_spec(split.shape)],
        out_specs=[row(RET_QK_WIDTH),
                   pl.BlockSpec((RET_HEADS, RET_KEY_DIM, BLK), lambda i: (0, 0, i)),
                   row(RET_V_WIDTH), row(RET_V_WIDTH), row(RET_V_WIDTH)],
        out_shape=[
            jax.ShapeDtypeStruct((R, RET_QK_WIDTH), BF16),
            jax.ShapeDtypeStruct((RET_HEADS, RET_KEY_DIM, R), BF16),
            jax.ShapeDtypeStruct((R, RET_V_WIDTH), BF16),
            jax.ShapeDtypeStruct((R, RET_V_WIDTH), F32),
            jax.ShapeDtypeStruct((R, RET_V_WIDTH), F32),
        ],
        compiler_params=pltpu.CompilerParams(
            dimension_semantics=("arbitrary",), vmem_limit_bytes=VMEM_LIMIT),
        name="odd_sample_a",
    )(x, gains, wqk_all, win_all, win_all, cos8, sin8, split)


STATE_BB = 4
STATE_HB = 2


def _odd_sample_b_kernel(q_ref, kdt_ref, v_ref, s_ref, cross_ref, snew_ref, *, steps):
    blk = pl.program_id(0)
    hg = pl.program_id(1)
    rows16 = STATE_BB * steps
    lane_base = (blk % (BLK // rows16)) * rows16
    lane = _iota((RET_KEY_DIM, BLK), 1)
    row16 = _iota((rows16, 1), 0)
    v = v_ref[...]
    for hl in range(STATE_HB):
        lg = jnp.float32(_ret_log_gamma(hl))
        for g in range(1, RET_HEADS // STATE_HB):
            lg = jnp.where(hg == g, jnp.float32(_ret_log_gamma(g * STATE_HB + hl)), lg)
        qdec = jnp.exp(lg * ((row16 % steps).astype(F32) + 1.0))
        cdec = jnp.exp(jnp.full((1, 1), lg * steps, F32))
        q = q_ref[:, hl * RET_KEY_DIM:(hl + 1) * RET_KEY_DIM]
        kdt = kdt_ref[hl]
        vh = v[:, hl * RET_VALUE_DIM:(hl + 1) * RET_VALUE_DIM]
        cross = jnp.zeros((rows16, RET_VALUE_DIM), F32)
        for bl in range(STATE_BB):
            st = s_ref[bl, hl]
            cr = _dot(q, st.astype(BF16))
            cross = jnp.where(row16 // steps == bl, cr, cross)
            mine = (lane - lane_base) // steps == bl
            upd = _dot(jnp.where(mine, kdt, jnp.zeros_like(kdt)), vh)
            snew_ref[bl, hl] = cdec * st + upd
        cross_ref[:, hl * RET_VALUE_DIM:(hl + 1) * RET_VALUE_DIM] = cross * qdec


def _odd_sample_b(q, kdt, v, state_all, layer, prev_outs, *, steps):
    R = q.shape[0]
    DB = state_all.shape[1]
    rows16 = STATE_BB * steps
    per_lane_blk = BLK // rows16
    extra, extra_specs, aliases = _layer_slab_args(prev_outs, 1, 4)
    kern = _ignoring_refs(functools.partial(_odd_sample_b_kernel, steps=steps), 4, len(extra))
    st_spec = pl.BlockSpec((None, STATE_BB, STATE_HB, RET_KEY_DIM, RET_VALUE_DIM),
                           lambda i, h: (layer, i, h, 0, 0))
    return pl.pallas_call(
        kern,
        grid=(DB // STATE_BB, RET_HEADS // STATE_HB),
        in_specs=[
            pl.BlockSpec((rows16, STATE_HB * RET_KEY_DIM), lambda i, h: (i, h)),
            pl.BlockSpec((STATE_HB, RET_KEY_DIM, BLK), lambda i, h: (h, 0, i // per_lane_blk)),
            pl.BlockSpec((BLK, STATE_HB * RET_VALUE_DIM), lambda i, h: (i // per_lane_blk, h)),
            st_spec,
        ] + extra_specs,
        out_specs=[
            pl.BlockSpec((rows16, STATE_HB * RET_VALUE_DIM), lambda i, h: (i, h)),
            st_spec,
        ],
        out_shape=[
            jax.ShapeDtypeStruct((R, RET_V_WIDTH), F32),
            jax.ShapeDtypeStruct(state_all.shape, F32),
        ],
        input_output_aliases=aliases,
        compiler_params=pltpu.CompilerParams(
            dimension_semantics=("arbitrary", "arbitrary"), vmem_limit_bytes=VMEM_LIMIT),
        name="odd_sample_b",
    )(q, kdt, v, state_all, *extra)


def _odd_sample_c_kernel(x_ref, intra_ref, cross_ref, gate_ref, wout_ref, fg_ref, y_ref, *, final_norm):
    gated = []
    for h in range(RET_HEADS):
        hs = slice(h * RET_VALUE_DIM, (h + 1) * RET_VALUE_DIM)
        on = _layernorm_nogain(intra_ref[:, hs] + cross_ref[:, hs])
        gated.append((gate_ref[:, hs] * on).astype(BF16))
    y = x_ref[...] + _dot(jnp.concatenate(gated, axis=1), wout_ref[...])
    if final_norm:
        y = _rmsnorm(y, fg_ref[...])
    y_ref[...] = y


def _odd_sample_c(x, intra, cross, gate, wout_all, o, fg, *, final_norm):
    R, D = x.shape
    row = lambda w: pl.BlockSpec((BLK, w), lambda i: (i, 0))
    kern = functools.partial(_odd_sample_c_kernel, final_norm=final_norm)
    return pl.pallas_call(
        kern,
        grid=(R // BLK,),
        in_specs=[row(D), row(RET_V_WIDTH), row(RET_V_WIDTH), row(RET_V_WIDTH),
                  _layer_spec((RET_V_WIDTH, D), o), _const_spec((1, D))],
        out_specs=row(D),
        out_shape=jax.ShapeDtypeStruct((R, D), F32),
        compiler_params=pltpu.CompilerParams(
            dimension_semantics=("arbitrary",), vmem_limit_bytes=VMEM_LIMIT),
        name="odd_sample_c",
    )(x, intra, cross, gate, wout_all, fg)


def _xpos_tables(pos):
    angle = 1.0 / (10000.0 ** jnp.linspace(0.0, 1.0, RET_KEY_DIM // 2, dtype=F32))
    ang = pos.astype(F32)[:, None] * angle[None, :]
    return jnp.cos(ang), jnp.sin(ang)


def _split_pairs_qk(w_qk):
    x = w_qk.reshape(w_qk.shape[0], D_MODEL, 2 * RET_HEADS, RET_KEY_DIM).astype(F32)
    y = jnp.einsum('ldhc,cn->ldhn', x, _pair_split_perm())
    return y.astype(BF16).reshape(w_qk.shape)


PROMPT_TB_EVEN = 512
PROMPT_TB_ODD = 512


def kernel(x_prompt, x_sample, cache_swa_k, cache_swa_v, state_ret, norm_gain, final_norm_gain,
           rel_bias_table, even_w_in, even_w_out, swa_sinks, gmlp_ws, gmlp_bs, gmlp_ln_gain,
           odd_w_in, odd_w_out):
    B, S, D = x_prompt.shape
    DB, T, _ = x_sample.shape
    R = DB * T
    n_even = even_w_in.shape[0]
    tb = PROMPT_TB_ODD

    table_flat = rel_bias_table.reshape(-1)
    gains = norm_gain.reshape(DEPTH, 1, D)
    fg = final_norm_gain.reshape(1, D)
    even_win = even_w_in.astype(BF16)
    even_wout = even_w_out.astype(BF16)
    odd_win = odd_w_in.astype(BF16)
    odd_wout = odd_w_out.astype(BF16)
    odd_wqk = _split_pairs_qk(odd_win[:, :, :2 * RET_QK_WIDTH])
    bst_all = gmlp_bs[:, :, :, None]
    lng_all = gmlp_ln_gain.reshape(n_even, 1, GMLP_WIDTH)
    ws4 = gmlp_ws[:, :, :T, :T].reshape(-1)
    bs4 = gmlp_bs[:, :, :T].reshape(-1)

    prompt_tables = (_xpos_tables(jnp.arange(0, S, tb, dtype=jnp.int32))
                     + _xpos_tables(jnp.arange(tb, dtype=jnp.int32)))
    cos4, sin4 = _xpos_tables(PAST_LEN + jnp.arange(T, dtype=jnp.int32))
    cos8 = jnp.tile(cos4, (PAIR_ROWS // T, 1))
    sin8 = jnp.tile(sin4, (PAIR_ROWS // T, 1))
    split = _pair_split_perm().astype(BF16)

    yp = x_prompt
    ys = x_sample.reshape(R, D)
    ck_all = cache_swa_k.reshape(n_even, DB, WINDOW, SWA_KV_WIDTH)
    cv_all = cache_swa_v.reshape(n_even, DB, WINDOW, SWA_KV_WIDTH)
    new_caches = None
    new_state = None
    kp_l, vp_l, sp_l, gv_l = [], [], [], []
    for layer in range(DEPTH):
        if layer % 2 == 0:
            e = layer // 2
            yp, kp, vp = _even_prompt(yp, gains, layer, e, even_win, even_wout, swa_sinks, table_flat,
                                      gmlp_ws, bst_all, lng_all, tb=PROMPT_TB_EVEN)
            q, kn, vn_, sga, mixb, gv = _even_sample_a(ys, gains, layer, e, even_win, ws4, bs4, lng_all, steps=T)
            attn, nk, nv = _even_sample_b(q, kn, vn_, ck_all, cv_all, e, new_caches,
                                          swa_sinks, table_flat, steps=T)
            new_caches = (nk, nv)
            ys = _even_sample_c(ys, sga, attn, mixb, even_wout, e)
            kp_l.append(kp.reshape(B, WINDOW, SWA_KV_HEADS, SWA_HEAD_DIM))
            vp_l.append(vp.reshape(B, WINDOW, SWA_KV_HEADS, SWA_HEAD_DIM))
            gv_l.append(gv.reshape(DB, T, GMLP_WIDTH))
        else:
            o = layer // 2
            last = layer == DEPTH - 1
            yp, sp = _odd_prompt(yp, gains, layer, o, odd_wqk, odd_win, odd_wout, prompt_tables, fg,
                                 tb=tb, final_norm=last)
            sp = sp.reshape(B, RET_HEADS, 2, RET_KEY_DIM // 2, RET_VALUE_DIM).swapaxes(2, 3).reshape(sp.shape)
            q, kdt, v, intra, gate = _odd_sample_a(ys, gains, layer, o, odd_wqk, odd_win, cos8, sin8, split,
                                                   steps=T)
            cross, ss = _odd_sample_b(q, kdt, v, state_ret, o, None if new_state is None else (new_state,),
                                      steps=T)
            new_state = ss
            ys = _odd_sample_c(ys, intra, cross, gate, odd_wout, o, fg, final_norm=last)
            sp_l.append(sp)
    cache_shape = (n_even, DB, WINDOW, SWA_KV_HEADS, SWA_HEAD_DIM)
    return (yp, ys.reshape(DB, T, D), jnp.stack(kp_l), jnp.stack(vp_l),
            new_caches[0].reshape(cache_shape), new_caches[1].reshape(cache_shape),
            jnp.stack(sp_l), new_state, jnp.stack(gv_l))
```

```python
import functools
import math

import jax
import jax.numpy as jnp
from jax import lax
from jax.experimental import pallas as pl
from jax.experimental.pallas import tpu as pltpu

D_MODEL = 1024
DEPTH = 4
PAST_LEN = 8192
NORM_EPS = 1e-6
NEG_INF = -1e30

SWA_HEADS = 8
SWA_KV_HEADS = 2
SWA_HEAD_DIM = 64
SWA_GROUP = SWA_HEADS // SWA_KV_HEADS
SWA_WIDTH = SWA_HEADS * SWA_HEAD_DIM
SWA_KV_WIDTH = SWA_KV_HEADS * SWA_HEAD_DIM
WINDOW = 128
REL_BUCKETS = 32
REL_MAX_DIST = 128

GMLP_GROUPS = 4
GMLP_CHUNK = 128
GMLP_WIDTH = D_MODEL // 2
GMLP_GROUP_DIM = GMLP_WIDTH // GMLP_GROUPS

E_Q = 0
E_K = E_Q + SWA_WIDTH
E_V = E_K + SWA_KV_WIDTH
E_GA = E_V + SWA_KV_WIDTH
E_U = E_GA + SWA_WIDTH
E_VB = E_U + GMLP_WIDTH
E_GB = E_VB + GMLP_WIDTH
EVEN_IN = E_GB + GMLP_WIDTH
EVEN_MIX = SWA_WIDTH + GMLP_WIDTH

RET_HEADS = 4
RET_KEY_DIM = 256
RET_VALUE_DIM = 512
RET_QK_WIDTH = RET_HEADS * RET_KEY_DIM
RET_V_WIDTH = RET_HEADS * RET_VALUE_DIM
RET_CHUNK = 128
ODD_IN = 2 * RET_QK_WIDTH + 2 * RET_V_WIDTH

LANES = 128
BLK = 128
VMEM_LIMIT = 56 * 1024 * 1024

F32 = jnp.float32
BF16 = jnp.bfloat16


def _ret_log_gamma(h):
    return math.log(1.0 - 2.0 ** (-5.0 - h))


def _dot(a, b):
    return jnp.dot(a, b, preferred_element_type=F32)


def _dot_nt(a, b):
    return lax.dot_general(a, b, (((1,), (1,)), ((), ())), preferred_element_type=F32)


def _dot_tn(a, b):
    return lax.dot_general(a, b, (((0,), (0,)), ((), ())), preferred_element_type=F32)


def _silu(x):
    return x * (1.0 / (1.0 + jnp.exp(-x)))


def _rmsnorm(x, g):
    ms = jnp.mean(x * x, axis=-1, keepdims=True)
    return x * lax.rsqrt(ms + NORM_EPS) * g


def _layernorm_nogain(x):
    mu = jnp.mean(x, axis=-1, keepdims=True)
    d = x - mu
    var = jnp.mean(d * d, axis=-1, keepdims=True)
    return d * lax.rsqrt(var + NORM_EPS)


def _iota(shape, dim):
    return lax.broadcasted_iota(jnp.int32, shape, dim)


def _t5_bias(dist, table_ref, head):
    n = jnp.maximum(dist, 0)
    max_exact = REL_BUCKETS // 2
    nf = jnp.maximum(n, 1).astype(F32)
    large = max_exact + (jnp.log(nf / max_exact) / math.log(REL_MAX_DIST / max_exact)
                         * (REL_BUCKETS - max_exact)).astype(jnp.int32)
    large = jnp.minimum(large, REL_BUCKETS - 1)
    bucket = jnp.where(n < max_exact, n, large)
    acc = jnp.zeros(dist.shape, F32)
    for b in range(REL_BUCKETS):
        acc = jnp.where(bucket == b, table_ref[b * SWA_HEADS + head], acc)
    return acc


def _split_heads_kv(x):
    lo = _iota(x.shape, 1) < SWA_HEAD_DIM
    xr = pltpu.roll(x, SWA_HEAD_DIM, 1)
    a0 = jnp.where(lo, x, 0.0).astype(BF16)
    b0 = jnp.where(lo, 0.0, xr).astype(BF16)
    a1 = jnp.where(lo, xr, 0.0).astype(BF16)
    b1 = jnp.where(lo, 0.0, x).astype(BF16)
    return (a0, b0), (a1, b1)


def _sink_softmax_parts(parts, sink):
    m = sink
    for t in parts:
        m = jnp.maximum(m, jnp.max(t, axis=-1, keepdims=True))
    es = [jnp.exp(t - m) for t in parts]
    den = jnp.exp(sink - m)
    for e in es:
        den = den + jnp.sum(e, axis=-1, keepdims=True)
    inv = 1.0 / den
    return [e * inv for e in es]


def _xpos_rotate_split(x, cos, sin):
    half = x.shape[1] // 2
    x0 = x[:, :half]
    x1 = x[:, half:]
    return jnp.concatenate([x0 * cos - x1 * sin, x1 * cos + x0 * sin], axis=1)


def _layer_spec(shape, layer, **kwargs):
    nd = len(shape)
    return pl.BlockSpec((None,) + tuple(shape), lambda *_: (layer,) + (0,) * nd, **kwargs)


def _const_spec(shape):
    nd = len(shape)
    return pl.BlockSpec(shape, lambda *_: (0,) * nd)


def _smem_spec():
    return pl.BlockSpec(memory_space=pltpu.SMEM)


EVEN_SUB_GROUP = 4

def _even_prompt_kernel(sinks_ref, table_ref, x_ref, g_ref, win_ref, wout_ref, ws_ref, bst_ref, lng_ref,
                        y_ref, newk_ref, newv_ref,
                        z_ref, qb_ref, vn_ref, mix_ref, ks_ref, vs_ref, bias_ref, wm_ref, *, tb, layer):
    b = pl.program_id(0)
    j = pl.program_id(1)
    nsub = tb // BLK

    @pl.when(jnp.logical_and(b == 0, j == 0))
    def _init_tables():
        qi = _iota((BLK, BLK), 0)
        c = _iota((BLK, BLK), 1)
        own = c <= qi
        dist = jnp.where(own, qi - c, qi + BLK - c)
        for h in range(SWA_HEADS):
            bias = _t5_bias(dist, table_ref, h)
            bias_ref[0, h] = bias
            bias_ref[1, h] = jnp.where(own, bias, NEG_INF)
        causal = _iota((BLK, BLK), 0) >= _iota((BLK, BLK), 1)
        for g in range(GMLP_GROUPS):
            wm_ref[g] = jnp.where(causal, ws_ref[g], 0.0).astype(BF16)

    @pl.when(j == 0)
    def _zero_prev():
        zeros = jnp.zeros((BLK, LANES), BF16)
        for h in range(SWA_KV_HEADS):
            for ab in range(2):
                ks_ref[h, ab, 0:BLK, :] = zeros
                vs_ref[h, ab, 0:BLK, :] = zeros

    x = x_ref[0]
    xn = _rmsnorm(x, g_ref[...]).astype(BF16)
    z_ref[...] = _dot(xn, win_ref[...])
    qb_ref[...] = (z_ref[:, E_Q:E_Q + SWA_WIDTH] * (SWA_HEAD_DIM ** -0.5)).astype(BF16)
    z_ref[:, E_GA:E_GA + SWA_WIDTH] = _silu(z_ref[:, E_GA:E_GA + SWA_WIDTH])
    z_ref[:, E_GB:E_GB + GMLP_WIDTH] = _silu(z_ref[:, E_GB:E_GB + GMLP_WIDTH])
    vn_ref[...] = (_layernorm_nogain(z_ref[:, E_VB:E_VB + GMLP_WIDTH]) * lng_ref[...]).astype(BF16)
    kparts = _split_heads_kv(z_ref[:, E_K:E_K + SWA_KV_WIDTH])
    vparts = _split_heads_kv(z_ref[:, E_V:E_V + SWA_KV_WIDTH])
    for h in range(SWA_KV_HEADS):
        for ab in range(2):
            ks_ref[h, ab, BLK:BLK + tb, :] = kparts[h][ab]
            vs_ref[h, ab, BLK:BLK + tb, :] = vparts[h][ab]

    def sub_group(sg, carry):
        own = _iota((BLK, BLK), 1) <= _iota((BLK, BLK), 0)
        col0 = lambda h, p: h * SWA_GROUP * SWA_HEAD_DIM + p * LANES
        units = []
        for i in range(EVEN_SUB_GROUP):
            s = sg * EVEN_SUB_GROUP + i
            r0 = pl.multiple_of(s * BLK, BLK)
            first = jnp.where(j * nsub + s == 0, 1, 0)
            for h in range(SWA_KV_HEADS):
                for p in range(SWA_GROUP // 2):
                    units.append((pl.ds(r0, BLK), pl.ds(r0, 2 * BLK), first, h, p))
        stacked = lambda ref, h, pc: jnp.concatenate([ref[h, 0, pc, :], ref[h, 1, pc, :]], axis=0)
        logits, sinks = [], []
        for rows, prev_cur, first, h, p in units:
            c0 = col0(h, p)
            sc = _dot_nt(qb_ref[rows, c0:c0 + LANES], stacked(ks_ref, h, prev_cur))
            for gi in range(2):
                hd = h * SWA_GROUP + p * 2 + gi
                s_prev = sc[:, (2 * gi) * BLK:(2 * gi + 1) * BLK]
                s_cur = sc[:, (2 * gi + 1) * BLK:(2 * gi + 2) * BLK]
                logits.append(jnp.where(own, s_cur, s_prev) + bias_ref[first, hd])
                sinks.append(sinks_ref[layer, hd])
        maxes = [jnp.maximum(jnp.max(t, axis=-1, keepdims=True), sk) for t, sk in zip(logits, sinks)]
        exps = [jnp.exp(t - m) for t, m in zip(logits, maxes)]
        dens = [jnp.sum(e, axis=-1, keepdims=True) + jnp.exp(sk - m) for e, sk, m in zip(exps, sinks, maxes)]
        probs = [e * (1.0 / d) for e, d in zip(exps, dens)]
        for n, (rows, prev_cur, first, h, p) in enumerate(units):
            c0 = col0(h, p)
            parts = []
            for pr in probs[2 * n:2 * n + 2]:
                parts.append(jnp.where(own, 0.0, pr).astype(BF16))
                parts.append(jnp.where(own, pr, 0.0).astype(BF16))
            o = _dot(jnp.concatenate(parts, axis=1), stacked(vs_ref, h, prev_cur))
            mix_ref[rows, c0:c0 + LANES] = (z_ref[rows, E_GA + c0:E_GA + c0 + LANES] * o).astype(BF16)

        for i in range(EVEN_SUB_GROUP):
            rows = pl.ds(pl.multiple_of((sg * EVEN_SUB_GROUP + i) * BLK, BLK), BLK)
            for g in range(GMLP_GROUPS):
                gs = slice(g * GMLP_GROUP_DIM, (g + 1) * GMLP_GROUP_DIM)
                sp = _dot(wm_ref[g], vn_ref[rows, gs]) + bst_ref[g]
                u = z_ref[rows, E_U + g * GMLP_GROUP_DIM:E_U + (g + 1) * GMLP_GROUP_DIM]
                sgb = z_ref[rows, E_GB + g * GMLP_GROUP_DIM:E_GB + (g + 1) * GMLP_GROUP_DIM]
                mix_ref[rows, SWA_WIDTH + g * GMLP_GROUP_DIM:SWA_WIDTH + (g + 1) * GMLP_GROUP_DIM] = (
                    sgb * (u * sp)).astype(BF16)
        return carry

    lax.fori_loop(0, nsub // EVEN_SUB_GROUP, sub_group, 0)

    for h in range(SWA_KV_HEADS):
        for ab in range(2):
            ks_ref[h, ab, 0:BLK, :] = ks_ref[h, ab, tb:tb + BLK, :]
            vs_ref[h, ab, 0:BLK, :] = vs_ref[h, ab, tb:tb + BLK, :]

    y_ref[0] = x + _dot(mix_ref[...], wout_ref[...])

    @pl.when(j == pl.num_programs(1) - 1)
    def _emit_cache():
        newk_ref[0] = z_ref[tb - WINDOW:tb, E_K:E_K + SWA_KV_WIDTH]
        newv_ref[0] = z_ref[tb - WINDOW:tb, E_V:E_V + SWA_KV_WIDTH]


def _even_prompt(x, gains, layer, e, win_all, wout_all, sinks, table_flat, ws_all, bst_all, lng_all, *, tb):
    B, S, D = x.shape
    kern = functools.partial(_even_prompt_kernel, tb=tb, layer=e)
    return pl.pallas_call(
        kern,
        grid=(B, S // tb),
        in_specs=[
            _smem_spec(), _smem_spec(),
            pl.BlockSpec((1, tb, D), lambda b, j: (b, j, 0)),
            _layer_spec((1, D), layer),
            _layer_spec((D, EVEN_IN), e),
            _layer_spec((EVEN_MIX, D), e),
            _layer_spec((GMLP_GROUPS, GMLP_CHUNK, GMLP_CHUNK), e),
            _layer_spec((GMLP_GROUPS, GMLP_CHUNK, 1), e),
            _layer_spec((1, GMLP_WIDTH), e),
        ],
        out_specs=[
            pl.BlockSpec((1, tb, D), lambda b, j: (b, j, 0)),
            pl.BlockSpec((1, WINDOW, SWA_KV_WIDTH), lambda b, j: (b, 0, 0)),
            pl.BlockSpec((1, WINDOW, SWA_KV_WIDTH), lambda b, j: (b, 0, 0)),
        ],
        out_shape=[
            jax.ShapeDtypeStruct((B, S, D), F32),
            jax.ShapeDtypeStruct((B, WINDOW, SWA_KV_WIDTH), F32),
            jax.ShapeDtypeStruct((B, WINDOW, SWA_KV_WIDTH), F32),
        ],
        scratch_shapes=[
            pltpu.VMEM((tb, EVEN_IN), F32),
            pltpu.VMEM((tb, SWA_WIDTH), BF16),
            pltpu.VMEM((tb, GMLP_WIDTH), BF16),
            pltpu.VMEM((tb, EVEN_MIX), BF16),
            pltpu.VMEM((SWA_KV_HEADS, 2, BLK + tb, LANES), BF16),
            pltpu.VMEM((SWA_KV_HEADS, 2, BLK + tb, LANES), BF16),
            pltpu.VMEM((2, SWA_HEADS, BLK, BLK), F32),
            pltpu.VMEM((GMLP_GROUPS, GMLP_CHUNK, GMLP_CHUNK), BF16),
        ],
        compiler_params=pltpu.CompilerParams(
            dimension_semantics=("arbitrary", "arbitrary"), vmem_limit_bytes=VMEM_LIMIT),
        name="even_prompt",
    )(sinks, table_flat, x, gains, win_all, wout_all, ws_all, bst_all, lng_all)


def _odd_prompt_kernel(x_ref, g_ref, wqk_ref, wv_ref, wg_ref, wout_ref, cblk_ref, sblk_ref, crow_ref, srow_ref,
                       fg_ref, y_ref, sout_ref,
                       xn_ref, zqk_ref, qb_ref, kb_ref, kd_ref, vb_ref, sg_ref, s_ref, gated_ref, cos_ref, sin_ref,
                       decay_ref, qdec_ref, kdec_ref, *, tb, final_norm):
    b = pl.program_id(0)
    j = pl.program_id(1)
    nchunk = tb // BLK

    @pl.when(jnp.logical_and(b == 0, j == 0))
    def _init_tables():
        ii = _iota((BLK, BLK), 0)
        jj = _iota((BLK, BLK), 1)
        diff = (ii - jj).astype(F32)
        idx = _iota((BLK, 1), 0).astype(F32)
        idx_blk = (_iota((tb, 1), 0) % BLK).astype(F32)
        for h in range(RET_HEADS):
            lg = _ret_log_gamma(h)
            decay_ref[h] = jnp.where(diff >= 0, jnp.exp(lg * jnp.maximum(diff, 0.0)), 0.0)
            qdec_ref[h] = jnp.exp(lg * (idx + 1.0))
            kdec_ref[h] = jnp.exp(lg * (BLK - 1.0 - idx_blk))

    @pl.when(j == 0)
    def _zero_state():
        s_ref[...] = jnp.zeros(s_ref.shape, F32)

    cb = cblk_ref[pl.ds(j, 1), :]
    sb = sblk_ref[pl.ds(j, 1), :]
    cos_ref[...] = cb * crow_ref[...] - sb * srow_ref[...]
    sin_ref[...] = sb * crow_ref[...] + cb * srow_ref[...]

    heads = range(RET_HEADS)
    xn_ref[...] = _rmsnorm(x_ref[0], g_ref[...]).astype(BF16)
    zqk_ref[...] = _dot(xn_ref[...], wqk_ref[...])
    vb_ref[...] = _dot(xn_ref[...], wv_ref[...]).astype(BF16)
    sg_ref[...] = _silu(_dot(xn_ref[...], wg_ref[...]))
    cs = cos_ref[...]
    sn = sin_ref[...]
    for h in heads:
        qcols = slice(h * RET_KEY_DIM, (h + 1) * RET_KEY_DIM)
        kcols = slice(RET_QK_WIDTH + h * RET_KEY_DIM, RET_QK_WIDTH + (h + 1) * RET_KEY_DIM)
        qb_ref[:, qcols] = _xpos_rotate_split(zqk_ref[:, qcols], cs, sn).astype(BF16)
        kr = _xpos_rotate_split(zqk_ref[:, kcols], cs, sn) * (RET_KEY_DIM ** -0.5)
        kb_ref[:, qcols] = kr.astype(BF16)
        kd_ref[:, qcols] = (kr * kdec_ref[h]).astype(BF16)

    def chunk(c, carry):
        rows = pl.ds(pl.multiple_of(c * BLK, BLK), BLK)
        qcols = [slice(h * RET_KEY_DIM, (h + 1) * RET_KEY_DIM) for h in heads]
        vcols = [slice(h * RET_VALUE_DIM, (h + 1) * RET_VALUE_DIM) for h in heads]
        qb = [qb_ref[rows, qcols[h]] for h in heads]
        vb = [vb_ref[rows, vcols[h]] for h in heads]
        att = [(_dot_nt(qb[h], kb_ref[rows, qcols[h]]) * decay_ref[h]).astype(BF16) for h in heads]
        cross = [_dot(qb[h], s_ref[h].astype(BF16)) * qdec_ref[h] for h in heads]
        for h in heads:
            o_ref[rows, vcols[h]] = _dot(att[h], vb[h]) + cross[h]
        for h in heads:
            s_ref[h] = (math.exp(_ret_log_gamma(h) * BLK) * s_ref[h]
                        + _dot_tn(kd_ref[rows, qcols[h]], vb[h]))
        return carry

    o_ref = zqk_ref
    lax.fori_loop(0, nchunk, chunk, 0)

    tiles = [(slice(c * BLK, (c + 1) * BLK), slice(h * RET_VALUE_DIM, (h + 1) * RET_VALUE_DIM))
             for c in range(nchunk) for h in heads]
    normed = [_layernorm_nogain(o_ref[r, v]) for r, v in tiles]
    for (r, v), on in zip(tiles, normed):
        gated_ref[r, v] = (sg_ref[r, v] * on).astype(BF16)

    y = x_ref[0] + _dot(gated_ref[...], wout_ref[...])
    if final_norm:
        y = _rmsnorm(y, fg_ref[...])
    y_ref[0] = y

    @pl.when(j == pl.num_programs(1) - 1)
    def _emit_state():
        sout_ref[0] = s_ref[...]


def _odd_prompt(x, gains, layer, o, wqk_all, win_all, wout_all, tables, fg, *, tb, final_norm):
    B, S, D = x.shape
    cblk, sblk, crow, srow = tables
    kern = functools.partial(_odd_prompt_kernel, tb=tb, final_norm=final_norm)
    once = dict(pipeline_mode=pl.Buffered(1))
    half = RET_KEY_DIM // 2
    return pl.pallas_call(
        kern,
        grid=(B, S // tb),
        in_specs=[
            pl.BlockSpec((1, tb, D), lambda b, j: (b, j, 0)),
            _layer_spec((1, D), layer),
            _layer_spec((D, 2 * RET_QK_WIDTH), o, **once),
            pl.BlockSpec((None, D, RET_V_WIDTH), lambda b, j: (o, 0, 1), **once),
            pl.BlockSpec((None, D, RET_V_WIDTH), lambda b, j: (o, 0, 2), **once),
            _layer_spec((RET_V_WIDTH, D), o, **once),
            _const_spec((S // tb, half)), _const_spec((S // tb, half)),
            _const_spec((tb, half)), _const_spec((tb, half)),
            _const_spec((1, D)),
        ],
        out_specs=[
            pl.BlockSpec((1, tb, D), lambda b, j: (b, j, 0)),
            pl.BlockSpec((1, RET_HEADS, RET_KEY_DIM, RET_VALUE_DIM), lambda b, j: (b, 0, 0, 0)),
        ],
        out_shape=[
            jax.ShapeDtypeStruct((B, S, D), F32),
            jax.ShapeDtypeStruct((B, RET_HEADS, RET_KEY_DIM, RET_VALUE_DIM), F32),
        ],
        scratch_shapes=[
            pltpu.VMEM((tb, D), BF16),
            pltpu.VMEM((tb, 2 * RET_QK_WIDTH), F32),
            pltpu.VMEM((tb, RET_QK_WIDTH), BF16),
            pltpu.VMEM((tb, RET_QK_WIDTH), BF16),
            pltpu.VMEM((tb, RET_QK_WIDTH), BF16),
            pltpu.VMEM((tb, RET_V_WIDTH), BF16),
            pltpu.VMEM((tb, RET_V_WIDTH), F32),
            pltpu.VMEM((RET_HEADS, RET_KEY_DIM, RET_VALUE_DIM), F32),
            pltpu.VMEM((tb, RET_V_WIDTH), BF16),
            pltpu.VMEM((tb, half), F32),
            pltpu.VMEM((tb, half), F32),
            pltpu.VMEM((RET_HEADS, BLK, BLK), F32),
            pltpu.VMEM((RET_HEADS, BLK, 1), F32),
            pltpu.VMEM((RET_HEADS, tb, 1), F32),
        ],
        compiler_params=pltpu.CompilerParams(
            dimension_semantics=("arbitrary", "arbitrary"), vmem_limit_bytes=VMEM_LIMIT),
        name="odd_prompt",
    )(x, gains, wqk_all, win_all, win_all, wout_all, cblk, sblk, crow, srow, fg)


def _same_batch_causal(shape, row0=0, col0=0, steps=4):
    r = _iota(shape, 0) + row0
    c = _iota(shape, 1) + col0
    return (r // steps == c // steps) & (c % steps <= r % steps)


def _even_sample_a_kernel(ws4_ref, bs4_ref, x_ref, g_ref, win_ref, lng_ref,
                          q_ref, k_ref, v_ref, sga_ref, mixb_ref, vn_ref, *, steps, layer):
    x = x_ref[...]
    xn = _rmsnorm(x, g_ref[...]).astype(BF16)
    z = _dot(xn, win_ref[...])
    q_ref[...] = z[:, E_Q:E_Q + SWA_WIDTH] * (SWA_HEAD_DIM ** -0.5)
    k_ref[...] = z[:, E_K:E_K + SWA_KV_WIDTH]
    v_ref[...] = z[:, E_V:E_V + SWA_KV_WIDTH]
    sga_ref[...] = _silu(z[:, E_GA:E_GA + SWA_WIDTH])
    vn = _layernorm_nogain(z[:, E_VB:E_VB + GMLP_WIDTH]) * lng_ref[...]
    vn_ref[...] = vn
    vnb = vn.astype(BF16)
    same = _iota((BLK, BLK), 0) // steps == _iota((BLK, BLK), 1) // steps
    rstep = _iota((BLK, BLK), 0) % steps
    cstep = _iota((BLK, BLK), 1) % steps
    rstep_col = _iota((BLK, 1), 0) % steps
    for g in range(GMLP_GROUPS):
        gs = slice(g * GMLP_GROUP_DIM, (g + 1) * GMLP_GROUP_DIM)
        wk = jnp.zeros((BLK, BLK), F32)
        bcol = jnp.zeros((BLK, 1), F32)
        for p in range(steps):
            bcol = jnp.where(rstep_col == p, bs4_ref[(layer * GMLP_GROUPS + g) * steps + p], bcol)
            for q in range(p + 1):
                w = ws4_ref[((layer * GMLP_GROUPS + g) * steps + p) * steps + q]
                wk = jnp.where(same & (rstep == p) & (cstep == q), w, wk)
        sp = _dot(wk.astype(BF16), vnb[:, gs]) + bcol
        u = z[:, E_U + g * GMLP_GROUP_DIM:E_U + (g + 1) * GMLP_GROUP_DIM]
        gb = z[:, E_GB + g * GMLP_GROUP_DIM:E_GB + (g + 1) * GMLP_GROUP_DIM]
        mixb_ref[:, gs] = (_silu(gb) * (u * sp)).astype(BF16)


def _even_sample_a(x, gains, layer, e, win_all, ws4, bs4, lng_all, *, steps):
    R, D = x.shape
    row = lambda w: pl.BlockSpec((BLK, w), lambda i: (i, 0))
    kern = functools.partial(_even_sample_a_kernel, steps=steps, layer=e)
    return pl.pallas_call(
        kern,
        grid=(R // BLK,),
        in_specs=[_smem_spec(), _smem_spec(), row(D), _layer_spec((1, D), layer), _layer_spec((D, EVEN_IN), e),
                  _layer_spec((1, GMLP_WIDTH), e)],
        out_specs=[row(SWA_WIDTH), row(SWA_KV_WIDTH), row(SWA_KV_WIDTH), row(SWA_WIDTH),
                   row(GMLP_WIDTH), row(GMLP_WIDTH)],
        out_shape=[
            jax.ShapeDtypeStruct((R, SWA_WIDTH), F32),
            jax.ShapeDtypeStruct((R, SWA_KV_WIDTH), F32),
            jax.ShapeDtypeStruct((R, SWA_KV_WIDTH), F32),
            jax.ShapeDtypeStruct((R, SWA_WIDTH), F32),
            jax.ShapeDtypeStruct((R, GMLP_WIDTH), BF16),
            jax.ShapeDtypeStruct((R, GMLP_WIDTH), F32),
        ],
        compiler_params=pltpu.CompilerParams(
            dimension_semantics=("arbitrary",), vmem_limit_bytes=VMEM_LIMIT),
        name="even_sample_a",
    )(ws4, bs4, x, gains, win_all, lng_all)


SAMPLE_BB = 32
PAIR_ROWS = 8
SAMPLE_PAIRS_PER_ITER = 2


def _even_sample_b_kernel(sinks_ref, table_ref, q_ref, kn_ref, vn_ref, ck_ref, cv_ref,
                          attn_ref, newk_ref, newv_ref,
                          sn_ref, kns_ref, vns_ref, biasc_ref, biasn_ref, *, steps, layer):
    kparts = _split_heads_kv(kn_ref[...])
    vparts = _split_heads_kv(vn_ref[...])
    for h in range(SWA_KV_HEADS):
        kns_ref[h] = jnp.concatenate(kparts[h], axis=0)
        vns_ref[h] = jnp.concatenate(vparts[h], axis=0)
        c0 = h * SWA_GROUP * SWA_HEAD_DIM
        lhs = jnp.concatenate([q_ref[:, c0:c0 + LANES], q_ref[:, c0 + LANES:c0 + 2 * LANES]], axis=0)
        sn_ref[h] = _dot_nt(lhs.astype(BF16), kns_ref[h])

    step_r = _iota((PAIR_ROWS, BLK), 0) % steps
    lane = _iota((PAIR_ROWS, BLK), 1)
    dist_c = step_r + WINDOW - lane
    valid_c = (dist_c >= 0) & (dist_c < WINDOW)
    dist_n = step_r - lane % steps
    first_local = _iota((PAIR_ROWS, 1), 0) < steps
    for hd in range(SWA_HEADS):
        biasc_ref[hd] = _t5_bias(dist_c, table_ref, hd)
        biasn_ref[hd] = _t5_bias(dist_n, table_ref, hd)

    first16 = jnp.concatenate([first_local, first_local], axis=0)

    def pair_group(it, carry):
        units = []
        for k in range(SAMPLE_PAIRS_PER_ITER):
            i = it * SAMPLE_PAIRS_PER_ITER + k
            r0 = pl.multiple_of(i * PAIR_ROWS, PAIR_ROWS)
            valid_n = _same_batch_causal((PAIR_ROWS, BLK), row0=r0, steps=steps)
            kc = [_split_heads_kv(ck_ref[2 * i + e]) for e in range(2)]
            vc = [_split_heads_kv(cv_ref[2 * i + e]) for e in range(2)]
            for h in range(SWA_KV_HEADS):
                units.append((r0, valid_n, h, [kc[e][h] for e in range(2)], [vc[e][h] for e in range(2)]))
            for e in range(2):
                be = 2 * i + e
                newk_ref[be, 0:WINDOW - steps, :] = ck_ref[be, steps:WINDOW, :]
                newv_ref[be, 0:WINDOW - steps, :] = cv_ref[be, steps:WINDOW, :]
                newk_ref[be, WINDOW - steps:WINDOW, :] = kn_ref[pl.ds(r0 + e * steps, steps), :]
                newv_ref[be, WINDOW - steps:WINDOW, :] = vn_ref[pl.ds(r0 + e * steps, steps), :]

        scores = []
        for r0, valid_n, h, kch, vch in units:
            rows = pl.ds(r0, PAIR_ROWS)
            c0 = h * SWA_GROUP * SWA_HEAD_DIM
            lhs = jnp.concatenate([q_ref[rows, c0:c0 + LANES], q_ref[rows, c0 + LANES:c0 + 2 * LANES]],
                                  axis=0).astype(BF16)
            sc_e = [_dot_nt(lhs, jnp.concatenate(kch[e], axis=0)) for e in range(2)]
            scores.append(jnp.where(first16, sc_e[0], sc_e[1]))

        tcs, tns, sks = [], [], []
        for (r0, valid_n, h, kch, vch), sc in zip(units, scores):
            for p in range(2):
                pr = slice(p * PAIR_ROWS, (p + 1) * PAIR_ROWS)
                for gi in range(2):
                    hd = h * SWA_GROUP + p * 2 + gi
                    gc = slice(gi * BLK, (gi + 1) * BLK)
                    tcs.append(jnp.where(valid_c, sc[pr, gc] + biasc_ref[hd], NEG_INF))
                    tn_raw = sn_ref[h, pl.ds(p * BLK + r0, PAIR_ROWS), gc]
                    tns.append(jnp.where(valid_n, tn_raw + biasn_ref[hd], NEG_INF))
                    sks.append(sinks_ref[layer, hd])
        ms = [jnp.maximum(jnp.maximum(jnp.max(tc, axis=-1, keepdims=True), jnp.max(tn, axis=-1, keepdims=True)), sk)
              for tc, tn, sk in zip(tcs, tns, sks)]
        ecs = [jnp.exp(tc - m) for tc, m in zip(tcs, ms)]
        ens = [jnp.exp(tn - m) for tn, m in zip(tns, ms)]
        invs = [1.0 / (jnp.exp(sk - m) + jnp.sum(ec, axis=-1, keepdims=True) + jnp.sum(en, axis=-1, keepdims=True))
                for sk, m, ec, en in zip(sks, ms, ecs, ens)]
        pcs = [ec * inv for ec, inv in zip(ecs, invs)]
        pns = [en * inv for en, inv in zip(ens, invs)]

        for n, (r0, valid_n, h, kch, vch) in enumerate(units):
            rows = pl.ds(r0, PAIR_ROWS)
            c0 = h * SWA_GROUP * SWA_HEAD_DIM
            grid16 = lambda ps: jnp.concatenate(
                [jnp.concatenate(ps[4 * n + 2 * p:4 * n + 2 * p + 2], axis=1) for p in range(2)], axis=0).astype(BF16)
            pc16 = grid16(pcs)
            pn16 = grid16(pns)
            o_e = [_dot(pc16, jnp.concatenate(vch[e], axis=0)) for e in range(2)]
            o = jnp.where(first16, o_e[0], o_e[1]) + _dot(pn16, vns_ref[h])
            attn_ref[rows, c0:c0 + LANES] = o[0:PAIR_ROWS]
            attn_ref[rows, c0 + LANES:c0 + 2 * LANES] = o[PAIR_ROWS:2 * PAIR_ROWS]
        return carry

    lax.fori_loop(0, SAMPLE_BB // (2 * SAMPLE_PAIRS_PER_ITER), pair_group, 0)


def _ignoring_refs(kernel_fn, start, count):
    if count == 0:
        return kernel_fn

    def wrapped(*refs):
        return kernel_fn(*refs[:start], *refs[start + count:])
    return wrapped


def _layer_slab_args(prev_outs, first_out_index, n_fixed_inputs):
    if prev_outs is None:
        return [], [], {}
    specs = [pl.BlockSpec(memory_space=pl.ANY) for _ in prev_outs]
    aliases = {n_fixed_inputs + k: first_out_index + k for k in range(len(prev_outs))}
    return list(prev_outs), specs, aliases


def _even_sample_b(q, kn, vn, ck_all, cv_all, layer, prev_outs, sinks, table_flat, *, steps):
    R = q.shape[0]
    row = lambda w: pl.BlockSpec((BLK, w), lambda i: (i, 0))
    cache = pl.BlockSpec((None, SAMPLE_BB, WINDOW, SWA_KV_WIDTH), lambda i: (layer, i, 0, 0))
    extra, extra_specs, aliases = _layer_slab_args(prev_outs, 1, 7)
    kern = _ignoring_refs(functools.partial(_even_sample_b_kernel, steps=steps, layer=layer), 7, len(extra))
    return pl.pallas_call(
        kern,
        grid=(R // BLK,),
        in_specs=[_smem_spec(), _smem_spec(), row(SWA_WIDTH), row(SWA_KV_WIDTH), row(SWA_KV_WIDTH), cache, cache]
        + extra_specs,
        out_specs=[row(SWA_WIDTH), cache, cache],
        out_shape=[
            jax.ShapeDtypeStruct((R, SWA_WIDTH), F32),
            jax.ShapeDtypeStruct(ck_all.shape, F32),
            jax.ShapeDtypeStruct(cv_all.shape, F32),
        ],
        input_output_aliases=aliases,
        scratch_shapes=[
            pltpu.VMEM((SWA_KV_HEADS, 2 * BLK, 2 * BLK), F32),
            pltpu.VMEM((SWA_KV_HEADS, 2 * BLK, LANES), BF16),
            pltpu.VMEM((SWA_KV_HEADS, 2 * BLK, LANES), BF16),
            pltpu.VMEM((SWA_HEADS, PAIR_ROWS, BLK), F32),
            pltpu.VMEM((SWA_HEADS, PAIR_ROWS, BLK), F32),
        ],
        compiler_params=pltpu.CompilerParams(
            dimension_semantics=("arbitrary",), vmem_limit_bytes=VMEM_LIMIT),
        name="even_sample_b",
    )(sinks, table_flat, q, kn, vn, ck_all, cv_all, *extra)


def _even_sample_c_kernel(x_ref, sga_ref, attn_ref, mixb_ref, wout_ref, y_ref):
    mixa = (sga_ref[...] * attn_ref[...]).astype(BF16)
    mix = jnp.concatenate([mixa, mixb_ref[...]], axis=1)
    y_ref[...] = x_ref[...] + _dot(mix, wout_ref[...])


def _even_sample_c(x, sga, attn, mixb, wout_all, e):
    R, D = x.shape
    row = lambda w: pl.BlockSpec((BLK, w), lambda i: (i, 0))
    return pl.pallas_call(
        _even_sample_c_kernel,
        grid=(R // BLK,),
        in_specs=[row(D), row(SWA_WIDTH), row(SWA_WIDTH), row(GMLP_WIDTH), _layer_spec((EVEN_MIX, D), e)],
        out_specs=row(D),
        out_shape=jax.ShapeDtypeStruct((R, D), F32),
        compiler_params=pltpu.CompilerParams(
            dimension_semantics=("arbitrary",), vmem_limit_bytes=VMEM_LIMIT),
        name="even_sample_c",
    )(x, sga, attn, mixb, wout_all)


def _pair_split_perm():
    n = jnp.arange(RET_KEY_DIM)
    half = RET_KEY_DIM // 2
    src = jnp.where(n < half, 2 * n, 2 * (n - half) + 1)
    return (jnp.arange(RET_KEY_DIM)[:, None] == src[None, :]).astype(F32)


def _odd_sample_a_kernel(x_ref, g_ref, wqk_ref, wv_ref, wg_ref, cos8_ref, sin8_ref, split_ref,
                         q_ref, kdt_ref, v_ref, intra_ref, gate_ref, *, steps):
    x = x_ref[...]
    xn = _rmsnorm(x, g_ref[...]).astype(BF16)
    reps = BLK // cos8_ref.shape[0]
    cs = jnp.concatenate([cos8_ref[...]] * reps, axis=0)
    sn = jnp.concatenate([sin8_ref[...]] * reps, axis=0)
    rr = _iota((BLK, BLK), 0)
    cc = _iota((BLK, BLK), 1)
    mask = _same_batch_causal((BLK, BLK), steps=steps)
    diff = (rr % steps - cc % steps).astype(F32)
    step_col = (_iota((BLK, 1), 0) % steps).astype(F32)
    split = split_ref[...]
    zqk = _dot(xn, wqk_ref[...])
    zv = _dot(xn, wv_ref[...])
    gate_ref[...] = _silu(_dot(xn, wg_ref[...]))
    v_ref[...] = zv.astype(BF16)
    for h in range(RET_HEADS):
        lg = _ret_log_gamma(h)
        qcols = slice(h * RET_KEY_DIM, (h + 1) * RET_KEY_DIM)
        kcols = slice(RET_QK_WIDTH + h * RET_KEY_DIM, RET_QK_WIDTH + (h + 1) * RET_KEY_DIM)
        vcols = slice(h * RET_VALUE_DIM, (h + 1) * RET_VALUE_DIM)
        qr = _xpos_rotate_split(zqk[:, qcols], cs, sn)
        kr = _xpos_rotate_split(zqk[:, kcols], cs, sn) * (RET_KEY_DIM ** -0.5)
        qb = qr.astype(BF16)
        decay = jnp.where(mask, jnp.exp(lg * jnp.maximum(diff, 0.0)), 0.0)
        att = _dot_nt(qb, kr.astype(BF16)) * decay
        intra_ref[:, vcols] = _dot(att.astype(BF16), zv[:, vcols].astype(BF16))
        kd = (kr * jnp.exp(lg * (steps - 1.0 - step_col))).astype(BF16)
        q_ref[:, qcols] = _dot_nt(qb, split).astype(BF16)
        kdt_ref[h] = _dot_nt(split, kd).astype(BF16)


def _odd_sample_a(x, gains, layer, o, wqk_all, win_all, cos8, sin8, split, *, steps):
    R, D = x.shape
    row = lambda w: pl.BlockSpec((BLK, w), lambda i: (i, 0))
    kern = functools.partial(_odd_sample_a_kernel, steps=steps)
    once = dict(pipeline_mode=pl.Buffered(1))
    return pl.pallas_call(
        kern,
        grid=(R // BLK,),
        in_specs=[row(D), _layer_spec((1, D), layer),
                  _layer_spec((D, 2 * RET_QK_WIDTH), o, **once),
                  pl.BlockSpec((None, D, RET_V_WIDTH), lambda i: (o, 0, 1), **once),
                  pl.BlockSpec((None, D, RET_V_WIDTH), lambda i: (o, 0, 2), **once),
                  _const_spec(cos8.shape), _const_spec(sin8.shape), _const_spec(split.shape)],
        out_specs=[row(RET_QK_WIDTH),
                   pl.BlockSpec((RET_HEADS, RET_KEY_DIM, BLK), lambda i: (0, 0, i)),
                   row(RET_V_WIDTH), row(RET_V_WIDTH), row(RET_V_WIDTH)],
        out_shape=[
            jax.ShapeDtypeStruct((R, RET_QK_WIDTH), BF16),
            jax.ShapeDtypeStruct((RET_HEADS, RET_KEY_DIM, R), BF16),
            jax.ShapeDtypeStruct((R, RET_V_WIDTH), BF16),
            jax.ShapeDtypeStruct((R, RET_V_WIDTH), F32),
            jax.ShapeDtypeStruct((R, RET_V_WIDTH), F32),
        ],
        compiler_params=pltpu.CompilerParams(
            dimension_semantics=("arbitrary",), vmem_limit_bytes=VMEM_LIMIT),
        name="odd_sample_a",
    )(x, gains, wqk_all, win_all, win_all, cos8, sin8, split)


STATE_BB = 4
STATE_HB = 2


def _odd_sample_b_kernel(q_ref, kdt_ref, v_ref, s_ref, cross_ref, snew_ref, *, steps):
    blk = pl.program_id(0)
    hg = pl.program_id(1)
    rows16 = STATE_BB * steps
    lane_base = (blk % (BLK // rows16)) * rows16
    lane = _iota((RET_KEY_DIM, BLK), 1)
    row16 = _iota((rows16, 1), 0)
    v = v_ref[...]
    for hl in range(STATE_HB):
        lg = jnp.float32(_ret_log_gamma(hl))
        for g in range(1, RET_HEADS // STATE_HB):
            lg = jnp.where(hg == g, jnp.float32(_ret_log_gamma(g * STATE_HB + hl)), lg)
        qdec = jnp.exp(lg * ((row16 % steps).astype(F32) + 1.0))
        cdec = jnp.exp(jnp.full((1, 1), lg * steps, F32))
        q = q_ref[:, hl * RET_KEY_DIM:(hl + 1) * RET_KEY_DIM]
        kdt = kdt_ref[hl]
        vh = v[:, hl * RET_VALUE_DIM:(hl + 1) * RET_VALUE_DIM]
        cross = jnp.zeros((rows16, RET_VALUE_DIM), F32)
        for bl in range(STATE_BB):
            st = s_ref[bl, hl]
            cr = _dot(q, st.astype(BF16))
            cross = jnp.where(row16 // steps == bl, cr, cross)
            mine = (lane - lane_base) // steps == bl
            upd = _dot(jnp.where(mine, kdt, jnp.zeros_like(kdt)), vh)
            snew_ref[bl, hl] = cdec * st + upd
        cross_ref[:, hl * RET_VALUE_DIM:(hl + 1) * RET_VALUE_DIM] = cross * qdec


def _odd_sample_b(q, kdt, v, state_all, layer, prev_outs, *, steps):
    R = q.shape[0]
    DB = state_all.shape[1]
    rows16 = STATE_BB * steps
    per_lane_blk = BLK // rows16
    extra, extra_specs, aliases = _layer_slab_args(prev_outs, 1, 4)
    kern = _ignoring_refs(functools.partial(_odd_sample_b_kernel, steps=steps), 4, len(extra))
    st_spec = pl.BlockSpec((None, STATE_BB, STATE_HB, RET_KEY_DIM, RET_VALUE_DIM),
                           lambda i, h: (layer, i, h, 0, 0))
    return pl.pallas_call(
        kern,
        grid=(DB // STATE_BB, RET_HEADS // STATE_HB),
        in_specs=[
            pl.BlockSpec((rows16, STATE_HB * RET_KEY_DIM), lambda i, h: (i, h)),
            pl.BlockSpec((STATE_HB, RET_KEY_DIM, BLK), lambda i, h: (h, 0, i // per_lane_blk)),
            pl.BlockSpec((BLK, STATE_HB * RET_VALUE_DIM), lambda i, h: (i // per_lane_blk, h)),
            st_spec,
        ] + extra_specs,
        out_specs=[
            pl.BlockSpec((rows16, STATE_HB * RET_VALUE_DIM), lambda i, h: (i, h)),
            st_spec,
        ],
        out_shape=[
            jax.ShapeDtypeStruct((R, RET_V_WIDTH), F32),
            jax.ShapeDtypeStruct(state_all.shape, F32),
        ],
        input_output_aliases=aliases,
        compiler_params=pltpu.CompilerParams(
            dimension_semantics=("arbitrary", "arbitrary"), vmem_limit_bytes=VMEM_LIMIT),
        name="odd_sample_b",
    )(q, kdt, v, state_all, *extra)


def _odd_sample_c_kernel(x_ref, intra_ref, cross_ref, gate_ref, wout_ref, fg_ref, y_ref, *, final_norm):
    gated = []
    for h in range(RET_HEADS):
        hs = slice(h * RET_VALUE_DIM, (h + 1) * RET_VALUE_DIM)
        on = _layernorm_nogain(intra_ref[:, hs] + cross_ref[:, hs])
        gated.append((gate_ref[:, hs] * on).astype(BF16))
    y = x_ref[...] + _dot(jnp.concatenate(gated, axis=1), wout_ref[...])
    if final_norm:
        y = _rmsnorm(y, fg_ref[...])
    y_ref[...] = y


def _odd_sample_c(x, intra, cross, gate, wout_all, o, fg, *, final_norm):
    R, D = x.shape
    row = lambda w: pl.BlockSpec((BLK, w), lambda i: (i, 0))
    kern = functools.partial(_odd_sample_c_kernel, final_norm=final_norm)
    return pl.pallas_call(
        kern,
        grid=(R // BLK,),
        in_specs=[row(D), row(RET_V_WIDTH), row(RET_V_WIDTH), row(RET_V_WIDTH),
                  _layer_spec((RET_V_WIDTH, D), o), _const_spec((1, D))],
        out_specs=row(D),
        out_shape=jax.ShapeDtypeStruct((R, D), F32),
        compiler_params=pltpu.CompilerParams(
            dimension_semantics=("arbitrary",), vmem_limit_bytes=VMEM_LIMIT),
        name="odd_sample_c",
    )(x, intra, cross, gate, wout_all, fg)


def _xpos_tables(pos):
    angle = 1.0 / (10000.0 ** jnp.linspace(0.0, 1.0, RET_KEY_DIM // 2, dtype=F32))
    ang = pos.astype(F32)[:, None] * angle[None, :]
    return jnp.cos(ang), jnp.sin(ang)


def _split_pairs_qk(w_qk):
    x = w_qk.reshape(w_qk.shape[0], D_MODEL, 2 * RET_HEADS, RET_KEY_DIM).astype(F32)
    y = jnp.einsum('ldhc,cn->ldhn', x, _pair_split_perm())
    return y.astype(BF16).reshape(w_qk.shape)


PROMPT_TB_EVEN = 512
PROMPT_TB_ODD = 512


def kernel(x_prompt, x_sample, cache_swa_k, cache_swa_v, state_ret, norm_gain, final_norm_gain,
           rel_bias_table, even_w_in, even_w_out, swa_sinks, gmlp_ws, gmlp_bs, gmlp_ln_gain,
           odd_w_in, odd_w_out):
    B, S, D = x_prompt.shape
    DB, T, _ = x_sample.shape
    R = DB * T
    n_even = even_w_in.shape[0]
    tb = PROMPT_TB_ODD

    table_flat = rel_bias_table.reshape(-1)
    gains = norm_gain.reshape(DEPTH, 1, D)
    fg = final_norm_gain.reshape(1, D)
    even_win = even_w_in.astype(BF16)
    even_wout = even_w_out.astype(BF16)
    odd_win = odd_w_in.astype(BF16)
    odd_wout = odd_w_out.astype(BF16)
    odd_wqk = _split_pairs_qk(odd_win[:, :, :2 * RET_QK_WIDTH])
    bst_all = gmlp_bs[:, :, :, None]
    lng_all = gmlp_ln_gain.reshape(n_even, 1, GMLP_WIDTH)
    ws4 = gmlp_ws[:, :, :T, :T].reshape(-1)
    bs4 = gmlp_bs[:, :, :T].reshape(-1)

    prompt_tables = (_xpos_tables(jnp.arange(0, S, tb, dtype=jnp.int32))
                     + _xpos_tables(jnp.arange(tb, dtype=jnp.int32)))
    cos4, sin4 = _xpos_tables(PAST_LEN + jnp.arange(T, dtype=jnp.int32))
    cos8 = jnp.tile(cos4, (PAIR_ROWS // T, 1))
    sin8 = jnp.tile(sin4, (PAIR_ROWS // T, 1))
    split = _pair_split_perm().astype(BF16)

    yp = x_prompt
    ys = x_sample.reshape(R, D)
    ck_all = cache_swa_k.reshape(n_even, DB, WINDOW, SWA_KV_WIDTH)
    cv_all = cache_swa_v.reshape(n_even, DB, WINDOW, SWA_KV_WIDTH)
    new_caches = None
    new_state = None
    kp_l, vp_l, sp_l, gv_l = [], [], [], []
    for layer in range(DEPTH):
        if layer % 2 == 0:
            e = layer // 2
            yp, kp, vp = _even_prompt(yp, gains, layer, e, even_win, even_wout, swa_sinks, table_flat,
                                      gmlp_ws, bst_all, lng_all, tb=PROMPT_TB_EVEN)
            q, kn, vn_, sga, mixb, gv = _even_sample_a(ys, gains, layer, e, even_win, ws4, bs4, lng_all, steps=T)
            attn, nk, nv = _even_sample_b(q, kn, vn_, ck_all, cv_all, e, new_caches,
                                          swa_sinks, table_flat, steps=T)
            new_caches = (nk, nv)
            ys = _even_sample_c(ys, sga, attn, mixb, even_wout, e)
            kp_l.append(kp.reshape(B, WINDOW, SWA_KV_HEADS, SWA_HEAD_DIM))
            vp_l.append(vp.reshape(B, WINDOW, SWA_KV_HEADS, SWA_HEAD_DIM))
            gv_l.append(gv.reshape(DB, T, GMLP_WIDTH))
        else:
            o = layer // 2
            last = layer == DEPTH - 1
            yp, sp = _odd_prompt(yp, gains, layer, o, odd_wqk, odd_win, odd_wout, prompt_tables, fg,
                                 tb=tb, final_norm=last)
            sp = sp.reshape(B, RET_HEADS, 2, RET_KEY_DIM // 2, RET_VALUE_DIM).swapaxes(2, 3).reshape(sp.shape)
            q, kdt, v, intra, gate = _odd_sample_a(ys, gains, layer, o, odd_wqk, odd_win, cos8, sin8, split,
                                                   steps=T)
            cross, ss = _odd_sample_b(q, kdt, v, state_ret, o, None if new_state is None else (new_state,),
                                      steps=T)
            new_state = ss
            ys = _odd_sample_c(ys, intra, cross, gate, odd_wout, o, fg, final_norm=last)
            sp_l.append(sp)
    cache_shape = (n_even, DB, WINDOW, SWA_KV_HEADS, SWA_HEAD_DIM)
    return (yp, ys.reshape(DB, T, D), jnp.stack(kp_l), jnp.stack(vp_l),
            new_caches[0].reshape(cache_shape), new_caches[1].reshape(cache_shape),
            jnp.stack(sp_l), new_state, jnp.stack(gv_l))
```

```python
import functools
import math

import jax
import jax.numpy as jnp
from jax import lax
from jax.experimental import pallas as pl
from jax.experimental.pallas import tpu as pltpu

D_MODEL = 1024
DEPTH = 4
PAST_LEN = 8192
NORM_EPS = 1e-6
NEG_INF = -1e30

SWA_HEADS = 8
SWA_KV_HEADS = 2
SWA_HEAD_DIM = 64
SWA_GROUP = SWA_HEADS // SWA_KV_HEADS
SWA_WIDTH = SWA_HEADS * SWA_HEAD_DIM
SWA_KV_WIDTH = SWA_KV_HEADS * SWA_HEAD_DIM
WINDOW = 128
REL_BUCKETS = 32
REL_MAX_DIST = 128

GMLP_GROUPS = 4
GMLP_CHUNK = 128
GMLP_WIDTH = D_MODEL // 2
GMLP_GROUP_DIM = GMLP_WIDTH // GMLP_GROUPS

E_Q = 0
E_K = E_Q + SWA_WIDTH
E_V = E_K + SWA_KV_WIDTH
E_GA = E_V + SWA_KV_WIDTH
E_U = E_GA + SWA_WIDTH
E_VB = E_U + GMLP_WIDTH
E_GB = E_VB + GMLP_WIDTH
EVEN_IN = E_GB + GMLP_WIDTH
EVEN_MIX = SWA_WIDTH + GMLP_WIDTH

RET_HEADS = 4
RET_KEY_DIM = 256
RET_VALUE_DIM = 512
RET_QK_WIDTH = RET_HEADS * RET_KEY_DIM
RET_V_WIDTH = RET_HEADS * RET_VALUE_DIM
RET_CHUNK = 128
ODD_IN = 2 * RET_QK_WIDTH + 2 * RET_V_WIDTH

LANES = 128
BLK = 128
VMEM_LIMIT = 56 * 1024 * 1024

F32 = jnp.float32
BF16 = jnp.bfloat16


def _ret_log_gamma(h):
    return math.log(1.0 - 2.0 ** (-5.0 - h))


def _dot(a, b):
    return jnp.dot(a, b, preferred_element_type=F32)


def _dot_nt(a, b):
    return lax.dot_general(a, b, (((1,), (1,)), ((), ())), preferred_element_type=F32)


def _dot_tn(a, b):
    return lax.dot_general(a, b, (((0,), (0,)), ((), ())), preferred_element_type=F32)


def _silu(x):
    return x * (1.0 / (1.0 + jnp.exp(-x)))


def _rmsnorm(x, g):
    ms = jnp.mean(x * x, axis=-1, keepdims=True)
    return x * lax.rsqrt(ms + NORM_EPS) * g


def _layernorm_nogain(x):
    mu = jnp.mean(x, axis=-1, keepdims=True)
    d = x - mu
    var = jnp.mean(d * d, axis=-1, keepdims=True)
    return d * lax.rsqrt(var + NORM_EPS)


def _iota(shape, dim):
    return lax.broadcasted_iota(jnp.int32, shape, dim)


def _t5_bias(dist, table_ref, head):
    n = jnp.maximum(dist, 0)
    max_exact = REL_BUCKETS // 2
    nf = jnp.maximum(n, 1).astype(F32)
    large = max_exact + (jnp.log(nf / max_exact) / math.log(REL_MAX_DIST / max_exact)
                         * (REL_BUCKETS - max_exact)).astype(jnp.int32)
    large = jnp.minimum(large, REL_BUCKETS - 1)
    bucket = jnp.where(n < max_exact, n, large)
    acc = jnp.zeros(dist.shape, F32)
    for b in range(REL_BUCKETS):
        acc = jnp.where(bucket >= b, table_ref[b * SWA_HEADS + head], acc)
    return acc


def _split_heads_kv(x):
    lo = _iota(x.shape, 1) < SWA_HEAD_DIM
    xr = pltpu.roll(x, SWA_HEAD_DIM, 1)
    a0 = jnp.where(lo, x, 0.0).astype(BF16)
    b0 = jnp.where(lo, 0.0, xr).astype(BF16)
    a1 = jnp.where(lo, xr, 0.0).astype(BF16)
    b1 = jnp.where(lo, 0.0, x).astype(BF16)
    return (a0, b0), (a1, b1)


def _sink_softmax_parts(parts, sink):
    m = sink
    for t in parts:
        m = jnp.maximum(m, jnp.max(t, axis=-1, keepdims=True))
    es = [jnp.exp(t - m) for t in parts]
    den = jnp.exp(sink - m)
    for e in es:
        den = den + jnp.sum(e, axis=-1, keepdims=True)
    inv = 1.0 / den
    return [e * inv for e in es]


def _xpos_rotate_split(x, cos, sin):
    half = x.shape[1] // 2
    x0 = x[:, :half]
    x1 = x[:, half:]
    return jnp.concatenate([x0 * cos - x1 * sin, x1 * cos + x0 * sin], axis=1)


def _layer_spec(shape, layer, **kwargs):
    nd = len(shape)
    return pl.BlockSpec((None,) + tuple(shape), lambda *_: (layer,) + (0,) * nd, **kwargs)


def _const_spec(shape):
    nd = len(shape)
    return pl.BlockSpec(shape, lambda *_: (0,) * nd)


def _smem_spec():
    return pl.BlockSpec(memory_space=pltpu.SMEM)


EVEN_SUB_GROUP = 2

def _even_prompt_kernel(sinks_ref, table_ref, x_ref, g_ref, win_ref, wout_ref, ws_ref, bst_ref, lng_ref,
                        y_ref, newk_ref, newv_ref,
                        z_ref, qb_ref, vn_ref, mix_ref, ks_ref, vs_ref, bias_ref, wm_ref, *, tb, layer,
                        side_work=None):
    b = pl.program_id(0)
    j = pl.program_id(1)
    nsub = tb // BLK

    @pl.when(jnp.logical_and(b == 0, j == 0))
    def _init_tables():
        qi = _iota((BLK, BLK), 0)
        c = _iota((BLK, BLK), 1)
        own = c <= qi
        dist = jnp.where(own, qi - c, qi + BLK - c)
        for h in range(SWA_HEADS):
            bias = _t5_bias(dist, table_ref, h)
            bias_ref[0, h] = bias
            bias_ref[1, h] = jnp.where(own, bias, NEG_INF)
        causal = _iota((BLK, BLK), 0) >= _iota((BLK, BLK), 1)
        for g in range(GMLP_GROUPS):
            wm_ref[g] = jnp.where(causal, ws_ref[g], 0.0).astype(BF16)

    @pl.when(j == 0)
    def _zero_prev():
        zeros = jnp.zeros((BLK, LANES), BF16)
        for h in range(SWA_KV_HEADS):
            for ab in range(2):
                ks_ref[h, ab, 0:BLK, :] = zeros
                vs_ref[h, ab, 0:BLK, :] = zeros

    x = x_ref[0]
    xn = _rmsnorm(x, g_ref[...]).astype(BF16)
    z_ref[...] = _dot(xn, win_ref[...])
    qb_ref[...] = (z_ref[:, E_Q:E_Q + SWA_WIDTH] * (SWA_HEAD_DIM ** -0.5)).astype(BF16)
    z_ref[:, E_GA:E_GA + SWA_WIDTH] = _silu(z_ref[:, E_GA:E_GA + SWA_WIDTH])
    z_ref[:, E_GB:E_GB + GMLP_WIDTH] = _silu(z_ref[:, E_GB:E_GB + GMLP_WIDTH])
    vn_ref[...] = (_layernorm_nogain(z_ref[:, E_VB:E_VB + GMLP_WIDTH]) * lng_ref[...]).astype(BF16)
    kparts = _split_heads_kv(z_ref[:, E_K:E_K + SWA_KV_WIDTH])
    vparts = _split_heads_kv(z_ref[:, E_V:E_V + SWA_KV_WIDTH])
    for h in range(SWA_KV_HEADS):
        for ab in range(2):
            ks_ref[h, ab, BLK:BLK + tb, :] = kparts[h][ab]
            vs_ref[h, ab, BLK:BLK + tb, :] = vparts[h][ab]

    if side_work is not None:
        side_work()

    def sub_group(sg, carry):
        own = _iota((BLK, BLK), 1) <= _iota((BLK, BLK), 0)
        col0 = lambda h, p: h * SWA_GROUP * SWA_HEAD_DIM + p * LANES
        units = []
        for i in range(EVEN_SUB_GROUP):
            s = sg * EVEN_SUB_GROUP + i
            r0 = pl.multiple_of(s * BLK, BLK)
            first = jnp.where(j * nsub + s == 0, 1, 0)
            for h in range(SWA_KV_HEADS):
                for p in range(SWA_GROUP // 2):
                    units.append((pl.ds(r0, BLK), pl.ds(r0, 2 * BLK), first, h, p))
        stacked = lambda ref, h, pc: jnp.concatenate([ref[h, 0, pc, :], ref[h, 1, pc, :]], axis=0)
        logits, sinks = [], []
        for rows, prev_cur, first, h, p in units:
            c0 = col0(h, p)
            sc = _dot_nt(qb_ref[rows, c0:c0 + LANES], stacked(ks_ref, h, prev_cur))
            for gi in range(2):
                hd = h * SWA_GROUP + p * 2 + gi
                s_prev = sc[:, (2 * gi) * BLK:(2 * gi + 1) * BLK]
                s_cur = sc[:, (2 * gi + 1) * BLK:(2 * gi + 2) * BLK]
                logits.append(jnp.where(own, s_cur, s_prev) + bias_ref[first, hd])
                sinks.append(sinks_ref[layer, hd])
        maxes = [jnp.maximum(jnp.max(t, axis=-1, keepdims=True), sk) for t, sk in zip(logits, sinks)]
        exps = [jnp.exp(t - m) for t, m in zip(logits, maxes)]
        dens = [jnp.sum(e, axis=-1, keepdims=True) + jnp.exp(sk - m) for e, sk, m in zip(exps, sinks, maxes)]
        probs = [e * (1.0 / d) for e, d in zip(exps, dens)]
        for n, (rows, prev_cur, first, h, p) in enumerate(units):
            c0 = col0(h, p)
            parts = []
            for pr in probs[2 * n:2 * n + 2]:
                parts.append(jnp.where(own, 0.0, pr).astype(BF16))
                parts.append(jnp.where(own, pr, 0.0).astype(BF16))
            o = _dot(jnp.concatenate(parts, axis=1), stacked(vs_ref, h, prev_cur))
            mix_ref[rows, c0:c0 + LANES] = (z_ref[rows, E_GA + c0:E_GA + c0 + LANES] * o).astype(BF16)

        for i in range(EVEN_SUB_GROUP):
            rows = pl.ds(pl.multiple_of((sg * EVEN_SUB_GROUP + i) * BLK, BLK), BLK)
            for g in range(GMLP_GROUPS):
                gs = slice(g * GMLP_GROUP_DIM, (g + 1) * GMLP_GROUP_DIM)
                sp = _dot(wm_ref[g], vn_ref[rows, gs]) + bst_ref[g]
                u = z_ref[rows, E_U + g * GMLP_GROUP_DIM:E_U + (g + 1) * GMLP_GROUP_DIM]
                sgb = z_ref[rows, E_GB + g * GMLP_GROUP_DIM:E_GB + (g + 1) * GMLP_GROUP_DIM]
                mix_ref[rows, SWA_WIDTH + g * GMLP_GROUP_DIM:SWA_WIDTH + (g + 1) * GMLP_GROUP_DIM] = (
                    sgb * (u * sp)).astype(BF16)
        return carry

    lax.fori_loop(0, nsub // EVEN_SUB_GROUP, sub_group, 0)

    for h in range(SWA_KV_HEADS):
        for ab in range(2):
            ks_ref[h, ab, 0:BLK, :] = ks_ref[h, ab, tb:tb + BLK, :]
            vs_ref[h, ab, 0:BLK, :] = vs_ref[h, ab, tb:tb + BLK, :]

    y_ref[0] = x + _dot(mix_ref[...], wout_ref[...])

    @pl.when(j == pl.num_programs(1) - 1)
    def _emit_cache():
        newk_ref[0] = z_ref[tb - WINDOW:tb, E_K:E_K + SWA_KV_WIDTH]
        newv_ref[0] = z_ref[tb - WINDOW:tb, E_V:E_V + SWA_KV_WIDTH]


N_EVEN_PROMPT_INPUTS = 9
N_STATE_INPUTS = 4


def _even_prompt_state_kernel(*refs, tb, layer, steps, n_alias):
    prompt_in = refs[:N_EVEN_PROMPT_INPUTS]
    state_in = refs[N_EVEN_PROMPT_INPUTS:N_EVEN_PROMPT_INPUTS + N_STATE_INPUTS]
    rest = refs[N_EVEN_PROMPT_INPUTS + N_STATE_INPUTS + n_alias:]
    y_ref, newk_ref, newv_ref, cross_ref, snew_ref = rest[:5]
    lin = pl.program_id(0) * pl.num_programs(1) + pl.program_id(1)
    state_work = functools.partial(_retention_state_block, lin // STATE_HEAD_GROUPS, lin % STATE_HEAD_GROUPS,
                                   *state_in, cross_ref, snew_ref, steps=steps)
    _even_prompt_kernel(*prompt_in, y_ref, newk_ref, newv_ref, *rest[5:], tb=tb, layer=layer,
                        side_work=state_work)


def _even_prompt(x, gains, layer, e, win_all, wout_all, sinks, table_flat, ws_all, bst_all, lng_all,
                 state_job, *, tb, steps):
    B, S, D = x.shape
    nj = S // tb
    q, kdt, v, state_all, o, prev_state = state_job
    R = q.shape[0]
    DB = state_all.shape[1]
    assert B * nj == (DB // STATE_BB) * STATE_HEAD_GROUPS, "one retention-state block per prompt grid step"
    rows16 = STATE_BB * steps
    per_lane_blk = BLK // rows16
    sblk = lambda b, j: (b * nj + j) // STATE_HEAD_GROUPS
    shg = lambda b, j: (b * nj + j) % STATE_HEAD_GROUPS
    st_spec = pl.BlockSpec((None, STATE_BB, STATE_HB, RET_KEY_DIM, RET_VALUE_DIM),
                           lambda b, j: (o, sblk(b, j), shg(b, j), 0, 0))
    extra, extra_specs, aliases = _layer_slab_args(None if prev_state is None else (prev_state,), 4,
                                                   N_EVEN_PROMPT_INPUTS + N_STATE_INPUTS)
    once = dict(pipeline_mode=pl.Buffered(1))
    kern = functools.partial(_even_prompt_state_kernel, tb=tb, layer=e, steps=steps, n_alias=len(extra))
    return pl.pallas_call(
        kern,
        grid=(B, nj),
        in_specs=[
            _smem_spec(), _smem_spec(),
            pl.BlockSpec((1, tb, D), lambda b, j: (b, j, 0)),
            _layer_spec((1, D), layer),
            _layer_spec((D, EVEN_IN), e, **once),
            _layer_spec((EVEN_MIX, D), e, **once),
            _layer_spec((GMLP_GROUPS, GMLP_CHUNK, GMLP_CHUNK), e),
            _layer_spec((GMLP_GROUPS, GMLP_CHUNK, 1), e),
            _layer_spec((1, GMLP_WIDTH), e),
            pl.BlockSpec((rows16, STATE_HB * RET_KEY_DIM), lambda b, j: (sblk(b, j), shg(b, j))),
            pl.BlockSpec((STATE_HB, RET_KEY_DIM, BLK), lambda b, j: (shg(b, j), 0, sblk(b, j) // per_lane_blk)),
            pl.BlockSpec((BLK, STATE_HB * RET_VALUE_DIM), lambda b, j: (sblk(b, j) // per_lane_blk, shg(b, j))),
            st_spec,
        ] + extra_specs,
        out_specs=[
            pl.BlockSpec((1, tb, D), lambda b, j: (b, j, 0)),
            pl.BlockSpec((1, WINDOW, SWA_KV_WIDTH), lambda b, j: (b, 0, 0)),
            pl.BlockSpec((1, WINDOW, SWA_KV_WIDTH), lambda b, j: (b, 0, 0)),
            pl.BlockSpec((rows16, STATE_HB * RET_VALUE_DIM), lambda b, j: (sblk(b, j), shg(b, j))),
            st_spec,
        ],
        out_shape=[
            jax.ShapeDtypeStruct((B, S, D), F32),
            jax.ShapeDtypeStruct((B, WINDOW, SWA_KV_WIDTH), F32),
            jax.ShapeDtypeStruct((B, WINDOW, SWA_KV_WIDTH), F32),
            jax.ShapeDtypeStruct((R, RET_V_WIDTH), F32),
            jax.ShapeDtypeStruct(state_all.shape, F32),
        ],
        input_output_aliases=aliases,
        scratch_shapes=[
            pltpu.VMEM((tb, EVEN_IN), F32),
            pltpu.VMEM((tb, SWA_WIDTH), BF16),
            pltpu.VMEM((tb, GMLP_WIDTH), BF16),
            pltpu.VMEM((tb, EVEN_MIX), BF16),
            pltpu.VMEM((SWA_KV_HEADS, 2, BLK + tb, LANES), BF16),
            pltpu.VMEM((SWA_KV_HEADS, 2, BLK + tb, LANES), BF16),
            pltpu.VMEM((2, SWA_HEADS, BLK, BLK), F32),
            pltpu.VMEM((GMLP_GROUPS, GMLP_CHUNK, GMLP_CHUNK), BF16),
        ],
        compiler_params=pltpu.CompilerParams(
            dimension_semantics=("arbitrary", "arbitrary"), vmem_limit_bytes=VMEM_LIMIT),
        name="even_prompt",
    )(sinks, table_flat, x, gains, win_all, wout_all, ws_all, bst_all, lng_all, q, kdt, v, state_all, *extra)


def _odd_prompt_kernel(x_ref, g_ref, wqk_ref, wv_ref, wg_ref, wout_ref, cblk_ref, sblk_ref, crow_ref, srow_ref,
                       fg_ref, y_ref, sout_ref,
                       xn_ref, zqk_ref, qb_ref, kb_ref, kd_ref, vb_ref, sg_ref, s_ref, gated_ref, cos_ref, sin_ref,
                       decay_ref, qdec_ref, kdec_ref, *, tb, final_norm):
    b = pl.program_id(0)
    j = pl.program_id(1)
    nchunk = tb // BLK

    @pl.when(jnp.logical_and(b == 0, j == 0))
    def _init_tables():
        ii = _iota((BLK, BLK), 0)
        jj = _iota((BLK, BLK), 1)
        diff = (ii - jj).astype(F32)
        idx = _iota((BLK, 1), 0).astype(F32)
        idx_blk = (_iota((tb, 1), 0) % BLK).astype(F32)
        for h in range(RET_HEADS):
            lg = _ret_log_gamma(h)
            decay_ref[h] = jnp.where(diff >= 0, jnp.exp(lg * jnp.maximum(diff, 0.0)), 0.0)
            qdec_ref[h] = jnp.exp(lg * (idx + 1.0))
            kdec_ref[h] = jnp.exp(lg * (BLK - 1.0 - idx_blk))

    @pl.when(j == 0)
    def _zero_state():
        s_ref[...] = jnp.zeros(s_ref.shape, F32)

    cb = cblk_ref[pl.ds(j, 1), :]
    sb = sblk_ref[pl.ds(j, 1), :]
    cos_ref[...] = cb * crow_ref[...] - sb * srow_ref[...]
    sin_ref[...] = sb * crow_ref[...] + cb * srow_ref[...]

    heads = range(RET_HEADS)
    xn_ref[...] = _rmsnorm(x_ref[0], g_ref[...]).astype(BF16)
    zqk_ref[...] = _dot(xn_ref[...], wqk_ref[...])
    vb_ref[...] = _dot(xn_ref[...], wv_ref[...]).astype(BF16)
    sg_ref[...] = _silu(_dot(xn_ref[...], wg_ref[...]))
    cs = cos_ref[...]
    sn = sin_ref[...]
    for h in heads:
        qcols = slice(h * RET_KEY_DIM, (h + 1) * RET_KEY_DIM)
        kcols = slice(RET_QK_WIDTH + h * RET_KEY_DIM, RET_QK_WIDTH + (h + 1) * RET_KEY_DIM)
        qb_ref[:, qcols] = _xpos_rotate_split(zqk_ref[:, qcols], cs, sn).astype(BF16)
        kr = _xpos_rotate_split(zqk_ref[:, kcols], cs, sn) * (RET_KEY_DIM ** -0.5)
        kb_ref[:, qcols] = kr.astype(BF16)
        kd_ref[:, qcols] = (kr * kdec_ref[h]).astype(BF16)

    def chunk(c, carry):
        rows = pl.ds(pl.multiple_of(c * BLK, BLK), BLK)
        qcols = [slice(h * RET_KEY_DIM, (h + 1) * RET_KEY_DIM) for h in heads]
        vcols = [slice(h * RET_VALUE_DIM, (h + 1) * RET_VALUE_DIM) for h in heads]
        qb = [qb_ref[rows, qcols[h]] for h in heads]
        vb = [vb_ref[rows, vcols[h]] for h in heads]
        att = [(_dot_nt(qb[h], kb_ref[rows, qcols[h]]) * decay_ref[h]).astype(BF16) for h in heads]
        cross = [_dot(qb[h], s_ref[h].astype(BF16)) * qdec_ref[h] for h in heads]
        for h in heads:
            o_ref[rows, vcols[h]] = _dot(att[h], vb[h]) + cross[h]
        for h in heads:
            s_ref[h] = (math.exp(_ret_log_gamma(h) * BLK) * s_ref[h]
                        + _dot_tn(kd_ref[rows, qcols[h]], vb[h]))
        return carry

    o_ref = zqk_ref
    lax.fori_loop(0, nchunk, chunk, 0)

    tiles = [(slice(c * BLK, (c + 1) * BLK), slice(h * RET_VALUE_DIM, (h + 1) * RET_VALUE_DIM))
             for c in range(nchunk) for h in heads]
    normed = [_layernorm_nogain(o_ref[r, v]) for r, v in tiles]
    for (r, v), on in zip(tiles, normed):
        gated_ref[r, v] = (sg_ref[r, v] * on).astype(BF16)

    y = x_ref[0] + _dot(gated_ref[...], wout_ref[...])
    if final_norm:
        y = _rmsnorm(y, fg_ref[...])
    y_ref[0] = y

    @pl.when(j == pl.num_programs(1) - 1)
    def _emit_state():
        sout_ref[0] = s_ref[...]


def _odd_prompt(x, gains, layer, o, wqk_all, win_all, wout_all, tables, fg, *, tb, final_norm):
    B, S, D = x.shape
    cblk, sblk, crow, srow = tables
    kern = functools.partial(_odd_prompt_kernel, tb=tb, final_norm=final_norm)
    once = dict(pipeline_mode=pl.Buffered(1))
    half = RET_KEY_DIM // 2
    return pl.pallas_call(
        kern,
        grid=(B, S // tb),
        in_specs=[
            pl.BlockSpec((1, tb, D), lambda b, j: (b, j, 0)),
            _layer_spec((1, D), layer),
            _layer_spec((D, 2 * RET_QK_WIDTH), o, **once),
            pl.BlockSpec((None, D, RET_V_WIDTH), lambda b, j: (o, 0, 1), **once),
            pl.BlockSpec((None, D, RET_V_WIDTH), lambda b, j: (o, 0, 2), **once),
            _layer_spec((RET_V_WIDTH, D), o, **once),
            _const_spec((S // tb, half)), _const_spec((S // tb, half)),
            _const_spec((tb, half)), _const_spec((tb, half)),
            _const_spec((1, D)),
        ],
        out_specs=[
            pl.BlockSpec((1, tb, D), lambda b, j: (b, j, 0)),
            pl.BlockSpec((1, RET_HEADS, RET_KEY_DIM, RET_VALUE_DIM), lambda b, j: (b, 0, 0, 0)),
        ],
        out_shape=[
            jax.ShapeDtypeStruct((B, S, D), F32),
            jax.ShapeDtypeStruct((B, RET_HEADS, RET_KEY_DIM, RET_VALUE_DIM), F32),
        ],
        scratch_shapes=[
            pltpu.VMEM((tb, D), BF16),
            pltpu.VMEM((tb, 2 * RET_QK_WIDTH), F32),
            pltpu.VMEM((tb, RET_QK_WIDTH), BF16),
            pltpu.VMEM((tb, RET_QK_WIDTH), BF16),
            pltpu.VMEM((tb, RET_QK_WIDTH), BF16),
            pltpu.VMEM((tb, RET_V_WIDTH), BF16),
            pltpu.VMEM((tb, RET_V_WIDTH), F32),
            pltpu.VMEM((RET_HEADS, RET_KEY_DIM, RET_VALUE_DIM), F32),
            pltpu.VMEM((tb, RET_V_WIDTH), BF16),
            pltpu.VMEM((tb, half), F32),
            pltpu.VMEM((tb, half), F32),
            pltpu.VMEM((RET_HEADS, BLK, BLK), F32),
            pltpu.VMEM((RET_HEADS, BLK, 1), F32),
            pltpu.VMEM((RET_HEADS, tb, 1), F32),
        ],
        compiler_params=pltpu.CompilerParams(
            dimension_semantics=("arbitrary", "arbitrary"), vmem_limit_bytes=VMEM_LIMIT),
        name="odd_prompt",
    )(x, gains, wqk_all, win_all, win_all, wout_all, cblk, sblk, crow, srow, fg)


def _same_batch_causal(shape, row0=0, col0=0, steps=4):
    r = _iota(shape, 0) + row0
    c = _iota(shape, 1) + col0
    return (r // steps == c // steps) & (c % steps <= r % steps)


def _even_sample_a_kernel(ws4_ref, bs4_ref, x_ref, g_ref, win_ref, lng_ref,
                          q_ref, k_ref, v_ref, sga_ref, mixb_ref, vn_ref, *, steps, layer):
    x = x_ref[...]
    xn = _rmsnorm(x, g_ref[...]).astype(BF16)
    z = _dot(xn, win_ref[...])
    q_ref[...] = z[:, E_Q:E_Q + SWA_WIDTH] * (SWA_HEAD_DIM ** -0.5)
    k_ref[...] = z[:, E_K:E_K + SWA_KV_WIDTH]
    v_ref[...] = z[:, E_V:E_V + SWA_KV_WIDTH]
    sga_ref[...] = _silu(z[:, E_GA:E_GA + SWA_WIDTH])
    vn = _layernorm_nogain(z[:, E_VB:E_VB + GMLP_WIDTH]) * lng_ref[...]
    vn_ref[...] = vn
    vnb = vn.astype(BF16)
    same = _iota((BLK, BLK), 0) // steps == _iota((BLK, BLK), 1) // steps
    rstep = _iota((BLK, BLK), 0) % steps
    cstep = _iota((BLK, BLK), 1) % steps
    rstep_col = _iota((BLK, 1), 0) % steps
    for g in range(GMLP_GROUPS):
        gs = slice(g * GMLP_GROUP_DIM, (g + 1) * GMLP_GROUP_DIM)
        wk = jnp.zeros((BLK, BLK), F32)
        bcol = jnp.zeros((BLK, 1), F32)
        for p in range(steps):
            bcol = jnp.where(rstep_col == p, bs4_ref[(layer * GMLP_GROUPS + g) * steps + p], bcol)
            for q in range(p + 1):
                w = ws4_ref[((layer * GMLP_GROUPS + g) * steps + p) * steps + q]
                wk = jnp.where(same & (rstep == p) & (cstep == q), w, wk)
        sp = _dot(wk.astype(BF16), vnb[:, gs]) + bcol
        u = z[:, E_U + g * GMLP_GROUP_DIM:E_U + (g + 1) * GMLP_GROUP_DIM]
        gb = z[:, E_GB + g * GMLP_GROUP_DIM:E_GB + (g + 1) * GMLP_GROUP_DIM]
        mixb_ref[:, gs] = (_silu(gb) * (u * sp)).astype(BF16)


def _even_sample_a(x, gains, layer, e, win_all, ws4, bs4, lng_all, *, steps):
    R, D = x.shape
    row = lambda w: pl.BlockSpec((BLK, w), lambda i: (i, 0))
    kern = functools.partial(_even_sample_a_kernel, steps=steps, layer=e)
    return pl.pallas_call(
        kern,
        grid=(R // BLK,),
        in_specs=[_smem_spec(), _smem_spec(), row(D), _layer_spec((1, D), layer), _layer_spec((D, EVEN_IN), e),
                  _layer_spec((1, GMLP_WIDTH), e)],
        out_specs=[row(SWA_WIDTH), row(SWA_KV_WIDTH), row(SWA_KV_WIDTH), row(SWA_WIDTH),
                   row(GMLP_WIDTH), row(GMLP_WIDTH)],
        out_shape=[
            jax.ShapeDtypeStruct((R, SWA_WIDTH), F32),
            jax.ShapeDtypeStruct((R, SWA_KV_WIDTH), F32),
            jax.ShapeDtypeStruct((R, SWA_KV_WIDTH), F32),
            jax.ShapeDtypeStruct((R, SWA_WIDTH), F32),
            jax.ShapeDtypeStruct((R, GMLP_WIDTH), BF16),
            jax.ShapeDtypeStruct((R, GMLP_WIDTH), F32),
        ],
        compiler_params=pltpu.CompilerParams(
            dimension_semantics=("arbitrary",), vmem_limit_bytes=VMEM_LIMIT),
        name="even_sample_a",
    )(ws4, bs4, x, gains, win_all, lng_all)


SAMPLE_BB = 32
PAIR_ROWS = 8
SAMPLE_PAIRS_PER_ITER = 2


def _even_sample_b_kernel(sinks_ref, table_ref, q_ref, kn_ref, vn_ref, ck_ref, cv_ref,
                          attn_ref, newk_ref, newv_ref,
                          sn_ref, kns_ref, vns_ref, biasc_ref, biasn_ref, *, steps, layer):
    kparts = _split_heads_kv(kn_ref[...])
    vparts = _split_heads_kv(vn_ref[...])
    for h in range(SWA_KV_HEADS):
        kns_ref[h] = jnp.concatenate(kparts[h], axis=0)
        vns_ref[h] = jnp.concatenate(vparts[h], axis=0)
        c0 = h * SWA_GROUP * SWA_HEAD_DIM
        lhs = jnp.concatenate([q_ref[:, c0:c0 + LANES], q_ref[:, c0 + LANES:c0 + 2 * LANES]], axis=0)
        sn_ref[h] = _dot_nt(lhs.astype(BF16), kns_ref[h])

    step_r = _iota((PAIR_ROWS, BLK), 0) % steps
    lane = _iota((PAIR_ROWS, BLK), 1)
    dist_c = step_r + WINDOW - lane
    valid_c = (dist_c >= 0) & (dist_c < WINDOW)
    dist_n = step_r - lane % steps
    first_local = _iota((PAIR_ROWS, 1), 0) < steps
    for hd in range(SWA_HEADS):
        biasc_ref[hd] = _t5_bias(dist_c, table_ref, hd)
        biasn_ref[hd] = _t5_bias(dist_n, table_ref, hd)

    first16 = jnp.concatenate([first_local, first_local], axis=0)

    def pair_group(it, carry):
        units = []
        for k in range(SAMPLE_PAIRS_PER_ITER):
            i = it * SAMPLE_PAIRS_PER_ITER + k
            r0 = pl.multiple_of(i * PAIR_ROWS, PAIR_ROWS)
            valid_n = _same_batch_causal((PAIR_ROWS, BLK), row0=r0, steps=steps)
            kc = [_split_heads_kv(ck_ref[2 * i + e]) for e in range(2)]
            vc = [_split_heads_kv(cv_ref[2 * i + e]) for e in range(2)]
            for h in range(SWA_KV_HEADS):
                units.append((r0, valid_n, h, [kc[e][h] for e in range(2)], [vc[e][h] for e in range(2)]))
            for e in range(2):
                be = 2 * i + e
                newk_ref[be, 0:WINDOW - steps, :] = ck_ref[be, steps:WINDOW, :]
                newv_ref[be, 0:WINDOW - steps, :] = cv_ref[be, steps:WINDOW, :]
                newk_ref[be, WINDOW - steps:WINDOW, :] = kn_ref[pl.ds(r0 + e * steps, steps), :]
                newv_ref[be, WINDOW - steps:WINDOW, :] = vn_ref[pl.ds(r0 + e * steps, steps), :]

        scores = []
        for r0, valid_n, h, kch, vch in units:
            rows = pl.ds(r0, PAIR_ROWS)
            c0 = h * SWA_GROUP * SWA_HEAD_DIM
            lhs = jnp.concatenate([q_ref[rows, c0:c0 + LANES], q_ref[rows, c0 + LANES:c0 + 2 * LANES]],
                                  axis=0).astype(BF16)
            sc_e = [_dot_nt(lhs, jnp.concatenate(kch[e], axis=0)) for e in range(2)]
            scores.append(jnp.where(first16, sc_e[0], sc_e[1]))

        tcs, tns, sks = [], [], []
        for (r0, valid_n, h, kch, vch), sc in zip(units, scores):
            for p in range(2):
                pr = slice(p * PAIR_ROWS, (p + 1) * PAIR_ROWS)
                for gi in range(2):
                    hd = h * SWA_GROUP + p * 2 + gi
                    gc = slice(gi * BLK, (gi + 1) * BLK)
                    tcs.append(jnp.where(valid_c, sc[pr, gc] + biasc_ref[hd], NEG_INF))
                    tn_raw = sn_ref[h, pl.ds(p * BLK + r0, PAIR_ROWS), gc]
                    tns.append(jnp.where(valid_n, tn_raw + biasn_ref[hd], NEG_INF))
                    sks.append(sinks_ref[layer, hd])
        ms = [jnp.maximum(jnp.maximum(jnp.max(tc, axis=-1, keepdims=True), jnp.max(tn, axis=-1, keepdims=True)), sk)
              for tc, tn, sk in zip(tcs, tns, sks)]
        ecs = [jnp.exp(tc - m) for tc, m in zip(tcs, ms)]
        ens = [jnp.exp(tn - m) for tn, m in zip(tns, ms)]
        invs = [1.0 / (jnp.exp(sk - m) + jnp.sum(ec, axis=-1, keepdims=True) + jnp.sum(en, axis=-1, keepdims=True))
                for sk, m, ec, en in zip(sks, ms, ecs, ens)]
        pcs = [ec * inv for ec, inv in zip(ecs, invs)]
        pns = [en * inv for en, inv in zip(ens, invs)]

        for n, (r0, valid_n, h, kch, vch) in enumerate(units):
            rows = pl.ds(r0, PAIR_ROWS)
            c0 = h * SWA_GROUP * SWA_HEAD_DIM
            grid16 = lambda ps: jnp.concatenate(
                [jnp.concatenate(ps[4 * n + 2 * p:4 * n + 2 * p + 2], axis=1) for p in range(2)], axis=0).astype(BF16)
            pc16 = grid16(pcs)
            pn16 = grid16(pns)
            o_e = [_dot(pc16, jnp.concatenate(vch[e], axis=0)) for e in range(2)]
            o = jnp.where(first16, o_e[0], o_e[1]) + _dot(pn16, vns_ref[h])
            attn_ref[rows, c0:c0 + LANES] = o[0:PAIR_ROWS]
            attn_ref[rows, c0 + LANES:c0 + 2 * LANES] = o[PAIR_ROWS:2 * PAIR_ROWS]
        return carry

    lax.fori_loop(0, SAMPLE_BB // (2 * SAMPLE_PAIRS_PER_ITER), pair_group, 0)


def _ignoring_refs(kernel_fn, start, count):
    if count == 0:
        return kernel_fn

    def wrapped(*refs):
        return kernel_fn(*refs[:start], *refs[start + count:])
    return wrapped


def _layer_slab_args(prev_outs, first_out_index, n_fixed_inputs):
    if prev_outs is None:
        return [], [], {}
    specs = [pl.BlockSpec(memory_space=pl.ANY) for _ in prev_outs]
    aliases = {n_fixed_inputs + k: first_out_index + k for k in range(len(prev_outs))}
    return list(prev_outs), specs, aliases


def _even_sample_b(q, kn, vn, ck_all, cv_all, layer, prev_outs, sinks, table_flat, *, steps):
    R = q.shape[0]
    row = lambda w: pl.BlockSpec((BLK, w), lambda i: (i, 0))
    cache = pl.BlockSpec((None, SAMPLE_BB, WINDOW, SWA_KV_WIDTH), lambda i: (layer, i, 0, 0))
    extra, extra_specs, aliases = _layer_slab_args(prev_outs, 1, 7)
    kern = _ignoring_refs(functools.partial(_even_sample_b_kernel, steps=steps, layer=layer), 7, len(extra))
    return pl.pallas_call(
        kern,
        grid=(R // BLK,),
        in_specs=[_smem_spec(), _smem_spec(), row(SWA_WIDTH), row(SWA_KV_WIDTH), row(SWA_KV_WIDTH), cache, cache]
        + extra_specs,
        out_specs=[row(SWA_WIDTH), cache, cache],
        out_shape=[
            jax.ShapeDtypeStruct((R, SWA_WIDTH), F32),
            jax.ShapeDtypeStruct(ck_all.shape, F32),
            jax.ShapeDtypeStruct(cv_all.shape, F32),
        ],
        input_output_aliases=aliases,
        scratch_shapes=[
            pltpu.VMEM((SWA_KV_HEADS, 2 * BLK, 2 * BLK), F32),
            pltpu.VMEM((SWA_KV_HEADS, 2 * BLK, LANES), BF16),
            pltpu.VMEM((SWA_KV_HEADS, 2 * BLK, LANES), BF16),
            pltpu.VMEM((SWA_HEADS, PAIR_ROWS, BLK), F32),
            pltpu.VMEM((SWA_HEADS, PAIR_ROWS, BLK), F32),
        ],
        compiler_params=pltpu.CompilerParams(
            dimension_semantics=("arbitrary",), vmem_limit_bytes=VMEM_LIMIT),
        name="even_sample_b",
    )(sinks, table_flat, q, kn, vn, ck_all, cv_all, *extra)


def _even_sample_c_kernel(x_ref, sga_ref, attn_ref, mixb_ref, wout_ref, y_ref):
    mixa = (sga_ref[...] * attn_ref[...]).astype(BF16)
    mix = jnp.concatenate([mixa, mixb_ref[...]], axis=1)
    y_ref[...] = x_ref[...] + _dot(mix, wout_ref[...])


def _even_sample_c(x, sga, attn, mixb, wout_all, e):
    R, D = x.shape
    row = lambda w: pl.BlockSpec((BLK, w), lambda i: (i, 0))
    return pl.pallas_call(
        _even_sample_c_kernel,
        grid=(R // BLK,),
        in_specs=[row(D), row(SWA_WIDTH), row(SWA_WIDTH), row(GMLP_WIDTH), _layer_spec((EVEN_MIX, D), e)],
        out_specs=row(D),
        out_shape=jax.ShapeDtypeStruct((R, D), F32),
        compiler_params=pltpu.CompilerParams(
            dimension_semantics=("arbitrary",), vmem_limit_bytes=VMEM_LIMIT),
        name="even_sample_c",
    )(x, sga, attn, mixb, wout_all)


def _pair_split_perm():
    n = jnp.arange(RET_KEY_DIM)
    half = RET_KEY_DIM // 2
    src = jnp.where(n < half, 2 * n, 2 * (n - half) + 1)
    return (jnp.arange(RET_KEY_DIM)[:, None] == src[None, :]).astype(F32)


def _odd_sample_a_kernel(x_ref, g_ref, wqk_ref, wv_ref, wg_ref, cos8_ref, sin8_ref, split_ref,
                         q_ref, kdt_ref, v_ref, intra_ref, gate_ref, *, steps):
    x = x_ref[...]
    xn = _rmsnorm(x, g_ref[...]).astype(BF16)
    reps = BLK // cos8_ref.shape[0]
    cs = jnp.concatenate([cos8_ref[...]] * reps, axis=0)
    sn = jnp.concatenate([sin8_ref[...]] * reps, axis=0)
    rr = _iota((BLK, BLK), 0)
    cc = _iota((BLK, BLK), 1)
    mask = _same_batch_causal((BLK, BLK), steps=steps)
    diff = (rr % steps - cc % steps).astype(F32)
    step_col = (_iota((BLK, 1), 0) % steps).astype(F32)
    split = split_ref[...]
    zqk = _dot(xn, wqk_ref[...])
    zv = _dot(xn, wv_ref[...])
    gate_ref[...] = _silu(_dot(xn, wg_ref[...]))
    v_ref[...] = zv.astype(BF16)
    for h in range(RET_HEADS):
        lg = _ret_log_gamma(h)
        qcols = slice(h * RET_KEY_DIM, (h + 1) * RET_KEY_DIM)
        kcols = slice(RET_QK_WIDTH + h * RET_KEY_DIM, RET_QK_WIDTH + (h + 1) * RET_KEY_DIM)
        vcols = slice(h * RET_VALUE_DIM, (h + 1) * RET_VALUE_DIM)
        qr = _xpos_rotate_split(zqk[:, qcols], cs, sn)
        kr = _xpos_rotate_split(zqk[:, kcols], cs, sn) * (RET_KEY_DIM ** -0.5)
        qb = qr.astype(BF16)
        decay = jnp.where(mask, jnp.exp(lg * jnp.maximum(diff, 0.0)), 0.0)
        att = _dot_nt(qb, kr.astype(BF16)) * decay
        intra_ref[:, vcols] = _dot(att.astype(BF16), zv[:, vcols].astype(BF16))
        kd = (kr * jnp.exp(lg * (steps - 1.0 - step_col))).astype(BF16)
        q_ref[:, qcols] = _dot_nt(qb, split).astype(BF16)
        kdt_ref[h] = _dot_nt(split, kd).astype(BF16)


def _odd_sample_a(x, gains, layer, o, wqk_all, win_all, cos8, sin8, split, *, steps):
    R, D = x.shape
    row = lambda w: pl.BlockSpec((BLK, w), lambda i: (i, 0))
    kern = functools.partial(_odd_sample_a_kernel, steps=steps)
    once = dict(pipeline_mode=pl.Buffered(1))
    return pl.pallas_call(
        kern,
        grid=(R // BLK,),
        in_specs=[row(D), _layer_spec((1, D), layer),
                  _layer_spec((D, 2 * RET_QK_WIDTH), o, **once),
                  pl.BlockSpec((None, D, RET_V_WIDTH), lambda i: (o, 0, 1), **once),
                  pl.BlockSpec((None, D, RET_V_WIDTH), lambda i: (o, 0, 2), **once),
                  _const_spec(cos8.shape), _const_spec(sin8.shape), _const_spec(split.shape)],
        out_specs=[row(RET_QK_WIDTH),
                   pl.BlockSpec((RET_HEADS, RET_KEY_DIM, BLK), lambda i: (0, 0, i)),
                   row(RET_V_WIDTH), row(RET_V_WIDTH), row(RET_V_WIDTH)],
        out_shape=[
            jax.ShapeDtypeStruct((R, RET_QK_WIDTH), BF16),
            jax.ShapeDtypeStruct((RET_HEADS, RET_KEY_DIM, R), BF16),
            jax.ShapeDtypeStruct((R, RET_V_WIDTH), BF16),
            jax.ShapeDtypeStruct((R, RET_V_WIDTH), F32),
            jax.ShapeDtypeStruct((R, RET_V_WIDTH), F32),
        ],
        compiler_params=pltpu.CompilerParams(
            dimension_semantics=("arbitrary",), vmem_limit_bytes=VMEM_LIMIT),
        name="odd_sample_a",
    )(x, gains, wqk_all, win_all, win_all, cos8, sin8, split)


STATE_BB = 4
STATE_HB = 2
STATE_HEAD_GROUPS = RET_HEADS // STATE_HB


def _retention_state_block(blk, hg, q_ref, kdt_ref, v_ref, s_ref, cross_ref, snew_ref, *, steps):
    rows16 = STATE_BB * steps
    lane_base = (blk % (BLK // rows16)) * rows16
    lane = _iota((RET_KEY_DIM, BLK), 1)
    row16 = _iota((rows16, 1), 0)
    v = v_ref[...]
    for hl in range(STATE_HB):
        lg = jnp.float32(_ret_log_gamma(hl))
        for g in range(1, STATE_HEAD_GROUPS):
            lg = jnp.where(hg == g, jnp.float32(_ret_log_gamma(g * STATE_HB + hl)), lg)
        qdec = jnp.exp(lg * ((row16 % steps).astype(F32) + 1.0))
        cdec = jnp.exp(jnp.full((1, 1), lg * steps, F32))
        q = q_ref[:, hl * RET_KEY_DIM:(hl + 1) * RET_KEY_DIM]
        kdt = kdt_ref[hl]
        vh = v[:, hl * RET_VALUE_DIM:(hl + 1) * RET_VALUE_DIM]
        cross = jnp.zeros((rows16, RET_VALUE_DIM), F32)
        for bl in range(STATE_BB):
            st = s_ref[bl, hl]
            cr = _dot(q, st.astype(BF16))
            cross = jnp.where(row16 // steps == bl, cr, cross)
            mine = (lane - lane_base) // steps == bl
            upd = _dot(jnp.where(mine, kdt, jnp.zeros_like(kdt)), vh)
            snew_ref[bl, hl] = cdec * st + upd
        cross_ref[:, hl * RET_VALUE_DIM:(hl + 1) * RET_VALUE_DIM] = cross * qdec


def _odd_sample_c_kernel(x_ref, intra_ref, cross_ref, gate_ref, wout_ref, fg_ref, y_ref, *, final_norm):
    gated = []
    for h in range(RET_HEADS):
        hs = slice(h * RET_VALUE_DIM, (h + 1) * RET_VALUE_DIM)
        on = _layernorm_nogain(intra_ref[:, hs] + cross_ref[:, hs])
        gated.append((gate_ref[:, hs] * on).astype(BF16))
    y = x_ref[...] + _dot(jnp.concatenate(gated, axis=1), wout_ref[...])
    if final_norm:
        y = _rmsnorm(y, fg_ref[...])
    y_ref[...] = y


def _odd_sample_c(x, intra, cross, gate, wout_all, o, fg, *, final_norm):
    R, D = x.shape
    row = lambda w: pl.BlockSpec((BLK, w), lambda i: (i, 0))
    kern = functools.partial(_odd_sample_c_kernel, final_norm=final_norm)
    return pl.pallas_call(
        kern,
        grid=(R // BLK,),
        in_specs=[row(D), row(RET_V_WIDTH), row(RET_V_WIDTH), row(RET_V_WIDTH),
                  _layer_spec((RET_V_WIDTH, D), o), _const_spec((1, D))],
        out_specs=row(D),
        out_shape=jax.ShapeDtypeStruct((R, D), F32),
        compiler_params=pltpu.CompilerParams(
            dimension_semantics=("arbitrary",), vmem_limit_bytes=VMEM_LIMIT),
        name="odd_sample_c",
    )(x, intra, cross, gate, wout_all, fg)


def _xpos_tables(pos):
    angle = 1.0 / (10000.0 ** jnp.linspace(0.0, 1.0, RET_KEY_DIM // 2, dtype=F32))
    ang = pos.astype(F32)[:, None] * angle[None, :]
    return jnp.cos(ang), jnp.sin(ang)


def _split_pairs_qk(w_qk):
    x = w_qk.reshape(w_qk.shape[0], D_MODEL, 2 * RET_HEADS, RET_KEY_DIM).astype(F32)
    y = jnp.einsum('ldhc,cn->ldhn', x, _pair_split_perm())
    return y.astype(BF16).reshape(w_qk.shape)


PROMPT_TB_EVEN = 256
PROMPT_TB_ODD = 512


def kernel(x_prompt, x_sample, cache_swa_k, cache_swa_v, state_ret, norm_gain, final_norm_gain,
           rel_bias_table, even_w_in, even_w_out, swa_sinks, gmlp_ws, gmlp_bs, gmlp_ln_gain,
           odd_w_in, odd_w_out):
    B, S, D = x_prompt.shape
    DB, T, _ = x_sample.shape
    R = DB * T
    n_even = even_w_in.shape[0]
    tb = PROMPT_TB_ODD

    table_flat = rel_bias_table.reshape(-1)
    gains = norm_gain.reshape(DEPTH, 1, D)
    fg = final_norm_gain.reshape(1, D)
    even_win = even_w_in.astype(BF16)
    even_wout = even_w_out.astype(BF16)
    odd_win = odd_w_in.astype(BF16)
    odd_wout = odd_w_out.astype(BF16)
    odd_wqk = _split_pairs_qk(odd_win[:, :, :2 * RET_QK_WIDTH])
    bst_all = gmlp_bs[:, :, :, None]
    lng_all = gmlp_ln_gain.reshape(n_even, 1, GMLP_WIDTH)
    ws4 = gmlp_ws[:, :, :T, :T].reshape(-1)
    bs4 = gmlp_bs[:, :, :T].reshape(-1)

    prompt_tables = (_xpos_tables(jnp.arange(0, S, tb, dtype=jnp.int32))
                     + _xpos_tables(jnp.arange(tb, dtype=jnp.int32)))
    cos4, sin4 = _xpos_tables(PAST_LEN + jnp.arange(T, dtype=jnp.int32))
    cos8 = jnp.tile(cos4, (PAIR_ROWS // T, 1))
    sin8 = jnp.tile(sin4, (PAIR_ROWS // T, 1))
    split = _pair_split_perm().astype(BF16)

    yp = x_prompt
    ys = x_sample.reshape(R, D)
    ck_all = cache_swa_k.reshape(n_even, DB, WINDOW, SWA_KV_WIDTH)
    cv_all = cache_swa_v.reshape(n_even, DB, WINDOW, SWA_KV_WIDTH)
    new_caches = None
    new_state = None
    kp_l, vp_l, sp_l, gv_l = [], [], [], []
    assert DEPTH % 2 == 0
    for pair in range(DEPTH // 2):
        even_layer, odd_layer = 2 * pair, 2 * pair + 1
        last = odd_layer == DEPTH - 1
        q, kn, vn_, sga, mixb, gv = _even_sample_a(ys, gains, even_layer, pair, even_win, ws4, bs4, lng_all,
                                                   steps=T)
        attn, nk, nv = _even_sample_b(q, kn, vn_, ck_all, cv_all, pair, new_caches, swa_sinks, table_flat, steps=T)
        new_caches = (nk, nv)
        ys = _even_sample_c(ys, sga, attn, mixb, even_wout, pair)
        gv_l.append(gv.reshape(DB, T, GMLP_WIDTH))
        q, kdt, v, intra, gate = _odd_sample_a(ys, gains, odd_layer, pair, odd_wqk, odd_win, cos8, sin8, split,
                                               steps=T)
        yp, kp, vp, cross, new_state = _even_prompt(
            yp, gains, even_layer, pair, even_win, even_wout, swa_sinks, table_flat, gmlp_ws, bst_all, lng_all,
            (q, kdt, v, state_ret, pair, new_state), tb=PROMPT_TB_EVEN, steps=T)
        kp_l.append(kp.reshape(B, WINDOW, SWA_KV_HEADS, SWA_HEAD_DIM))
        vp_l.append(vp.reshape(B, WINDOW, SWA_KV_HEADS, SWA_HEAD_DIM))
        ys = _odd_sample_c(ys, intra, cross, gate, odd_wout, pair, fg, final_norm=last)
        yp, sp = _odd_prompt(yp, gains, odd_layer, pair, odd_wqk, odd_win, odd_wout, prompt_tables, fg,
                             tb=tb, final_norm=last)
        sp = sp.reshape(B, RET_HEADS, 2, RET_KEY_DIM // 2, RET_VALUE_DIM).swapaxes(2, 3).reshape(sp.shape)
        sp_l.append(sp)
    cache_shape = (n_even, DB, WINDOW, SWA_KV_HEADS, SWA_HEAD_DIM)
    return (yp, ys.reshape(DB, T, D), jnp.stack(kp_l), jnp.stack(vp_l),
            new_caches[0].reshape(cache_shape), new_caches[1].reshape(cache_shape),
            jnp.stack(sp_l), new_state, jnp.stack(gv_l))
```

```python
import functools
import math

import jax
import jax.numpy as jnp
from jax import lax
from jax.experimental import pallas as pl
from jax.experimental.pallas import tpu as pltpu

D_MODEL = 1024
DEPTH = 4
PAST_LEN = 8192
NORM_EPS = 1e-6
NEG_INF = -1e30

SWA_HEADS = 8
SWA_KV_HEADS = 2
SWA_HEAD_DIM = 64
SWA_GROUP = SWA_HEADS // SWA_KV_HEADS
SWA_WIDTH = SWA_HEADS * SWA_HEAD_DIM
SWA_KV_WIDTH = SWA_KV_HEADS * SWA_HEAD_DIM
WINDOW = 128
REL_BUCKETS = 32
REL_MAX_DIST = 128

GMLP_GROUPS = 4
GMLP_CHUNK = 128
GMLP_WIDTH = D_MODEL // 2
GMLP_GROUP_DIM = GMLP_WIDTH // GMLP_GROUPS

E_Q = 0
E_K = E_Q + SWA_WIDTH
E_V = E_K + SWA_KV_WIDTH
E_GA = E_V + SWA_KV_WIDTH
E_U = E_GA + SWA_WIDTH
E_VB = E_U + GMLP_WIDTH
E_GB = E_VB + GMLP_WIDTH
EVEN_IN = E_GB + GMLP_WIDTH
EVEN_MIX = SWA_WIDTH + GMLP_WIDTH

RET_HEADS = 4
RET_KEY_DIM = 256
RET_VALUE_DIM = 512
RET_QK_WIDTH = RET_HEADS * RET_KEY_DIM
RET_V_WIDTH = RET_HEADS * RET_VALUE_DIM
RET_CHUNK = 128
ODD_IN = 2 * RET_QK_WIDTH + 2 * RET_V_WIDTH

LANES = 128
BLK = 128
VMEM_LIMIT = 56 * 1024 * 1024

F32 = jnp.float32
BF16 = jnp.bfloat16


def _ret_log_gamma(h):
    return math.log(1.0 - 2.0 ** (-5.0 - h))


def _dot(a, b):
    return jnp.dot(a, b, preferred_element_type=F32)


def _dot_nt(a, b):
    return lax.dot_general(a, b, (((1,), (1,)), ((), ())), preferred_element_type=F32)


def _dot_tn(a, b):
    return lax.dot_general(a, b, (((0,), (0,)), ((), ())), preferred_element_type=F32)


def _silu(x):
    return x * (1.0 / (1.0 + jnp.exp(-x)))


def _rmsnorm(x, g):
    ms = jnp.mean(x * x, axis=-1, keepdims=True)
    return x * lax.rsqrt(ms + NORM_EPS) * g


def _layernorm_nogain(x):
    mu = jnp.mean(x, axis=-1, keepdims=True)
    d = x - mu
    var = jnp.mean(d * d, axis=-1, keepdims=True)
    return d * lax.rsqrt(var + NORM_EPS)


def _iota(shape, dim):
    return lax.broadcasted_iota(jnp.int32, shape, dim)


def _t5_bias(dist, table_ref, head):
    n = jnp.maximum(dist, 0)
    max_exact = REL_BUCKETS // 2
    nf = jnp.maximum(n, 1).astype(F32)
    large = max_exact + (jnp.log(nf / max_exact) / math.log(REL_MAX_DIST / max_exact)
                         * (REL_BUCKETS - max_exact)).astype(jnp.int32)
    large = jnp.minimum(large, REL_BUCKETS - 1)
    bucket = jnp.where(n < max_exact, n, large)
    acc = jnp.zeros(dist.shape, F32)
    for b in range(REL_BUCKETS):
        acc = jnp.where(bucket >= b, table_ref[b * SWA_HEADS + head], acc)
    return acc


def _split_heads_kv(x):
    lo = _iota(x.shape, 1) < SWA_HEAD_DIM
    xr = pltpu.roll(x, SWA_HEAD_DIM, 1)
    a0 = jnp.where(lo, x, 0.0).astype(BF16)
    b0 = jnp.where(lo, 0.0, xr).astype(BF16)
    a1 = jnp.where(lo, xr, 0.0).astype(BF16)
    b1 = jnp.where(lo, 0.0, x).astype(BF16)
    return (a0, b0), (a1, b1)


def _sink_softmax_parts(parts, sink):
    m = sink
    for t in parts:
        m = jnp.maximum(m, jnp.max(t, axis=-1, keepdims=True))
    es = [jnp.exp(t - m) for t in parts]
    den = jnp.exp(sink - m)
    for e in es:
        den = den + jnp.sum(e, axis=-1, keepdims=True)
    inv = 1.0 / den
    return [e * inv for e in es]


def _xpos_rotate(x, cos_f, sin_s):
    even = (_iota((x.shape[0], LANES), 1) % 2) == 0
    outs = []
    for c in range(x.shape[1] // LANES):
        sl = slice(c * LANES, (c + 1) * LANES)
        xs = x[:, sl]
        partner = jnp.where(even, pltpu.roll(xs, LANES - 1, 1), pltpu.roll(xs, 1, 1))
        outs.append(xs * cos_f[:, sl] + partner * sin_s[:, sl])
    return jnp.concatenate(outs, axis=1)


def _layer_spec(shape, layer, **kwargs):
    nd = len(shape)
    return pl.BlockSpec((None,) + tuple(shape), lambda *_: (layer,) + (0,) * nd, **kwargs)


def _const_spec(shape):
    nd = len(shape)
    return pl.BlockSpec(shape, lambda *_: (0,) * nd)


def _smem_spec():
    return pl.BlockSpec(memory_space=pltpu.SMEM)


EVEN_SUB_GROUP = 2

def _even_prompt_kernel(sinks_ref, table_ref, x_ref, g_ref, win_ref, wout_ref, ws_ref, bst_ref, lng_ref,
                        y_ref, newk_ref, newv_ref,
                        z_ref, qb_ref, vn_ref, mix_ref, ks_ref, vs_ref, bias_ref, wm_ref, *, tb, layer,
                        side_work=None):
    b = pl.program_id(0)
    j = pl.program_id(1)
    nsub = tb // BLK

    @pl.when(jnp.logical_and(b == 0, j == 0))
    def _init_tables():
        qi = _iota((BLK, BLK), 0)
        c = _iota((BLK, BLK), 1)
        own = c <= qi
        dist = jnp.where(own, qi - c, qi + BLK - c)
        for h in range(SWA_HEADS):
            bias = _t5_bias(dist, table_ref, h)
            bias_ref[0, h] = bias
            bias_ref[1, h] = jnp.where(own, bias, NEG_INF)
        causal = _iota((BLK, BLK), 0) >= _iota((BLK, BLK), 1)
        for g in range(GMLP_GROUPS):
            wm_ref[g] = jnp.where(causal, ws_ref[g], 0.0).astype(BF16)

    @pl.when(j == 0)
    def _zero_prev():
        zeros = jnp.zeros((BLK, LANES), BF16)
        for h in range(SWA_KV_HEADS):
            for ab in range(2):
                ks_ref[h, ab, 0:BLK, :] = zeros
                vs_ref[h, ab, 0:BLK, :] = zeros

    x = x_ref[0]
    xn = _rmsnorm(x, g_ref[...]).astype(BF16)
    z_ref[...] = _dot(xn, win_ref[...])
    qb_ref[...] = (z_ref[:, E_Q:E_Q + SWA_WIDTH] * (SWA_HEAD_DIM ** -0.5)).astype(BF16)
    z_ref[:, E_GA:E_GA + SWA_WIDTH] = _silu(z_ref[:, E_GA:E_GA + SWA_WIDTH])
    z_ref[:, E_GB:E_GB + GMLP_WIDTH] = _silu(z_ref[:, E_GB:E_GB + GMLP_WIDTH])
    vn_ref[...] = (_layernorm_nogain(z_ref[:, E_VB:E_VB + GMLP_WIDTH]) * lng_ref[...]).astype(BF16)
    kparts = _split_heads_kv(z_ref[:, E_K:E_K + SWA_KV_WIDTH])
    vparts = _split_heads_kv(z_ref[:, E_V:E_V + SWA_KV_WIDTH])
    for h in range(SWA_KV_HEADS):
        for ab in range(2):
            ks_ref[h, ab, BLK:BLK + tb, :] = kparts[h][ab]
            vs_ref[h, ab, BLK:BLK + tb, :] = vparts[h][ab]

    if side_work is not None:
        side_work()

    def sub_group(sg, carry):
        own = _iota((BLK, BLK), 1) <= _iota((BLK, BLK), 0)
        col0 = lambda h, p: h * SWA_GROUP * SWA_HEAD_DIM + p * LANES
        units = []
        for i in range(EVEN_SUB_GROUP):
            s = sg * EVEN_SUB_GROUP + i
            r0 = pl.multiple_of(s * BLK, BLK)
            first = jnp.where(j * nsub + s == 0, 1, 0)
            for h in range(SWA_KV_HEADS):
                for p in range(SWA_GROUP // 2):
                    units.append((pl.ds(r0, BLK), pl.ds(r0, 2 * BLK), first, h, p))
        stacked = lambda ref, h, pc: jnp.concatenate([ref[h, 0, pc, :], ref[h, 1, pc, :]], axis=0)
        logits, sinks = [], []
        for rows, prev_cur, first, h, p in units:
            c0 = col0(h, p)
            sc = _dot_nt(qb_ref[rows, c0:c0 + LANES], stacked(ks_ref, h, prev_cur))
            for gi in range(2):
                hd = h * SWA_GROUP + p * 2 + gi
                s_prev = sc[:, (2 * gi) * BLK:(2 * gi + 1) * BLK]
                s_cur = sc[:, (2 * gi + 1) * BLK:(2 * gi + 2) * BLK]
                logits.append(jnp.where(own, s_cur, s_prev) + bias_ref[first, hd])
                sinks.append(sinks_ref[layer, hd])
        maxes = [jnp.maximum(jnp.max(t, axis=-1, keepdims=True), sk) for t, sk in zip(logits, sinks)]
        exps = [jnp.exp(t - m) for t, m in zip(logits, maxes)]
        dens = [jnp.sum(e, axis=-1, keepdims=True) + jnp.exp(sk - m) for e, sk, m in zip(exps, sinks, maxes)]
        probs = [e * (1.0 / d) for e, d in zip(exps, dens)]
        for n, (rows, prev_cur, first, h, p) in enumerate(units):
            c0 = col0(h, p)
            parts = []
            for pr in probs[2 * n:2 * n + 2]:
                parts.append(jnp.where(own, 0.0, pr).astype(BF16))
                parts.append(jnp.where(own, pr, 0.0).astype(BF16))
            o = _dot(jnp.concatenate(parts, axis=1), stacked(vs_ref, h, prev_cur))
            mix_ref[rows, c0:c0 + LANES] = (z_ref[rows, E_GA + c0:E_GA + c0 + LANES] * o).astype(BF16)

        for i in range(EVEN_SUB_GROUP):
            rows = pl.ds(pl.multiple_of((sg * EVEN_SUB_GROUP + i) * BLK, BLK), BLK)
            for g in range(GMLP_GROUPS):
                gs = slice(g * GMLP_GROUP_DIM, (g + 1) * GMLP_GROUP_DIM)
                sp = _dot(wm_ref[g], vn_ref[rows, gs]) + bst_ref[g]
                u = z_ref[rows, E_U + g * GMLP_GROUP_DIM:E_U + (g + 1) * GMLP_GROUP_DIM]
                sgb = z_ref[rows, E_GB + g * GMLP_GROUP_DIM:E_GB + (g + 1) * GMLP_GROUP_DIM]
                mix_ref[rows, SWA_WIDTH + g * GMLP_GROUP_DIM:SWA_WIDTH + (g + 1) * GMLP_GROUP_DIM] = (
                    sgb * (u * sp)).astype(BF16)
        return carry

    lax.fori_loop(0, nsub // EVEN_SUB_GROUP, sub_group, 0)

    for h in range(SWA_KV_HEADS):
        for ab in range(2):
            ks_ref[h, ab, 0:BLK, :] = ks_ref[h, ab, tb:tb + BLK, :]
            vs_ref[h, ab, 0:BLK, :] = vs_ref[h, ab, tb:tb + BLK, :]

    y_ref[0] = x + _dot(mix_ref[...], wout_ref[...])

    @pl.when(j == pl.num_programs(1) - 1)
    def _emit_cache():
        newk_ref[0] = z_ref[tb - WINDOW:tb, E_K:E_K + SWA_KV_WIDTH]
        newv_ref[0] = z_ref[tb - WINDOW:tb, E_V:E_V + SWA_KV_WIDTH]


N_EVEN_PROMPT_INPUTS = 9
N_STATE_INPUTS = 4


def _even_prompt_state_kernel(*refs, tb, layer, steps, n_alias):
    prompt_in = refs[:N_EVEN_PROMPT_INPUTS]
    state_in = refs[N_EVEN_PROMPT_INPUTS:N_EVEN_PROMPT_INPUTS + N_STATE_INPUTS]
    rest = refs[N_EVEN_PROMPT_INPUTS + N_STATE_INPUTS + n_alias:]
    y_ref, newk_ref, newv_ref, cross_ref, snew_ref = rest[:5]
    lin = pl.program_id(0) * pl.num_programs(1) + pl.program_id(1)
    state_work = functools.partial(_retention_state_block, lin // STATE_HEAD_GROUPS, lin % STATE_HEAD_GROUPS,
                                   *state_in, cross_ref, snew_ref, steps=steps)
    _even_prompt_kernel(*prompt_in, y_ref, newk_ref, newv_ref, *rest[5:], tb=tb, layer=layer,
                        side_work=state_work)


def _even_prompt(x, gains, layer, e, win_all, wout_all, sinks, table_flat, ws_all, bst_all, lng_all,
                 state_job, *, tb, steps):
    B, S, D = x.shape
    nj = S // tb
    q, kdt, v, state_all, o, prev_state = state_job
    R = q.shape[0]
    DB = state_all.shape[1]
    assert B * nj == (DB // STATE_BB) * STATE_HEAD_GROUPS, "one retention-state block per prompt grid step"
    rows16 = STATE_BB * steps
    per_lane_blk = BLK // rows16
    sblk = lambda b, j: (b * nj + j) // STATE_HEAD_GROUPS
    shg = lambda b, j: (b * nj + j) % STATE_HEAD_GROUPS
    st_spec = pl.BlockSpec((None, STATE_BB, STATE_HB, RET_KEY_DIM, RET_VALUE_DIM),
                           lambda b, j: (o, sblk(b, j), shg(b, j), 0, 0))
    extra, extra_specs, aliases = _layer_slab_args(None if prev_state is None else (prev_state,), 4,
                                                   N_EVEN_PROMPT_INPUTS + N_STATE_INPUTS)
    once = dict(pipeline_mode=pl.Buffered(1))
    row_block = lambda: pl.BlockSpec((1, tb, D), lambda b, j: (b, j, 0))
    kern = functools.partial(_even_prompt_state_kernel, tb=tb, layer=e, steps=steps, n_alias=len(extra))
    return pl.pallas_call(
        kern,
        grid=(B, nj),
        in_specs=[
            _smem_spec(), _smem_spec(),
            row_block(),
            _layer_spec((1, D), layer),
            _layer_spec((D, EVEN_IN), e, **once),
            _layer_spec((EVEN_MIX, D), e, **once),
            _layer_spec((GMLP_GROUPS, GMLP_CHUNK, GMLP_CHUNK), e),
            _layer_spec((GMLP_GROUPS, GMLP_CHUNK, 1), e),
            _layer_spec((1, GMLP_WIDTH), e),
            pl.BlockSpec((rows16, STATE_HB * RET_KEY_DIM), lambda b, j: (sblk(b, j), shg(b, j))),
            pl.BlockSpec((STATE_HB, RET_KEY_DIM, BLK), lambda b, j: (shg(b, j), 0, sblk(b, j) // per_lane_blk)),
            pl.BlockSpec((BLK, STATE_HB * RET_VALUE_DIM), lambda b, j: (sblk(b, j) // per_lane_blk, shg(b, j))),
            st_spec,
        ] + extra_specs,
        out_specs=[
            row_block(),
            pl.BlockSpec((1, WINDOW, SWA_KV_WIDTH), lambda b, j: (b, 0, 0)),
            pl.BlockSpec((1, WINDOW, SWA_KV_WIDTH), lambda b, j: (b, 0, 0)),
            pl.BlockSpec((rows16, STATE_HB * RET_VALUE_DIM), lambda b, j: (sblk(b, j), shg(b, j))),
            st_spec,
        ],
        out_shape=[
            jax.ShapeDtypeStruct((B, S, D), F32),
            jax.ShapeDtypeStruct((B, WINDOW, SWA_KV_WIDTH), F32),
            jax.ShapeDtypeStruct((B, WINDOW, SWA_KV_WIDTH), F32),
            jax.ShapeDtypeStruct((R, RET_V_WIDTH), F32),
            jax.ShapeDtypeStruct(state_all.shape, F32),
        ],
        input_output_aliases=aliases,
        scratch_shapes=[
            pltpu.VMEM((tb, EVEN_IN), F32),
            pltpu.VMEM((tb, SWA_WIDTH), BF16),
            pltpu.VMEM((tb, GMLP_WIDTH), BF16),
            pltpu.VMEM((tb, EVEN_MIX), BF16),
            pltpu.VMEM((SWA_KV_HEADS, 2, BLK + tb, LANES), BF16),
            pltpu.VMEM((SWA_KV_HEADS, 2, BLK + tb, LANES), BF16),
            pltpu.VMEM((2, SWA_HEADS, BLK, BLK), F32),
            pltpu.VMEM((GMLP_GROUPS, GMLP_CHUNK, GMLP_CHUNK), BF16),
        ],
        compiler_params=pltpu.CompilerParams(
            dimension_semantics=("arbitrary", "arbitrary"), vmem_limit_bytes=VMEM_LIMIT),
        name="even_prompt",
    )(sinks, table_flat, x, gains, win_all, wout_all, ws_all, bst_all, lng_all, q, kdt, v, state_all, *extra)


RET_PROMPT_CHUNK = 256


def _odd_prompt_kernel(x_ref, g_ref, wqk_ref, wv_ref, wg_ref, wout_ref, cblk_ref, sblk_ref, crow_ref, srow_ref,
                       fg_ref, y_ref, sout_ref,
                       xn_ref, zqk_ref, qb_ref, kb_ref, kd_ref, vb_ref, sg_ref, s_ref, gated_ref, cos_ref, sin_ref,
                       decay_ref, qdec_ref, kdec_ref, *, tb, final_norm):
    b = pl.program_id(0)
    j = pl.program_id(1)
    ch = RET_PROMPT_CHUNK
    nchunk = tb // ch

    @pl.when(jnp.logical_and(b == 0, j == 0))
    def _init_tables():
        ii = _iota((ch, ch), 0)
        jj = _iota((ch, ch), 1)
        diff = (ii - jj).astype(F32)
        idx = _iota((ch, 1), 0).astype(F32)
        idx_blk = (_iota((tb, 1), 0) % ch).astype(F32)
        for h in range(RET_HEADS):
            lg = _ret_log_gamma(h)
            decay_ref[h] = jnp.where(diff >= 0, jnp.exp(lg * jnp.maximum(diff, 0.0)), 0.0)
            qdec_ref[h] = jnp.exp(lg * (idx + 1.0))
            kdec_ref[h] = jnp.exp(lg * (ch - 1.0 - idx_blk))

    @pl.when(j == 0)
    def _zero_state():
        s_ref[...] = jnp.zeros(s_ref.shape, F32)

    cb = cblk_ref[pl.ds(j, 1), :]
    sb = sblk_ref[pl.ds(j, 1), :]
    cos_ref[...] = cb * crow_ref[...] - sb * srow_ref[...]
    sin_ref[...] = sb * crow_ref[...] + cb * srow_ref[...]

    heads = range(RET_HEADS)
    xn_ref[...] = _rmsnorm(x_ref[0], g_ref[...]).astype(BF16)
    zqk_ref[...] = _dot(xn_ref[...], wqk_ref[...])
    vb_ref[...] = _dot(xn_ref[...], wv_ref[...]).astype(BF16)
    sg_ref[...] = _silu(_dot(xn_ref[...], wg_ref[...]))
    cs = cos_ref[...]
    sn = sin_ref[...]
    for h in heads:
        qcols = slice(h * RET_KEY_DIM, (h + 1) * RET_KEY_DIM)
        kcols = slice(RET_QK_WIDTH + h * RET_KEY_DIM, RET_QK_WIDTH + (h + 1) * RET_KEY_DIM)
        qb_ref[:, qcols] = _xpos_rotate(zqk_ref[:, qcols], cs, sn).astype(BF16)
        kr = _xpos_rotate(zqk_ref[:, kcols], cs, sn) * (RET_KEY_DIM ** -0.5)
        kb_ref[:, qcols] = kr.astype(BF16)
        kd_ref[:, qcols] = (kr * kdec_ref[h]).astype(BF16)

    def chunk(c, carry):
        rows = pl.ds(pl.multiple_of(c * ch, ch), ch)
        qcols = [slice(h * RET_KEY_DIM, (h + 1) * RET_KEY_DIM) for h in heads]
        vcols = [slice(h * RET_VALUE_DIM, (h + 1) * RET_VALUE_DIM) for h in heads]
        qb = [qb_ref[rows, qcols[h]] for h in heads]
        vb = [vb_ref[rows, vcols[h]] for h in heads]
        att = [(_dot_nt(qb[h], kb_ref[rows, qcols[h]]) * decay_ref[h]).astype(BF16) for h in heads]
        cross = [_dot(qb[h], s_ref[h].astype(BF16)) * qdec_ref[h] for h in heads]
        for h in heads:
            o_ref[rows, vcols[h]] = _dot(att[h], vb[h]) + cross[h]
        for h in heads:
            s_ref[h] = (math.exp(_ret_log_gamma(h) * ch) * s_ref[h]
                        + _dot_tn(kd_ref[rows, qcols[h]], vb[h]))
        return carry

    o_ref = zqk_ref
    lax.fori_loop(0, nchunk, chunk, 0)

    tiles = [(slice(c * BLK, (c + 1) * BLK), slice(h * RET_VALUE_DIM, (h + 1) * RET_VALUE_DIM))
             for c in range(tb // BLK) for h in heads]
    normed = [_layernorm_nogain(o_ref[r, v]) for r, v in tiles]
    for (r, v), on in zip(tiles, normed):
        gated_ref[r, v] = (sg_ref[r, v] * on).astype(BF16)

    y = x_ref[0] + _dot(gated_ref[...], wout_ref[...])
    if final_norm:
        y = _rmsnorm(y, fg_ref[...])
    y_ref[0] = y

    @pl.when(j == pl.num_programs(1) - 1)
    def _emit_state():
        sout_ref[0] = s_ref[...]


def _odd_prompt(x, gains, layer, o, win_all, wout_all, tables, fg, *, tb, final_norm):
    B, S, D = x.shape
    cblk, sblk, crow, srow = tables
    kern = functools.partial(_odd_prompt_kernel, tb=tb, final_norm=final_norm)
    once = dict(pipeline_mode=pl.Buffered(1))
    row_block = lambda: pl.BlockSpec((1, tb, D), lambda b, j: (b, j, 0))
    col_block = lambda c: pl.BlockSpec((None, D, RET_V_WIDTH), lambda b, j: (o, 0, c), **once)
    return pl.pallas_call(
        kern,
        grid=(B, S // tb),
        in_specs=[
            row_block(),
            _layer_spec((1, D), layer),
            col_block(0), col_block(1), col_block(2),
            _layer_spec((RET_V_WIDTH, D), o, **once),
            _const_spec((S // tb, RET_KEY_DIM)), _const_spec((S // tb, RET_KEY_DIM)),
            _const_spec((tb, RET_KEY_DIM)), _const_spec((tb, RET_KEY_DIM)),
            _const_spec((1, D)),
        ],
        out_specs=[
            row_block(),
            pl.BlockSpec((1, RET_HEADS, RET_KEY_DIM, RET_VALUE_DIM), lambda b, j: (b, 0, 0, 0)),
        ],
        out_shape=[
            jax.ShapeDtypeStruct((B, S, D), F32),
            jax.ShapeDtypeStruct((B, RET_HEADS, RET_KEY_DIM, RET_VALUE_DIM), F32),
        ],
        scratch_shapes=[
            pltpu.VMEM((tb, D), BF16),
            pltpu.VMEM((tb, 2 * RET_QK_WIDTH), F32),
            pltpu.VMEM((tb, RET_QK_WIDTH), BF16),
            pltpu.VMEM((tb, RET_QK_WIDTH), BF16),
            pltpu.VMEM((tb, RET_QK_WIDTH), BF16),
            pltpu.VMEM((tb, RET_V_WIDTH), BF16),
            pltpu.VMEM((tb, RET_V_WIDTH), F32),
            pltpu.VMEM((RET_HEADS, RET_KEY_DIM, RET_VALUE_DIM), F32),
            pltpu.VMEM((tb, RET_V_WIDTH), BF16),
            pltpu.VMEM((tb, RET_KEY_DIM), F32),
            pltpu.VMEM((tb, RET_KEY_DIM), F32),
            pltpu.VMEM((RET_HEADS, RET_PROMPT_CHUNK, RET_PROMPT_CHUNK), F32),
            pltpu.VMEM((RET_HEADS, RET_PROMPT_CHUNK, 1), F32),
            pltpu.VMEM((RET_HEADS, tb, 1), F32),
        ],
        compiler_params=pltpu.CompilerParams(
            dimension_semantics=("arbitrary", "arbitrary"), vmem_limit_bytes=VMEM_LIMIT),
        name="odd_prompt",
    )(x, gains, win_all, win_all, win_all, wout_all, cblk, sblk, crow, srow, fg)


def _same_batch_causal(shape, row0=0, col0=0, steps=4):
    r = _iota(shape, 0) + row0
    c = _iota(shape, 1) + col0
    return (r // steps == c // steps) & (c % steps <= r % steps)


def _even_sample_a_kernel(ws4_ref, bs4_ref, x_ref, g_ref, win_ref, lng_ref,
                          q_ref, k_ref, v_ref, sga_ref, mixb_ref, vn_ref, *, steps, layer):
    x = x_ref[...]
    xn = _rmsnorm(x, g_ref[...]).astype(BF16)
    z = _dot(xn, win_ref[...])
    q_ref[...] = z[:, E_Q:E_Q + SWA_WIDTH] * (SWA_HEAD_DIM ** -0.5)
    k_ref[...] = z[:, E_K:E_K + SWA_KV_WIDTH]
    v_ref[...] = z[:, E_V:E_V + SWA_KV_WIDTH]
    sga_ref[...] = _silu(z[:, E_GA:E_GA + SWA_WIDTH])
    vn = _layernorm_nogain(z[:, E_VB:E_VB + GMLP_WIDTH]) * lng_ref[...]
    vn_ref[...] = vn
    vnb = vn.astype(BF16)
    same = _iota((BLK, BLK), 0) // steps == _iota((BLK, BLK), 1) // steps
    rstep = _iota((BLK, BLK), 0) % steps
    cstep = _iota((BLK, BLK), 1) % steps
    rstep_col = _iota((BLK, 1), 0) % steps
    for g in range(GMLP_GROUPS):
        gs = slice(g * GMLP_GROUP_DIM, (g + 1) * GMLP_GROUP_DIM)
        wk = jnp.zeros((BLK, BLK), F32)
        bcol = jnp.zeros((BLK, 1), F32)
        for p in range(steps):
            bcol = jnp.where(rstep_col == p, bs4_ref[(layer * GMLP_GROUPS + g) * steps + p], bcol)
            for q in range(p + 1):
                w = ws4_ref[((layer * GMLP_GROUPS + g) * steps + p) * steps + q]
                wk = jnp.where(same & (rstep == p) & (cstep == q), w, wk)
        sp = _dot(wk.astype(BF16), vnb[:, gs]) + bcol
        u = z[:, E_U + g * GMLP_GROUP_DIM:E_U + (g + 1) * GMLP_GROUP_DIM]
        gb = z[:, E_GB + g * GMLP_GROUP_DIM:E_GB + (g + 1) * GMLP_GROUP_DIM]
        mixb_ref[:, gs] = (_silu(gb) * (u * sp)).astype(BF16)


def _even_sample_a(x, gains, layer, e, win_all, ws4, bs4, lng_all, *, steps):
    R, D = x.shape
    row = lambda w: pl.BlockSpec((BLK, w), lambda i: (i, 0))
    kern = functools.partial(_even_sample_a_kernel, steps=steps, layer=e)
    return pl.pallas_call(
        kern,
        grid=(R // BLK,),
        in_specs=[_smem_spec(), _smem_spec(), row(D), _layer_spec((1, D), layer), _layer_spec((D, EVEN_IN), e),
                  _layer_spec((1, GMLP_WIDTH), e)],
        out_specs=[row(SWA_WIDTH), row(SWA_KV_WIDTH), row(SWA_KV_WIDTH), row(SWA_WIDTH),
                   row(GMLP_WIDTH), row(GMLP_WIDTH)],
        out_shape=[
            jax.ShapeDtypeStruct((R, SWA_WIDTH), F32),
            jax.ShapeDtypeStruct((R, SWA_KV_WIDTH), F32),
            jax.ShapeDtypeStruct((R, SWA_KV_WIDTH), F32),
            jax.ShapeDtypeStruct((R, SWA_WIDTH), F32),
            jax.ShapeDtypeStruct((R, GMLP_WIDTH), BF16),
            jax.ShapeDtypeStruct((R, GMLP_WIDTH), F32),
        ],
        compiler_params=pltpu.CompilerParams(
            dimension_semantics=("arbitrary",), vmem_limit_bytes=VMEM_LIMIT),
        name="even_sample_a",
    )(ws4, bs4, x, gains, win_all, lng_all)


SAMPLE_BB = 32
PAIR_ROWS = 8
SAMPLE_PAIRS_PER_ITER = 2


def _even_sample_b_kernel(sinks_ref, table_ref, q_ref, kn_ref, vn_ref, ck_ref, cv_ref,
                          attn_ref, newk_ref, newv_ref,
                          sn_ref, kns_ref, vns_ref, biasc_ref, biasn_ref, *, steps, layer):
    kparts = _split_heads_kv(kn_ref[...])
    vparts = _split_heads_kv(vn_ref[...])
    for h in range(SWA_KV_HEADS):
        kns_ref[h] = jnp.concatenate(kparts[h], axis=0)
        vns_ref[h] = jnp.concatenate(vparts[h], axis=0)
        c0 = h * SWA_GROUP * SWA_HEAD_DIM
        lhs = jnp.concatenate([q_ref[:, c0:c0 + LANES], q_ref[:, c0 + LANES:c0 + 2 * LANES]], axis=0)
        sn_ref[h] = _dot_nt(lhs.astype(BF16), kns_ref[h])

    step_r = _iota((PAIR_ROWS, BLK), 0) % steps
    lane = _iota((PAIR_ROWS, BLK), 1)
    dist_c = step_r + WINDOW - lane
    valid_c = (dist_c >= 0) & (dist_c < WINDOW)
    dist_n = step_r - lane % steps
    first_local = _iota((PAIR_ROWS, 1), 0) < steps
    for hd in range(SWA_HEADS):
        biasc_ref[hd] = _t5_bias(dist_c, table_ref, hd)
        biasn_ref[hd] = _t5_bias(dist_n, table_ref, hd)

    first16 = jnp.concatenate([first_local, first_local], axis=0)

    def pair_group(it, carry):
        units = []
        for k in range(SAMPLE_PAIRS_PER_ITER):
            i = it * SAMPLE_PAIRS_PER_ITER + k
            r0 = pl.multiple_of(i * PAIR_ROWS, PAIR_ROWS)
            valid_n = _same_batch_causal((PAIR_ROWS, BLK), row0=r0, steps=steps)
            kc = [_split_heads_kv(ck_ref[2 * i + e]) for e in range(2)]
            vc = [_split_heads_kv(cv_ref[2 * i + e]) for e in range(2)]
            for h in range(SWA_KV_HEADS):
                units.append((r0, valid_n, h, [kc[e][h] for e in range(2)], [vc[e][h] for e in range(2)]))
            for e in range(2):
                be = 2 * i + e
                newk_ref[be, 0:WINDOW - steps, :] = ck_ref[be, steps:WINDOW, :]
                newv_ref[be, 0:WINDOW - steps, :] = cv_ref[be, steps:WINDOW, :]
                newk_ref[be, WINDOW - steps:WINDOW, :] = kn_ref[pl.ds(r0 + e * steps, steps), :]
                newv_ref[be, WINDOW - steps:WINDOW, :] = vn_ref[pl.ds(r0 + e * steps, steps), :]

        scores = []
        for r0, valid_n, h, kch, vch in units:
            rows = pl.ds(r0, PAIR_ROWS)
            c0 = h * SWA_GROUP * SWA_HEAD_DIM
            lhs = jnp.concatenate([q_ref[rows, c0:c0 + LANES], q_ref[rows, c0 + LANES:c0 + 2 * LANES]],
                                  axis=0).astype(BF16)
            sc_e = [_dot_nt(lhs, jnp.concatenate(kch[e], axis=0)) for e in range(2)]
            scores.append(jnp.where(first16, sc_e[0], sc_e[1]))

        tcs, tns, sks = [], [], []
        for (r0, valid_n, h, kch, vch), sc in zip(units, scores):
            for p in range(2):
                pr = slice(p * PAIR_ROWS, (p + 1) * PAIR_ROWS)
                for gi in range(2):
                    hd = h * SWA_GROUP + p * 2 + gi
                    gc = slice(gi * BLK, (gi + 1) * BLK)
                    tcs.append(jnp.where(valid_c, sc[pr, gc] + biasc_ref[hd], NEG_INF))
                    tn_raw = sn_ref[h, pl.ds(p * BLK + r0, PAIR_ROWS), gc]
                    tns.append(jnp.where(valid_n, tn_raw + biasn_ref[hd], NEG_INF))
                    sks.append(sinks_ref[layer, hd])
        ms = [jnp.maximum(jnp.maximum(jnp.max(tc, axis=-1, keepdims=True), jnp.max(tn, axis=-1, keepdims=True)), sk)
              for tc, tn, sk in zip(tcs, tns, sks)]
        ecs = [jnp.exp(tc - m) for tc, m in zip(tcs, ms)]
        ens = [jnp.exp(tn - m) for tn, m in zip(tns, ms)]
        invs = [1.0 / (jnp.exp(sk - m) + jnp.sum(ec, axis=-1, keepdims=True) + jnp.sum(en, axis=-1, keepdims=True))
                for sk, m, ec, en in zip(sks, ms, ecs, ens)]
        pcs = [ec * inv for ec, inv in zip(ecs, invs)]
        pns = [en * inv for en, inv in zip(ens, invs)]

        for n, (r0, valid_n, h, kch, vch) in enumerate(units):
            rows = pl.ds(r0, PAIR_ROWS)
            c0 = h * SWA_GROUP * SWA_HEAD_DIM
            grid16 = lambda ps: jnp.concatenate(
                [jnp.concatenate(ps[4 * n + 2 * p:4 * n + 2 * p + 2], axis=1) for p in range(2)], axis=0).astype(BF16)
            pc16 = grid16(pcs)
            pn16 = grid16(pns)
            o_e = [_dot(pc16, jnp.concatenate(vch[e], axis=0)) for e in range(2)]
            o = jnp.where(first16, o_e[0], o_e[1]) + _dot(pn16, vns_ref[h])
            attn_ref[rows, c0:c0 + LANES] = o[0:PAIR_ROWS]
            attn_ref[rows, c0 + LANES:c0 + 2 * LANES] = o[PAIR_ROWS:2 * PAIR_ROWS]
        return carry

    lax.fori_loop(0, SAMPLE_BB // (2 * SAMPLE_PAIRS_PER_ITER), pair_group, 0)


def _ignoring_refs(kernel_fn, start, count):
    if count == 0:
        return kernel_fn

    def wrapped(*refs):
        return kernel_fn(*refs[:start], *refs[start + count:])
    return wrapped


def _layer_slab_args(prev_outs, first_out_index, n_fixed_inputs):
    if prev_outs is None:
        return [], [], {}
    specs = [pl.BlockSpec(memory_space=pl.ANY) for _ in prev_outs]
    aliases = {n_fixed_inputs + k: first_out_index + k for k in range(len(prev_outs))}
    return list(prev_outs), specs, aliases


def _even_sample_b(q, kn, vn, ck_all, cv_all, layer, prev_outs, sinks, table_flat, *, steps):
    R = q.shape[0]
    row = lambda w: pl.BlockSpec((BLK, w), lambda i: (i, 0))
    cache = pl.BlockSpec((None, SAMPLE_BB, WINDOW, SWA_KV_WIDTH), lambda i: (layer, i, 0, 0))
    extra, extra_specs, aliases = _layer_slab_args(prev_outs, 1, 7)
    kern = _ignoring_refs(functools.partial(_even_sample_b_kernel, steps=steps, layer=layer), 7, len(extra))
    return pl.pallas_call(
        kern,
        grid=(R // BLK,),
        in_specs=[_smem_spec(), _smem_spec(), row(SWA_WIDTH), row(SWA_KV_WIDTH), row(SWA_KV_WIDTH), cache, cache]
        + extra_specs,
        out_specs=[row(SWA_WIDTH), cache, cache],
        out_shape=[
            jax.ShapeDtypeStruct((R, SWA_WIDTH), F32),
            jax.ShapeDtypeStruct(ck_all.shape, F32),
            jax.ShapeDtypeStruct(cv_all.shape, F32),
        ],
        input_output_aliases=aliases,
        scratch_shapes=[
            pltpu.VMEM((SWA_KV_HEADS, 2 * BLK, 2 * BLK), F32),
            pltpu.VMEM((SWA_KV_HEADS, 2 * BLK, LANES), BF16),
            pltpu.VMEM((SWA_KV_HEADS, 2 * BLK, LANES), BF16),
            pltpu.VMEM((SWA_HEADS, PAIR_ROWS, BLK), F32),
            pltpu.VMEM((SWA_HEADS, PAIR_ROWS, BLK), F32),
        ],
        compiler_params=pltpu.CompilerParams(
            dimension_semantics=("arbitrary",), vmem_limit_bytes=VMEM_LIMIT),
        name="even_sample_b",
    )(sinks, table_flat, q, kn, vn, ck_all, cv_all, *extra)


def _even_sample_c_kernel(x_ref, sga_ref, attn_ref, mixb_ref, wout_ref, y_ref):
    mixa = (sga_ref[...] * attn_ref[...]).astype(BF16)
    mix = jnp.concatenate([mixa, mixb_ref[...]], axis=1)
    y_ref[...] = x_ref[...] + _dot(mix, wout_ref[...])


def _even_sample_c(x, sga, attn, mixb, wout_all, e):
    R, D = x.shape
    row = lambda w: pl.BlockSpec((BLK, w), lambda i: (i, 0))
    return pl.pallas_call(
        _even_sample_c_kernel,
        grid=(R // BLK,),
        in_specs=[row(D), row(SWA_WIDTH), row(SWA_WIDTH), row(GMLP_WIDTH), _layer_spec((EVEN_MIX, D), e)],
        out_specs=row(D),
        out_shape=jax.ShapeDtypeStruct((R, D), F32),
        compiler_params=pltpu.CompilerParams(
            dimension_semantics=("arbitrary",), vmem_limit_bytes=VMEM_LIMIT),
        name="even_sample_c",
    )(x, sga, attn, mixb, wout_all)


def _odd_sample_a_kernel(x_ref, g_ref, wqk_ref, wv_ref, wg_ref, cos8_ref, sin8_ref,
                         q_ref, kdt_ref, v_ref, intra_ref, gate_ref, *, steps):
    x = x_ref[...]
    xn = _rmsnorm(x, g_ref[...]).astype(BF16)
    reps = BLK // cos8_ref.shape[0]
    cs = jnp.concatenate([cos8_ref[...]] * reps, axis=0)
    sn = jnp.concatenate([sin8_ref[...]] * reps, axis=0)
    rr = _iota((BLK, BLK), 0)
    cc = _iota((BLK, BLK), 1)
    mask = _same_batch_causal((BLK, BLK), steps=steps)
    diff = (rr % steps - cc % steps).astype(F32)
    step_col = (_iota((BLK, 1), 0) % steps).astype(F32)
    zqk = _dot(xn, wqk_ref[...])
    zv = _dot(xn, wv_ref[...])
    gate_ref[...] = _silu(_dot(xn, wg_ref[...]))
    v_ref[...] = zv.astype(BF16)
    for h in range(RET_HEADS):
        lg = _ret_log_gamma(h)
        qcols = slice(h * RET_KEY_DIM, (h + 1) * RET_KEY_DIM)
        kcols = slice(RET_QK_WIDTH + h * RET_KEY_DIM, RET_QK_WIDTH + (h + 1) * RET_KEY_DIM)
        vcols = slice(h * RET_VALUE_DIM, (h + 1) * RET_VALUE_DIM)
        qr = _xpos_rotate(zqk[:, qcols], cs, sn)
        kr = _xpos_rotate(zqk[:, kcols], cs, sn) * (RET_KEY_DIM ** -0.5)
        qb = qr.astype(BF16)
        decay = jnp.where(mask, jnp.exp(lg * jnp.maximum(diff, 0.0)), 0.0)
        att = _dot_nt(qb, kr.astype(BF16)) * decay
        intra_ref[:, vcols] = _dot(att.astype(BF16), zv[:, vcols].astype(BF16))
        kd = kr * jnp.exp(lg * (steps - 1.0 - step_col))
        q_ref[:, qcols] = qb
        kdt_ref[h] = kd.T.astype(BF16)


def _odd_sample_a(x, gains, layer, o, win_all, cos8, sin8, *, steps):
    R, D = x.shape
    row = lambda w: pl.BlockSpec((BLK, w), lambda i: (i, 0))
    kern = functools.partial(_odd_sample_a_kernel, steps=steps)
    once = dict(pipeline_mode=pl.Buffered(1))
    col_block = lambda c: pl.BlockSpec((None, D, RET_V_WIDTH), lambda i: (o, 0, c), **once)
    return pl.pallas_call(
        kern,
        grid=(R // BLK,),
        in_specs=[row(D), _layer_spec((1, D), layer),
                  col_block(0), col_block(1), col_block(2),
                  _const_spec(cos8.shape), _const_spec(sin8.shape)],
        out_specs=[row(RET_QK_WIDTH),
                   pl.BlockSpec((RET_HEADS, RET_KEY_DIM, BLK), lambda i: (0, 0, i)),
                   row(RET_V_WIDTH), row(RET_V_WIDTH), row(RET_V_WIDTH)],
        out_shape=[
            jax.ShapeDtypeStruct((R, RET_QK_WIDTH), BF16),
            jax.ShapeDtypeStruct((RET_HEADS, RET_KEY_DIM, R), BF16),
            jax.ShapeDtypeStruct((R, RET_V_WIDTH), BF16),
            jax.ShapeDtypeStruct((R, RET_V_WIDTH), F32),
            jax.ShapeDtypeStruct((R, RET_V_WIDTH), F32),
        ],
        compiler_params=pltpu.CompilerParams(
            dimension_semantics=("arbitrary",), vmem_limit_bytes=VMEM_LIMIT),
        name="odd_sample_a",
    )(x, gains, win_all, win_all, win_all, cos8, sin8)


STATE_BB = 4
STATE_HB = 2
STATE_HEAD_GROUPS = RET_HEADS // STATE_HB


def _retention_state_block(blk, hg, q_ref, kdt_ref, v_ref, s_ref, cross_ref, snew_ref, *, steps):
    rows16 = STATE_BB * steps
    lane_base = (blk % (BLK // rows16)) * rows16
    lane = _iota((RET_KEY_DIM, BLK), 1)
    row16 = _iota((rows16, 1), 0)
    v = v_ref[...]
    for hl in range(STATE_HB):
        lg = jnp.float32(_ret_log_gamma(hl))
        for g in range(1, STATE_HEAD_GROUPS):
            lg = jnp.where(hg == g, jnp.float32(_ret_log_gamma(g * STATE_HB + hl)), lg)
        qdec = jnp.exp(lg * ((row16 % steps).astype(F32) + 1.0))
        cdec = jnp.exp(jnp.full((1, 1), lg * steps, F32))
        q = q_ref[:, hl * RET_KEY_DIM:(hl + 1) * RET_KEY_DIM]
        kdt = kdt_ref[hl]
        vh = v[:, hl * RET_VALUE_DIM:(hl + 1) * RET_VALUE_DIM]
        cross = jnp.zeros((rows16, RET_VALUE_DIM), F32)
        for bl in range(STATE_BB):
            st = s_ref[bl, hl]
            cr = _dot(q, st.astype(BF16))
            cross = jnp.where(row16 // steps == bl, cr, cross)
            mine = (lane - lane_base) // steps == bl
            upd = _dot(jnp.where(mine, kdt, jnp.zeros_like(kdt)), vh)
            snew_ref[bl, hl] = cdec * st + upd
        cross_ref[:, hl * RET_VALUE_DIM:(hl + 1) * RET_VALUE_DIM] = cross * qdec


def _odd_sample_c_kernel(x_ref, intra_ref, cross_ref, gate_ref, wout_ref, fg_ref, y_ref, *, final_norm):
    gated = []
    for h in range(RET_HEADS):
        hs = slice(h * RET_VALUE_DIM, (h + 1) * RET_VALUE_DIM)
        on = _layernorm_nogain(intra_ref[:, hs] + cross_ref[:, hs])
        gated.append((gate_ref[:, hs] * on).astype(BF16))
    y = x_ref[...] + _dot(jnp.concatenate(gated, axis=1), wout_ref[...])
    if final_norm:
        y = _rmsnorm(y, fg_ref[...])
    y_ref[...] = y


def _odd_sample_c(x, intra, cross, gate, wout_all, o, fg, *, final_norm):
    R, D = x.shape
    row = lambda w: pl.BlockSpec((BLK, w), lambda i: (i, 0))
    kern = functools.partial(_odd_sample_c_kernel, final_norm=final_norm)
    return pl.pallas_call(
        kern,
        grid=(R // BLK,),
        in_specs=[row(D), row(RET_V_WIDTH), row(RET_V_WIDTH), row(RET_V_WIDTH),
                  _layer_spec((RET_V_WIDTH, D), o), _const_spec((1, D))],
        out_specs=row(D),
        out_shape=jax.ShapeDtypeStruct((R, D), F32),
        compiler_params=pltpu.CompilerParams(
            dimension_semantics=("arbitrary",), vmem_limit_bytes=VMEM_LIMIT),
        name="odd_sample_c",
    )(x, intra, cross, gate, wout_all, fg)


def _xpos_tables(pos):
    angle = 1.0 / (10000.0 ** jnp.linspace(0.0, 1.0, RET_KEY_DIM // 2, dtype=F32))
    ang = pos.astype(F32)[:, None] * angle[None, :]
    sin = jnp.sin(ang)
    cos_f = jnp.repeat(jnp.cos(ang), 2, axis=-1)
    sin_s = jnp.stack([-sin, sin], axis=-1).reshape(pos.shape[0], RET_KEY_DIM)
    return cos_f, sin_s


PROMPT_TB_EVEN = 256
PROMPT_TB_ODD = 512


def kernel(x_prompt, x_sample, cache_swa_k, cache_swa_v, state_ret, norm_gain, final_norm_gain,
           rel_bias_table, even_w_in, even_w_out, swa_sinks, gmlp_ws, gmlp_bs, gmlp_ln_gain,
           odd_w_in, odd_w_out):
    B, S, D = x_prompt.shape
    DB, T, _ = x_sample.shape
    R = DB * T
    n_even = even_w_in.shape[0]
    tb = PROMPT_TB_ODD

    table_flat = rel_bias_table.reshape(-1)
    gains = norm_gain.reshape(DEPTH, 1, D)
    fg = final_norm_gain.reshape(1, D)
    even_win = even_w_in.astype(BF16)
    even_wout = even_w_out.astype(BF16)
    odd_win = odd_w_in.astype(BF16)
    odd_wout = odd_w_out.astype(BF16)
    bst_all = gmlp_bs[:, :, :, None]
    lng_all = gmlp_ln_gain.reshape(n_even, 1, GMLP_WIDTH)
    ws4 = gmlp_ws[:, :, :T, :T].reshape(-1)
    bs4 = gmlp_bs[:, :, :T].reshape(-1)

    prompt_tables = (_xpos_tables(jnp.arange(0, S, tb, dtype=jnp.int32))
                     + _xpos_tables(jnp.arange(tb, dtype=jnp.int32)))
    cos4, sin4 = _xpos_tables(PAST_LEN + jnp.arange(T, dtype=jnp.int32))
    cos8 = jnp.tile(cos4, (PAIR_ROWS // T, 1))
    sin8 = jnp.tile(sin4, (PAIR_ROWS // T, 1))

    yp = x_prompt
    ys = x_sample.reshape(R, D)
    ck_all = cache_swa_k.reshape(n_even, DB, WINDOW, SWA_KV_WIDTH)
    cv_all = cache_swa_v.reshape(n_even, DB, WINDOW, SWA_KV_WIDTH)
    new_caches = None
    new_state = None
    kp_l, vp_l, sp_l, gv_l = [], [], [], []
    assert DEPTH % 2 == 0
    for pair in range(DEPTH // 2):
        even_layer, odd_layer = 2 * pair, 2 * pair + 1
        last = odd_layer == DEPTH - 1
        q, kn, vn_, sga, mixb, gv = _even_sample_a(ys, gains, even_layer, pair, even_win, ws4, bs4, lng_all,
                                                   steps=T)
        attn, nk, nv = _even_sample_b(q, kn, vn_, ck_all, cv_all, pair, new_caches, swa_sinks, table_flat, steps=T)
        new_caches = (nk, nv)
        ys = _even_sample_c(ys, sga, attn, mixb, even_wout, pair)
        gv_l.append(gv.reshape(DB, T, GMLP_WIDTH))
        q, kdt, v, intra, gate = _odd_sample_a(ys, gains, odd_layer, pair, odd_win, cos8, sin8, steps=T)
        yp, kp, vp, cross, new_state = _even_prompt(
            yp, gains, even_layer, pair, even_win, even_wout, swa_sinks, table_flat, gmlp_ws, bst_all, lng_all,
            (q, kdt, v, state_ret, pair, new_state), tb=PROMPT_TB_EVEN, steps=T)
        kp_l.append(kp.reshape(B, WINDOW, SWA_KV_HEADS, SWA_HEAD_DIM))
        vp_l.append(vp.reshape(B, WINDOW, SWA_KV_HEADS, SWA_HEAD_DIM))
        ys = _odd_sample_c(ys, intra, cross, gate, odd_wout, pair, fg, final_norm=last)
        yp, sp = _odd_prompt(yp, gains, odd_layer, pair, odd_win, odd_wout, prompt_tables, fg,
                             tb=tb, final_norm=last)
        sp_l.append(sp)
    cache_shape = (n_even, DB, WINDOW, SWA_KV_HEADS, SWA_HEAD_DIM)
    return (yp, ys.reshape(DB, T, D), jnp.stack(kp_l), jnp.stack(vp_l),
            new_caches[0].reshape(cache_shape), new_caches[1].reshape(cache_shape),
            jnp.stack(sp_l), new_state, jnp.stack(gv_l))
```

```python
import functools
import math

import jax
import jax.numpy as jnp
from jax import lax
from jax.experimental import pallas as pl
from jax.experimental.pallas import tpu as pltpu

D_MODEL = 1024
DEPTH = 4
PAST_LEN = 8192
NORM_EPS = 1e-6
NEG_INF = -1e30

SWA_HEADS = 8
SWA_KV_HEADS = 2
SWA_HEAD_DIM = 64
SWA_GROUP = SWA_HEADS // SWA_KV_HEADS
SWA_WIDTH = SWA_HEADS * SWA_HEAD_DIM
SWA_KV_WIDTH = SWA_KV_HEADS * SWA_HEAD_DIM
WINDOW = 128
REL_BUCKETS = 32
REL_MAX_DIST = 128

GMLP_GROUPS = 4
GMLP_CHUNK = 128
GMLP_WIDTH = D_MODEL // 2
GMLP_GROUP_DIM = GMLP_WIDTH // GMLP_GROUPS

E_Q = 0
E_K = E_Q + SWA_WIDTH
E_V = E_K + SWA_KV_WIDTH
E_GA = E_V + SWA_KV_WIDTH
E_U = E_GA + SWA_WIDTH
E_VB = E_U + GMLP_WIDTH
E_GB = E_VB + GMLP_WIDTH
EVEN_IN = E_GB + GMLP_WIDTH
EVEN_MIX = SWA_WIDTH + GMLP_WIDTH

RET_HEADS = 4
RET_KEY_DIM = 256
RET_VALUE_DIM = 512
RET_QK_WIDTH = RET_HEADS * RET_KEY_DIM
RET_V_WIDTH = RET_HEADS * RET_VALUE_DIM
RET_CHUNK = 128
ODD_IN = 2 * RET_QK_WIDTH + 2 * RET_V_WIDTH

LANES = 128
BLK = 128
VMEM_LIMIT = 56 * 1024 * 1024

F32 = jnp.float32
BF16 = jnp.bfloat16


def _ret_log_gamma(h):
    return math.log(1.0 - 2.0 ** (-5.0 - h))


def _dot(a, b):
    return jnp.dot(a, b, preferred_element_type=F32)


def _dot_nt(a, b):
    return lax.dot_general(a, b, (((1,), (1,)), ((), ())), preferred_element_type=F32)


def _dot_tn(a, b):
    return lax.dot_general(a, b, (((0,), (0,)), ((), ())), preferred_element_type=F32)


def _silu(x):
    return x * (1.0 / (1.0 + jnp.exp(-x)))


def _rmsnorm(x, g):
    ms = jnp.mean(x * x, axis=-1, keepdims=True)
    return x * lax.rsqrt(ms + NORM_EPS) * g


def _layernorm_nogain(x):
    mu = jnp.mean(x, axis=-1, keepdims=True)
    d = x - mu
    var = jnp.mean(d * d, axis=-1, keepdims=True)
    return d * lax.rsqrt(var + NORM_EPS)


def _iota(shape, dim):
    return lax.broadcasted_iota(jnp.int32, shape, dim)


def _t5_bias(dist, table_ref, head):
    n = jnp.maximum(dist, 0)
    max_exact = REL_BUCKETS // 2
    nf = jnp.maximum(n, 1).astype(F32)
    large = max_exact + (jnp.log(nf / max_exact) / math.log(REL_MAX_DIST / max_exact)
                         * (REL_BUCKETS - max_exact)).astype(jnp.int32)
    large = jnp.minimum(large, REL_BUCKETS - 1)
    bucket = jnp.where(n < max_exact, n, large)
    acc = jnp.zeros(dist.shape, F32)
    for b in range(REL_BUCKETS):
        acc = jnp.where(bucket >= b, table_ref[b * SWA_HEADS + head], acc)
    return acc


def _split_heads_kv(x):
    lo = _iota(x.shape, 1) < SWA_HEAD_DIM
    xr = pltpu.roll(x, SWA_HEAD_DIM, 1)
    a0 = jnp.where(lo, x, 0.0).astype(BF16)
    b0 = jnp.where(lo, 0.0, xr).astype(BF16)
    a1 = jnp.where(lo, xr, 0.0).astype(BF16)
    b1 = jnp.where(lo, 0.0, x).astype(BF16)
    return (a0, b0), (a1, b1)


def _sink_softmax_parts(parts, sink):
    m = sink
    for t in parts:
        m = jnp.maximum(m, jnp.max(t, axis=-1, keepdims=True))
    es = [jnp.exp(t - m) for t in parts]
    den = jnp.exp(sink - m)
    for e in es:
        den = den + jnp.sum(e, axis=-1, keepdims=True)
    inv = 1.0 / den
    return [e * inv for e in es]


def _xpos_rotate(x, cos_f, sin_s):
    even = (_iota((x.shape[0], LANES), 1) % 2) == 0
    outs = []
    for c in range(x.shape[1] // LANES):
        sl = slice(c * LANES, (c + 1) * LANES)
        xs = x[:, sl]
        partner = jnp.where(even, pltpu.roll(xs, LANES - 1, 1), pltpu.roll(xs, 1, 1))
        outs.append(xs * cos_f[:, sl] + partner * sin_s[:, sl])
    return jnp.concatenate(outs, axis=1)


def _layer_spec(shape, layer, **kwargs):
    nd = len(shape)
    return pl.BlockSpec((None,) + tuple(shape), lambda *_: (layer,) + (0,) * nd, **kwargs)


def _const_spec(shape):
    nd = len(shape)
    return pl.BlockSpec(shape, lambda *_: (0,) * nd)


def _smem_spec():
    return pl.BlockSpec(memory_space=pltpu.SMEM)


def _even_prompt_kernel(sinks_ref, table_ref, x_ref, g_ref, win_ref, wout_ref, ws_ref, bst_ref, lng_ref,
                        y_ref, newk_ref, newv_ref,
                        z_ref, zga_ref, zvb_ref, qb_ref, vn_ref, mix_ref, ks_ref, vs_ref, bias_ref, wm_ref,
                        *, tb, layer, side_work=None):
    b = pl.program_id(0)
    j = pl.program_id(1)
    nsub = tb // BLK

    @pl.when(jnp.logical_and(b == 0, j == 0))
    def _init_tables():
        qi = _iota((BLK, BLK), 0)
        c = _iota((BLK, BLK), 1)
        own = c <= qi
        dist = jnp.where(own, qi - c, qi + BLK - c)
        for h in range(SWA_HEADS):
            bias = _t5_bias(dist, table_ref, h)
            bias_ref[0, h] = bias
            bias_ref[1, h] = jnp.where(own, bias, NEG_INF)
        causal = _iota((BLK, BLK), 0) >= _iota((BLK, BLK), 1)
        for g in range(GMLP_GROUPS):
            wm_ref[g] = jnp.where(causal, ws_ref[g], 0.0).astype(BF16)

    @pl.when(j == 0)
    def _zero_prev():
        zeros = jnp.zeros((BLK, LANES), BF16)
        for h in range(SWA_KV_HEADS):
            for ab in range(2):
                ks_ref[h, ab, 0:BLK, :] = zeros
                vs_ref[h, ab, 0:BLK, :] = zeros

    x = x_ref[0]
    xn = _rmsnorm(x, g_ref[...]).astype(BF16)
    z_ref[...] = _dot(xn, win_ref[:, 0:E_GA])
    zga_ref[...] = _dot(xn, win_ref[:, E_GA:E_VB])
    qb_ref[...] = (z_ref[:, E_Q:E_Q + SWA_WIDTH] * (SWA_HEAD_DIM ** -0.5)).astype(BF16)
    kparts = _split_heads_kv(z_ref[:, E_K:E_K + SWA_KV_WIDTH])
    vparts = _split_heads_kv(z_ref[:, E_V:E_V + SWA_KV_WIDTH])
    for h in range(SWA_KV_HEADS):
        for ab in range(2):
            ks_ref[h, ab, BLK:BLK + tb, :] = kparts[h][ab]
            vs_ref[h, ab, BLK:BLK + tb, :] = vparts[h][ab]

    own = _iota((BLK, BLK), 1) <= _iota((BLK, BLK), 0)
    col0 = lambda h, p: h * SWA_GROUP * SWA_HEAD_DIM + p * LANES
    sub_rows = [slice(s * BLK, (s + 1) * BLK) for s in range(nsub)]
    units = []
    for s in range(nsub):
        first = jnp.where(j * nsub + s == 0, 1, 0)
        for h in range(SWA_KV_HEADS):
            for p in range(SWA_GROUP // 2):
                units.append((sub_rows[s], slice(s * BLK, (s + 2) * BLK), first, h, p))
    stacked = lambda ref, h, pc: jnp.concatenate([ref[h, 0, pc, :], ref[h, 1, pc, :]], axis=0)
    logits, sinks = [], []
    for rows, prev_cur, first, h, p in units:
        c0 = col0(h, p)
        sc = _dot_nt(qb_ref[rows, c0:c0 + LANES], stacked(ks_ref, h, prev_cur))
        for gi in range(2):
            hd = h * SWA_GROUP + p * 2 + gi
            s_prev = sc[:, (2 * gi) * BLK:(2 * gi + 1) * BLK]
            s_cur = sc[:, (2 * gi + 1) * BLK:(2 * gi + 2) * BLK]
            logits.append(jnp.where(own, s_cur, s_prev) + bias_ref[first, hd])
            sinks.append(sinks_ref[layer, hd])

    zvb_ref[...] = _dot(xn, win_ref[:, E_VB:EVEN_IN])

    maxes = [jnp.maximum(jnp.max(t, axis=-1, keepdims=True), sk) for t, sk in zip(logits, sinks)]
    exps = [jnp.exp(t - m) for t, m in zip(logits, maxes)]
    dens = [jnp.sum(e, axis=-1, keepdims=True) + jnp.exp(sk - m) for e, sk, m in zip(exps, sinks, maxes)]
    probs = [e * (1.0 / d) for e, d in zip(exps, dens)]

    if side_work is not None:
        side_work()

    zga_ref[:, 0:SWA_WIDTH] = _silu(zga_ref[:, 0:SWA_WIDTH])
    zvb_ref[:, GMLP_WIDTH:2 * GMLP_WIDTH] = _silu(zvb_ref[:, GMLP_WIDTH:2 * GMLP_WIDTH])
    vn_ref[...] = (_layernorm_nogain(zvb_ref[:, 0:GMLP_WIDTH]) * lng_ref[...]).astype(BF16)

    for n, (rows, prev_cur, first, h, p) in enumerate(units):
        c0 = col0(h, p)
        parts = []
        for pr in probs[2 * n:2 * n + 2]:
            parts.append(jnp.where(own, 0.0, pr).astype(BF16))
            parts.append(jnp.where(own, pr, 0.0).astype(BF16))
        o = _dot(jnp.concatenate(parts, axis=1), stacked(vs_ref, h, prev_cur))
        mix_ref[rows, c0:c0 + LANES] = (zga_ref[rows, c0:c0 + LANES] * o).astype(BF16)

    for rows in sub_rows:
        for g in range(GMLP_GROUPS):
            gs = slice(g * GMLP_GROUP_DIM, (g + 1) * GMLP_GROUP_DIM)
            sp = _dot(wm_ref[g], vn_ref[rows, gs]) + bst_ref[g]
            u = zga_ref[rows, SWA_WIDTH + g * GMLP_GROUP_DIM:SWA_WIDTH + (g + 1) * GMLP_GROUP_DIM]
            sgb = zvb_ref[rows, GMLP_WIDTH + g * GMLP_GROUP_DIM:GMLP_WIDTH + (g + 1) * GMLP_GROUP_DIM]
            mix_ref[rows, SWA_WIDTH + g * GMLP_GROUP_DIM:SWA_WIDTH + (g + 1) * GMLP_GROUP_DIM] = (
                sgb * (u * sp)).astype(BF16)

    for h in range(SWA_KV_HEADS):
        for ab in range(2):
            ks_ref[h, ab, 0:BLK, :] = ks_ref[h, ab, tb:tb + BLK, :]
            vs_ref[h, ab, 0:BLK, :] = vs_ref[h, ab, tb:tb + BLK, :]

    y_ref[0] = x + _dot(mix_ref[...], wout_ref[...])

    @pl.when(j == pl.num_programs(1) - 1)
    def _emit_cache():
        newk_ref[0] = z_ref[tb - WINDOW:tb, E_K:E_K + SWA_KV_WIDTH]
        newv_ref[0] = z_ref[tb - WINDOW:tb, E_V:E_V + SWA_KV_WIDTH]


N_EVEN_PROMPT_INPUTS = 9
N_STATE_INPUTS = 4


def _even_prompt_state_kernel(*refs, tb, layer, steps, n_alias):
    prompt_in = refs[:N_EVEN_PROMPT_INPUTS]
    state_in = refs[N_EVEN_PROMPT_INPUTS:N_EVEN_PROMPT_INPUTS + N_STATE_INPUTS]
    rest = refs[N_EVEN_PROMPT_INPUTS + N_STATE_INPUTS + n_alias:]
    y_ref, newk_ref, newv_ref, cross_ref, snew_ref = rest[:5]
    lin = pl.program_id(0) * pl.num_programs(1) + pl.program_id(1)
    state_work = functools.partial(_retention_state_block, lin // STATE_HEAD_GROUPS, lin % STATE_HEAD_GROUPS,
                                   *state_in, cross_ref, snew_ref, steps=steps)
    _even_prompt_kernel(*prompt_in, y_ref, newk_ref, newv_ref, *rest[5:], tb=tb, layer=layer,
                        side_work=state_work)


def _even_prompt(x, gains, layer, e, win_all, wout_all, sinks, table_flat, ws_all, bst_all, lng_all,
                 state_job, *, tb, steps):
    B, S, D = x.shape
    nj = S // tb
    q, kdt, v, state_all, o, prev_state = state_job
    R = q.shape[0]
    DB = state_all.shape[1]
    assert B * nj == (DB // STATE_BB) * STATE_HEAD_GROUPS, "one retention-state block per prompt grid step"
    rows16 = STATE_BB * steps
    per_lane_blk = BLK // rows16
    sblk = lambda b, j: (b * nj + j) // STATE_HEAD_GROUPS
    shg = lambda b, j: (b * nj + j) % STATE_HEAD_GROUPS
    st_spec = pl.BlockSpec((None, STATE_BB, STATE_HB, RET_KEY_DIM, RET_VALUE_DIM),
                           lambda b, j: (o, sblk(b, j), shg(b, j), 0, 0))
    extra, extra_specs, aliases = _layer_slab_args(None if prev_state is None else (prev_state,), 4,
                                                   N_EVEN_PROMPT_INPUTS + N_STATE_INPUTS)
    once = dict(pipeline_mode=pl.Buffered(1))
    row_block = lambda: pl.BlockSpec((1, tb, D), lambda b, j: (b, j, 0))
    kern = functools.partial(_even_prompt_state_kernel, tb=tb, layer=e, steps=steps, n_alias=len(extra))
    return pl.pallas_call(
        kern,
        grid=(B, nj),
        in_specs=[
            _smem_spec(), _smem_spec(),
            row_block(),
            _layer_spec((1, D), layer),
            _layer_spec((D, EVEN_IN), e, **once),
            _layer_spec((EVEN_MIX, D), e, **once),
            _layer_spec((GMLP_GROUPS, GMLP_CHUNK, GMLP_CHUNK), e),
            _layer_spec((GMLP_GROUPS, GMLP_CHUNK, 1), e),
            _layer_spec((1, GMLP_WIDTH), e),
            pl.BlockSpec((rows16, STATE_HB * RET_KEY_DIM), lambda b, j: (sblk(b, j), shg(b, j))),
            pl.BlockSpec((STATE_HB, RET_KEY_DIM, BLK), lambda b, j: (shg(b, j), 0, sblk(b, j) // per_lane_blk)),
            pl.BlockSpec((BLK, STATE_HB * RET_VALUE_DIM), lambda b, j: (sblk(b, j) // per_lane_blk, shg(b, j))),
            st_spec,
        ] + extra_specs,
        out_specs=[
            row_block(),
            pl.BlockSpec((1, WINDOW, SWA_KV_WIDTH), lambda b, j: (b, 0, 0)),
            pl.BlockSpec((1, WINDOW, SWA_KV_WIDTH), lambda b, j: (b, 0, 0)),
            pl.BlockSpec((rows16, STATE_HB * RET_VALUE_DIM), lambda b, j: (sblk(b, j), shg(b, j))),
            st_spec,
        ],
        out_shape=[
            jax.ShapeDtypeStruct((B, S, D), F32),
            jax.ShapeDtypeStruct((B, WINDOW, SWA_KV_WIDTH), F32),
            jax.ShapeDtypeStruct((B, WINDOW, SWA_KV_WIDTH), F32),
            jax.ShapeDtypeStruct((R, RET_V_WIDTH), F32),
            jax.ShapeDtypeStruct(state_all.shape, F32),
        ],
        input_output_aliases=aliases,
        scratch_shapes=[
            pltpu.VMEM((tb, E_GA), F32),
            pltpu.VMEM((tb, E_VB - E_GA), F32),
            pltpu.VMEM((tb, EVEN_IN - E_VB), F32),
            pltpu.VMEM((tb, SWA_WIDTH), BF16),
            pltpu.VMEM((tb, GMLP_WIDTH), BF16),
            pltpu.VMEM((tb, EVEN_MIX), BF16),
            pltpu.VMEM((SWA_KV_HEADS, 2, BLK + tb, LANES), BF16),
            pltpu.VMEM((SWA_KV_HEADS, 2, BLK + tb, LANES), BF16),
            pltpu.VMEM((2, SWA_HEADS, BLK, BLK), F32),
            pltpu.VMEM((GMLP_GROUPS, GMLP_CHUNK, GMLP_CHUNK), BF16),
        ],
        compiler_params=pltpu.CompilerParams(
            dimension_semantics=("arbitrary", "arbitrary"), vmem_limit_bytes=VMEM_LIMIT),
        name="even_prompt",
    )(sinks, table_flat, x, gains, win_all, wout_all, ws_all, bst_all, lng_all, q, kdt, v, state_all, *extra)


RET_PROMPT_CHUNK = 256


def _odd_prompt_kernel(x_ref, g_ref, wqk_ref, wv_ref, wg_ref, wout_ref, cblk_ref, sblk_ref, crow_ref, srow_ref,
                       fg_ref, y_ref, sout_ref,
                       xn_ref, zqk_ref, qb_ref, kb_ref, kd_ref, vb_ref, sg_ref, s_ref, gated_ref, cos_ref, sin_ref,
                       decay_ref, qdec_ref, kdec_ref, *, tb, final_norm):
    b = pl.program_id(0)
    j = pl.program_id(1)
    ch = RET_PROMPT_CHUNK
    nchunk = tb // ch

    @pl.when(jnp.logical_and(b == 0, j == 0))
    def _init_tables():
        ii = _iota((ch, ch), 0)
        jj = _iota((ch, ch), 1)
        diff = (ii - jj).astype(F32)
        idx = _iota((ch, 1), 0).astype(F32)
        idx_blk = (_iota((tb, 1), 0) % ch).astype(F32)
        for h in range(RET_HEADS):
            lg = _ret_log_gamma(h)
            decay_ref[h] = jnp.where(diff >= 0, jnp.exp(lg * jnp.maximum(diff, 0.0)), 0.0)
            qdec_ref[h] = jnp.exp(lg * (idx + 1.0))
            kdec_ref[h] = jnp.exp(lg * (ch - 1.0 - idx_blk))

    @pl.when(j == 0)
    def _zero_state():
        s_ref[...] = jnp.zeros(s_ref.shape, F32)

    cb = cblk_ref[pl.ds(j, 1), :]
    sb = sblk_ref[pl.ds(j, 1), :]
    cos_ref[...] = cb * crow_ref[...] - sb * srow_ref[...]
    sin_ref[...] = sb * crow_ref[...] + cb * srow_ref[...]

    heads = range(RET_HEADS)
    xn_ref[...] = _rmsnorm(x_ref[0], g_ref[...]).astype(BF16)
    zqk_ref[...] = _dot(xn_ref[...], wqk_ref[...])
    vb_ref[...] = _dot(xn_ref[...], wv_ref[...]).astype(BF16)
    sg_ref[...] = _silu(_dot(xn_ref[...], wg_ref[...]))
    cs = cos_ref[...]
    sn = sin_ref[...]
    for h in heads:
        qcols = slice(h * RET_KEY_DIM, (h + 1) * RET_KEY_DIM)
        kcols = slice(RET_QK_WIDTH + h * RET_KEY_DIM, RET_QK_WIDTH + (h + 1) * RET_KEY_DIM)
        qb_ref[:, qcols] = _xpos_rotate(zqk_ref[:, qcols], cs, sn).astype(BF16)
        kr = _xpos_rotate(zqk_ref[:, kcols], cs, sn) * (RET_KEY_DIM ** -0.5)
        kb_ref[:, qcols] = kr.astype(BF16)
        kd_ref[:, qcols] = (kr * kdec_ref[h]).astype(BF16)

    def chunk(c, carry):
        rows = pl.ds(pl.multiple_of(c * ch, ch), ch)
        qcols = [slice(h * RET_KEY_DIM, (h + 1) * RET_KEY_DIM) for h in heads]
        vcols = [slice(h * RET_VALUE_DIM, (h + 1) * RET_VALUE_DIM) for h in heads]
        qb = [qb_ref[rows, qcols[h]] for h in heads]
        vb = [vb_ref[rows, vcols[h]] for h in heads]
        att = [(_dot_nt(qb[h], kb_ref[rows, qcols[h]]) * decay_ref[h]).astype(BF16) for h in heads]
        cross = [_dot(qb[h], s_ref[h].astype(BF16)) * qdec_ref[h] for h in heads]
        for h in heads:
            o_ref[rows, vcols[h]] = _dot(att[h], vb[h]) + cross[h]
        for h in heads:
            s_ref[h] = (math.exp(_ret_log_gamma(h) * ch) * s_ref[h]
                        + _dot_tn(kd_ref[rows, qcols[h]], vb[h]))
        return carry

    o_ref = zqk_ref
    lax.fori_loop(0, nchunk, chunk, 0)

    y = x_ref[0]
    for h in heads:
        v = slice(h * RET_VALUE_DIM, (h + 1) * RET_VALUE_DIM)
        tiles = [slice(c * BLK, (c + 1) * BLK) for c in range(tb // BLK)]
        normed = [_layernorm_nogain(o_ref[r, v]) for r in tiles]
        for r, on in zip(tiles, normed):
            gated_ref[r, v] = (sg_ref[r, v] * on).astype(BF16)
        y = y + _dot(gated_ref[:, v], wout_ref[v, :])
    if final_norm:
        y = _rmsnorm(y, fg_ref[...])
    y_ref[0] = y

    @pl.when(j == pl.num_programs(1) - 1)
    def _emit_state():
        sout_ref[0] = s_ref[...]


def _odd_prompt(x, gains, layer, o, win_all, wout_all, tables, fg, *, tb, final_norm):
    B, S, D = x.shape
    cblk, sblk, crow, srow = tables
    kern = functools.partial(_odd_prompt_kernel, tb=tb, final_norm=final_norm)
    once = dict(pipeline_mode=pl.Buffered(1))
    row_block = lambda: pl.BlockSpec((1, tb, D), lambda b, j: (b, j, 0))
    col_block = lambda c: pl.BlockSpec((None, D, RET_V_WIDTH), lambda b, j: (o, 0, c), **once)
    return pl.pallas_call(
        kern,
        grid=(B, S // tb),
        in_specs=[
            row_block(),
            _layer_spec((1, D), layer),
            col_block(0), col_block(1), col_block(2),
            _layer_spec((RET_V_WIDTH, D), o, **once),
            _const_spec((S // tb, RET_KEY_DIM)), _const_spec((S // tb, RET_KEY_DIM)),
            _const_spec((tb, RET_KEY_DIM)), _const_spec((tb, RET_KEY_DIM)),
            _const_spec((1, D)),
        ],
        out_specs=[
            row_block(),
            pl.BlockSpec((1, RET_HEADS, RET_KEY_DIM, RET_VALUE_DIM), lambda b, j: (b, 0, 0, 0)),
        ],
        out_shape=[
            jax.ShapeDtypeStruct((B, S, D), F32),
            jax.ShapeDtypeStruct((B, RET_HEADS, RET_KEY_DIM, RET_VALUE_DIM), F32),
        ],
        scratch_shapes=[
            pltpu.VMEM((tb, D), BF16),
            pltpu.VMEM((tb, 2 * RET_QK_WIDTH), F32),
            pltpu.VMEM((tb, RET_QK_WIDTH), BF16),
            pltpu.VMEM((tb, RET_QK_WIDTH), BF16),
            pltpu.VMEM((tb, RET_QK_WIDTH), BF16),
            pltpu.VMEM((tb, RET_V_WIDTH), BF16),
            pltpu.VMEM((tb, RET_V_WIDTH), F32),
            pltpu.VMEM((RET_HEADS, RET_KEY_DIM, RET_VALUE_DIM), F32),
            pltpu.VMEM((tb, RET_V_WIDTH), BF16),
            pltpu.VMEM((tb, RET_KEY_DIM), F32),
            pltpu.VMEM((tb, RET_KEY_DIM), F32),
            pltpu.VMEM((RET_HEADS, RET_PROMPT_CHUNK, RET_PROMPT_CHUNK), F32),
            pltpu.VMEM((RET_HEADS, RET_PROMPT_CHUNK, 1), F32),
            pltpu.VMEM((RET_HEADS, tb, 1), F32),
        ],
        compiler_params=pltpu.CompilerParams(
            dimension_semantics=("arbitrary", "arbitrary"), vmem_limit_bytes=VMEM_LIMIT),
        name="odd_prompt",
    )(x, gains, win_all, win_all, win_all, wout_all, cblk, sblk, crow, srow, fg)


def _same_batch_causal(shape, row0=0, col0=0, steps=4):
    r = _iota(shape, 0) + row0
    c = _iota(shape, 1) + col0
    return (r // steps == c // steps) & (c % steps <= r % steps)


def _even_sample_a_kernel(ws4_ref, bs4_ref, x_ref, g_ref, win_ref, lng_ref,
                          q_ref, k_ref, v_ref, sga_ref, mixb_ref, vn_ref, *, steps, layer):
    x = x_ref[...]
    xn = _rmsnorm(x, g_ref[...]).astype(BF16)
    z = _dot(xn, win_ref[...])
    q_ref[...] = z[:, E_Q:E_Q + SWA_WIDTH] * (SWA_HEAD_DIM ** -0.5)
    k_ref[...] = z[:, E_K:E_K + SWA_KV_WIDTH]
    v_ref[...] = z[:, E_V:E_V + SWA_KV_WIDTH]
    sga_ref[...] = _silu(z[:, E_GA:E_GA + SWA_WIDTH])
    vn = _layernorm_nogain(z[:, E_VB:E_VB + GMLP_WIDTH]) * lng_ref[...]
    vn_ref[...] = vn
    vnb = vn.astype(BF16)
    same = _iota((BLK, BLK), 0) // steps == _iota((BLK, BLK), 1) // steps
    rstep = _iota((BLK, BLK), 0) % steps
    cstep = _iota((BLK, BLK), 1) % steps
    rstep_col = _iota((BLK, 1), 0) % steps
    for g in range(GMLP_GROUPS):
        gs = slice(g * GMLP_GROUP_DIM, (g + 1) * GMLP_GROUP_DIM)
        wk = jnp.zeros((BLK, BLK), F32)
        bcol = jnp.zeros((BLK, 1), F32)
        for p in range(steps):
            bcol = jnp.where(rstep_col == p, bs4_ref[(layer * GMLP_GROUPS + g) * steps + p], bcol)
            for q in range(p + 1):
                w = ws4_ref[((layer * GMLP_GROUPS + g) * steps + p) * steps + q]
                wk = jnp.where(same & (rstep == p) & (cstep == q), w, wk)
        sp = _dot(wk.astype(BF16), vnb[:, gs]) + bcol
        u = z[:, E_U + g * GMLP_GROUP_DIM:E_U + (g + 1) * GMLP_GROUP_DIM]
        gb = z[:, E_GB + g * GMLP_GROUP_DIM:E_GB + (g + 1) * GMLP_GROUP_DIM]
        mixb_ref[:, gs] = (_silu(gb) * (u * sp)).astype(BF16)


def _even_sample_a(x, gains, layer, e, win_all, ws4, bs4, lng_all, *, steps):
    R, D = x.shape
    row = lambda w: pl.BlockSpec((BLK, w), lambda i: (i, 0))
    kern = functools.partial(_even_sample_a_kernel, steps=steps, layer=e)
    return pl.pallas_call(
        kern,
        grid=(R // BLK,),
        in_specs=[_smem_spec(), _smem_spec(), row(D), _layer_spec((1, D), layer), _layer_spec((D, EVEN_IN), e),
                  _layer_spec((1, GMLP_WIDTH), e)],
        out_specs=[row(SWA_WIDTH), row(SWA_KV_WIDTH), row(SWA_KV_WIDTH), row(SWA_WIDTH),
                   row(GMLP_WIDTH), row(GMLP_WIDTH)],
        out_shape=[
            jax.ShapeDtypeStruct((R, SWA_WIDTH), F32),
            jax.ShapeDtypeStruct((R, SWA_KV_WIDTH), F32),
            jax.ShapeDtypeStruct((R, SWA_KV_WIDTH), F32),
            jax.ShapeDtypeStruct((R, SWA_WIDTH), F32),
            jax.ShapeDtypeStruct((R, GMLP_WIDTH), BF16),
            jax.ShapeDtypeStruct((R, GMLP_WIDTH), F32),
        ],
        compiler_params=pltpu.CompilerParams(
            dimension_semantics=("arbitrary",), vmem_limit_bytes=VMEM_LIMIT),
        name="even_sample_a",
    )(ws4, bs4, x, gains, win_all, lng_all)


SAMPLE_BB = 32
PAIR_ROWS = 8
SAMPLE_PAIRS_PER_ITER = 2


def _even_sample_b_kernel(sinks_ref, table_ref, q_ref, kn_ref, vn_ref, ck_ref, cv_ref,
                          attn_ref, newk_ref, newv_ref,
                          sn_ref, kns_ref, vns_ref, biasc_ref, biasn_ref, *, steps, layer):
    kparts = _split_heads_kv(kn_ref[...])
    vparts = _split_heads_kv(vn_ref[...])
    for h in range(SWA_KV_HEADS):
        kns_ref[h] = jnp.concatenate(kparts[h], axis=0)
        vns_ref[h] = jnp.concatenate(vparts[h], axis=0)
        c0 = h * SWA_GROUP * SWA_HEAD_DIM
        lhs = jnp.concatenate([q_ref[:, c0:c0 + LANES], q_ref[:, c0 + LANES:c0 + 2 * LANES]], axis=0)
        sn_ref[h] = _dot_nt(lhs.astype(BF16), kns_ref[h])

    step_r = _iota((PAIR_ROWS, BLK), 0) % steps
    lane = _iota((PAIR_ROWS, BLK), 1)
    dist_c = step_r + WINDOW - lane
    valid_c = (dist_c >= 0) & (dist_c < WINDOW)
    dist_n = step_r - lane % steps
    first_local = _iota((PAIR_ROWS, 1), 0) < steps
    for hd in range(SWA_HEADS):
        biasc_ref[hd] = _t5_bias(dist_c, table_ref, hd)
        biasn_ref[hd] = _t5_bias(dist_n, table_ref, hd)

    first16 = jnp.concatenate([first_local, first_local], axis=0)

    def pair_group(it, carry):
        units = []
        for k in range(SAMPLE_PAIRS_PER_ITER):
            i = it * SAMPLE_PAIRS_PER_ITER + k
            r0 = pl.multiple_of(i * PAIR_ROWS, PAIR_ROWS)
            valid_n = _same_batch_causal((PAIR_ROWS, BLK), row0=r0, steps=steps)
            kc = [_split_heads_kv(ck_ref[2 * i + e]) for e in range(2)]
            vc = [_split_heads_kv(cv_ref[2 * i + e]) for e in range(2)]
            for h in range(SWA_KV_HEADS):
                units.append((r0, valid_n, h, [kc[e][h] for e in range(2)], [vc[e][h] for e in range(2)]))
            for e in range(2):
                be = 2 * i + e
                newk_ref[be, 0:WINDOW - steps, :] = ck_ref[be, steps:WINDOW, :]
                newv_ref[be, 0:WINDOW - steps, :] = cv_ref[be, steps:WINDOW, :]
                newk_ref[be, WINDOW - steps:WINDOW, :] = kn_ref[pl.ds(r0 + e * steps, steps), :]
                newv_ref[be, WINDOW - steps:WINDOW, :] = vn_ref[pl.ds(r0 + e * steps, steps), :]

        scores = []
        for r0, valid_n, h, kch, vch in units:
            rows = pl.ds(r0, PAIR_ROWS)
            c0 = h * SWA_GROUP * SWA_HEAD_DIM
            lhs = jnp.concatenate([q_ref[rows, c0:c0 + LANES], q_ref[rows, c0 + LANES:c0 + 2 * LANES]],
                                  axis=0).astype(BF16)
            sc_e = [_dot_nt(lhs, jnp.concatenate(kch[e], axis=0)) for e in range(2)]
            scores.append(jnp.where(first16, sc_e[0], sc_e[1]))

        tcs, tns, sks = [], [], []
        for (r0, valid_n, h, kch, vch), sc in zip(units, scores):
            for p in range(2):
                pr = slice(p * PAIR_ROWS, (p + 1) * PAIR_ROWS)
                for gi in range(2):
                    hd = h * SWA_GROUP + p * 2 + gi
                    gc = slice(gi * BLK, (gi + 1) * BLK)
                    tcs.append(jnp.where(valid_c, sc[pr, gc] + biasc_ref[hd], NEG_INF))
                    tn_raw = sn_ref[h, pl.ds(p * BLK + r0, PAIR_ROWS), gc]
                    tns.append(jnp.where(valid_n, tn_raw + biasn_ref[hd], NEG_INF))
                    sks.append(sinks_ref[layer, hd])
        ms = [jnp.maximum(jnp.maximum(jnp.max(tc, axis=-1, keepdims=True), jnp.max(tn, axis=-1, keepdims=True)), sk)
              for tc, tn, sk in zip(tcs, tns, sks)]
        ecs = [jnp.exp(tc - m) for tc, m in zip(tcs, ms)]
        ens = [jnp.exp(tn - m) for tn, m in zip(tns, ms)]
        invs = [1.0 / (jnp.exp(sk - m) + jnp.sum(ec, axis=-1, keepdims=True) + jnp.sum(en, axis=-1, keepdims=True))
                for sk, m, ec, en in zip(sks, ms, ecs, ens)]
        pcs = [ec * inv for ec, inv in zip(ecs, invs)]
        pns = [en * inv for en, inv in zip(ens, invs)]

        for n, (r0, valid_n, h, kch, vch) in enumerate(units):
            rows = pl.ds(r0, PAIR_ROWS)
            c0 = h * SWA_GROUP * SWA_HEAD_DIM
            grid16 = lambda ps: jnp.concatenate(
                [jnp.concatenate(ps[4 * n + 2 * p:4 * n + 2 * p + 2], axis=1) for p in range(2)], axis=0).astype(BF16)
            pc16 = grid16(pcs)
            pn16 = grid16(pns)
            o_e = [_dot(pc16, jnp.concatenate(vch[e], axis=0)) for e in range(2)]
            o = jnp.where(first16, o_e[0], o_e[1]) + _dot(pn16, vns_ref[h])
            attn_ref[rows, c0:c0 + LANES] = o[0:PAIR_ROWS]
            attn_ref[rows, c0 + LANES:c0 + 2 * LANES] = o[PAIR_ROWS:2 * PAIR_ROWS]
        return carry

    lax.fori_loop(0, SAMPLE_BB // (2 * SAMPLE_PAIRS_PER_ITER), pair_group, 0)


def _ignoring_refs(kernel_fn, start, count):
    if count == 0:
        return kernel_fn

    def wrapped(*refs):
        return kernel_fn(*refs[:start], *refs[start + count:])
    return wrapped


def _layer_slab_args(prev_outs, first_out_index, n_fixed_inputs):
    if prev_outs is None:
        return [], [], {}
    specs = [pl.BlockSpec(memory_space=pl.ANY) for _ in prev_outs]
    aliases = {n_fixed_inputs + k: first_out_index + k for k in range(len(prev_outs))}
    return list(prev_outs), specs, aliases


def _even_sample_b(q, kn, vn, ck_all, cv_all, layer, prev_outs, sinks, table_flat, *, steps):
    R = q.shape[0]
    row = lambda w: pl.BlockSpec((BLK, w), lambda i: (i, 0))
    cache = pl.BlockSpec((None, SAMPLE_BB, WINDOW, SWA_KV_WIDTH), lambda i: (layer, i, 0, 0))
    extra, extra_specs, aliases = _layer_slab_args(prev_outs, 1, 7)
    kern = _ignoring_refs(functools.partial(_even_sample_b_kernel, steps=steps, layer=layer), 7, len(extra))
    return pl.pallas_call(
        kern,
        grid=(R // BLK,),
        in_specs=[_smem_spec(), _smem_spec(), row(SWA_WIDTH), row(SWA_KV_WIDTH), row(SWA_KV_WIDTH), cache, cache]
        + extra_specs,
        out_specs=[row(SWA_WIDTH), cache, cache],
        out_shape=[
            jax.ShapeDtypeStruct((R, SWA_WIDTH), F32),
            jax.ShapeDtypeStruct(ck_all.shape, F32),
            jax.ShapeDtypeStruct(cv_all.shape, F32),
        ],
        input_output_aliases=aliases,
        scratch_shapes=[
            pltpu.VMEM((SWA_KV_HEADS, 2 * BLK, 2 * BLK), F32),
            pltpu.VMEM((SWA_KV_HEADS, 2 * BLK, LANES), BF16),
            pltpu.VMEM((SWA_KV_HEADS, 2 * BLK, LANES), BF16),
            pltpu.VMEM((SWA_HEADS, PAIR_ROWS, BLK), F32),
            pltpu.VMEM((SWA_HEADS, PAIR_ROWS, BLK), F32),
        ],
        compiler_params=pltpu.CompilerParams(
            dimension_semantics=("arbitrary",), vmem_limit_bytes=VMEM_LIMIT),
        name="even_sample_b",
    )(sinks, table_flat, q, kn, vn, ck_all, cv_all, *extra)


def _even_sample_c_kernel(x_ref, sga_ref, attn_ref, mixb_ref, wout_ref, y_ref):
    mixa = (sga_ref[...] * attn_ref[...]).astype(BF16)
    mix = jnp.concatenate([mixa, mixb_ref[...]], axis=1)
    y_ref[...] = x_ref[...] + _dot(mix, wout_ref[...])


def _even_sample_c(x, sga, attn, mixb, wout_all, e):
    R, D = x.shape
    row = lambda w: pl.BlockSpec((BLK, w), lambda i: (i, 0))
    return pl.pallas_call(
        _even_sample_c_kernel,
        grid=(R // BLK,),
        in_specs=[row(D), row(SWA_WIDTH), row(SWA_WIDTH), row(GMLP_WIDTH), _layer_spec((EVEN_MIX, D), e)],
        out_specs=row(D),
        out_shape=jax.ShapeDtypeStruct((R, D), F32),
        compiler_params=pltpu.CompilerParams(
            dimension_semantics=("arbitrary",), vmem_limit_bytes=VMEM_LIMIT),
        name="even_sample_c",
    )(x, sga, attn, mixb, wout_all)


def _odd_sample_a_kernel(x_ref, g_ref, wqk_ref, wv_ref, wg_ref, cos8_ref, sin8_ref,
                         q_ref, kdt_ref, v_ref, intra_ref, gate_ref, *, steps):
    x = x_ref[...]
    xn = _rmsnorm(x, g_ref[...]).astype(BF16)
    reps = BLK // cos8_ref.shape[0]
    cs = jnp.concatenate([cos8_ref[...]] * reps, axis=0)
    sn = jnp.concatenate([sin8_ref[...]] * reps, axis=0)
    rr = _iota((BLK, BLK), 0)
    cc = _iota((BLK, BLK), 1)
    mask = _same_batch_causal((BLK, BLK), steps=steps)
    diff = (rr % steps - cc % steps).astype(F32)
    step_col = (_iota((BLK, 1), 0) % steps).astype(F32)
    zqk = _dot(xn, wqk_ref[...])
    zv = _dot(xn, wv_ref[...])
    gate_ref[...] = _silu(_dot(xn, wg_ref[...]))
    v_ref[...] = zv.astype(BF16)
    for h in range(RET_HEADS):
        lg = _ret_log_gamma(h)
        qcols = slice(h * RET_KEY_DIM, (h + 1) * RET_KEY_DIM)
        kcols = slice(RET_QK_WIDTH + h * RET_KEY_DIM, RET_QK_WIDTH + (h + 1) * RET_KEY_DIM)
        vcols = slice(h * RET_VALUE_DIM, (h + 1) * RET_VALUE_DIM)
        qr = _xpos_rotate(zqk[:, qcols], cs, sn)
        kr = _xpos_rotate(zqk[:, kcols], cs, sn) * (RET_KEY_DIM ** -0.5)
        qb = qr.astype(BF16)
        decay = jnp.where(mask, jnp.exp(lg * jnp.maximum(diff, 0.0)), 0.0)
        att = _dot_nt(qb, kr.astype(BF16)) * decay
        intra_ref[:, vcols] = _dot(att.astype(BF16), zv[:, vcols].astype(BF16))
        kd = kr * jnp.exp(lg * (steps - 1.0 - step_col))
        q_ref[:, qcols] = qb
        kdt_ref[h] = kd.T.astype(BF16)


def _odd_sample_a(x, gains, layer, o, win_all, cos8, sin8, *, steps):
    R, D = x.shape
    row = lambda w: pl.BlockSpec((BLK, w), lambda i: (i, 0))
    kern = functools.partial(_odd_sample_a_kernel, steps=steps)
    once = dict(pipeline_mode=pl.Buffered(1))
    col_block = lambda c: pl.BlockSpec((None, D, RET_V_WIDTH), lambda i: (o, 0, c), **once)
    return pl.pallas_call(
        kern,
        grid=(R // BLK,),
        in_specs=[row(D), _layer_spec((1, D), layer),
                  col_block(0), col_block(1), col_block(2),
                  _const_spec(cos8.shape), _const_spec(sin8.shape)],
        out_specs=[row(RET_QK_WIDTH),
                   pl.BlockSpec((RET_HEADS, RET_KEY_DIM, BLK), lambda i: (0, 0, i)),
                   row(RET_V_WIDTH), row(RET_V_WIDTH), row(RET_V_WIDTH)],
        out_shape=[
            jax.ShapeDtypeStruct((R, RET_QK_WIDTH), BF16),
            jax.ShapeDtypeStruct((RET_HEADS, RET_KEY_DIM, R), BF16),
            jax.ShapeDtypeStruct((R, RET_V_WIDTH), BF16),
            jax.ShapeDtypeStruct((R, RET_V_WIDTH), F32),
            jax.ShapeDtypeStruct((R, RET_V_WIDTH), F32),
        ],
        compiler_params=pltpu.CompilerParams(
            dimension_semantics=("arbitrary",), vmem_limit_bytes=VMEM_LIMIT),
        name="odd_sample_a",
    )(x, gains, win_all, win_all, win_all, cos8, sin8)


STATE_BB = 4
STATE_HB = 2
STATE_HEAD_GROUPS = RET_HEADS // STATE_HB


def _retention_state_block(blk, hg, q_ref, kdt_ref, v_ref, s_ref, cross_ref, snew_ref, *, steps):
    rows16 = STATE_BB * steps
    lane_base = (blk % (BLK // rows16)) * rows16
    lane = _iota((RET_KEY_DIM, BLK), 1)
    row16 = _iota((rows16, 1), 0)
    v = v_ref[...]
    for hl in range(STATE_HB):
        lg = jnp.float32(_ret_log_gamma(hl))
        for g in range(1, STATE_HEAD_GROUPS):
            lg = jnp.where(hg == g, jnp.float32(_ret_log_gamma(g * STATE_HB + hl)), lg)
        qdec = jnp.exp(lg * ((row16 % steps).astype(F32) + 1.0))
        cdec = jnp.exp(jnp.full((1, 1), lg * steps, F32))
        q = q_ref[:, hl * RET_KEY_DIM:(hl + 1) * RET_KEY_DIM]
        kdt = kdt_ref[hl]
        vh = v[:, hl * RET_VALUE_DIM:(hl + 1) * RET_VALUE_DIM]
        cross = jnp.zeros((rows16, RET_VALUE_DIM), F32)
        for bl in range(STATE_BB):
            st = s_ref[bl, hl]
            cr = _dot(q, st.astype(BF16))
            cross = jnp.where(row16 // steps == bl, cr, cross)
            mine = (lane - lane_base) // steps == bl
            upd = _dot(jnp.where(mine, kdt, jnp.zeros_like(kdt)), vh)
            snew_ref[bl, hl] = cdec * st + upd
        cross_ref[:, hl * RET_VALUE_DIM:(hl + 1) * RET_VALUE_DIM] = cross * qdec


def _odd_sample_c_kernel(x_ref, intra_ref, cross_ref, gate_ref, wout_ref, fg_ref, y_ref, *, final_norm):
    gated = []
    for h in range(RET_HEADS):
        hs = slice(h * RET_VALUE_DIM, (h + 1) * RET_VALUE_DIM)
        on = _layernorm_nogain(intra_ref[:, hs] + cross_ref[:, hs])
        gated.append((gate_ref[:, hs] * on).astype(BF16))
    y = x_ref[...] + _dot(jnp.concatenate(gated, axis=1), wout_ref[...])
    if final_norm:
        y = _rmsnorm(y, fg_ref[...])
    y_ref[...] = y


def _odd_sample_c(x, intra, cross, gate, wout_all, o, fg, *, final_norm):
    R, D = x.shape
    row = lambda w: pl.BlockSpec((BLK, w), lambda i: (i, 0))
    kern = functools.partial(_odd_sample_c_kernel, final_norm=final_norm)
    return pl.pallas_call(
        kern,
        grid=(R // BLK,),
        in_specs=[row(D), row(RET_V_WIDTH), row(RET_V_WIDTH), row(RET_V_WIDTH),
                  _layer_spec((RET_V_WIDTH, D), o), _const_spec((1, D))],
        out_specs=row(D),
        out_shape=jax.ShapeDtypeStruct((R, D), F32),
        compiler_params=pltpu.CompilerParams(
            dimension_semantics=("arbitrary",), vmem_limit_bytes=VMEM_LIMIT),
        name="odd_sample_c",
    )(x, intra, cross, gate, wout_all, fg)


def _xpos_tables(pos):
    angle = 1.0 / (10000.0 ** jnp.linspace(0.0, 1.0, RET_KEY_DIM // 2, dtype=F32))
    ang = pos.astype(F32)[:, None] * angle[None, :]
    sin = jnp.sin(ang)
    cos_f = jnp.repeat(jnp.cos(ang), 2, axis=-1)
    sin_s = jnp.stack([-sin, sin], axis=-1).reshape(pos.shape[0], RET_KEY_DIM)
    return cos_f, sin_s


PROMPT_TB_EVEN = 256
PROMPT_TB_ODD = 512


def kernel(x_prompt, x_sample, cache_swa_k, cache_swa_v, state_ret, norm_gain, final_norm_gain,
           rel_bias_table, even_w_in, even_w_out, swa_sinks, gmlp_ws, gmlp_bs, gmlp_ln_gain,
           odd_w_in, odd_w_out):
    B, S, D = x_prompt.shape
    DB, T, _ = x_sample.shape
    R = DB * T
    n_even = even_w_in.shape[0]
    tb = PROMPT_TB_ODD

    table_flat = rel_bias_table.reshape(-1)
    gains = norm_gain.reshape(DEPTH, 1, D)
    fg = final_norm_gain.reshape(1, D)
    even_win = even_w_in.astype(BF16)
    even_wout = even_w_out.astype(BF16)
    odd_win = odd_w_in.astype(BF16)
    odd_wout = odd_w_out.astype(BF16)
    bst_all = gmlp_bs[:, :, :, None]
    lng_all = gmlp_ln_gain.reshape(n_even, 1, GMLP_WIDTH)
    ws4 = gmlp_ws[:, :, :T, :T].reshape(-1)
    bs4 = gmlp_bs[:, :, :T].reshape(-1)

    prompt_tables = (_xpos_tables(jnp.arange(0, S, tb, dtype=jnp.int32))
                     + _xpos_tables(jnp.arange(tb, dtype=jnp.int32)))
    cos4, sin4 = _xpos_tables(PAST_LEN + jnp.arange(T, dtype=jnp.int32))
    cos8 = jnp.tile(cos4, (PAIR_ROWS // T, 1))
    sin8 = jnp.tile(sin4, (PAIR_ROWS // T, 1))

    yp = x_prompt
    ys = x_sample.reshape(R, D)
    ck_all = cache_swa_k.reshape(n_even, DB, WINDOW, SWA_KV_WIDTH)
    cv_all = cache_swa_v.reshape(n_even, DB, WINDOW, SWA_KV_WIDTH)
    new_caches = None
    new_state = None
    kp_l, vp_l, sp_l, gv_l = [], [], [], []
    assert DEPTH % 2 == 0
    for pair in range(DEPTH // 2):
        even_layer, odd_layer = 2 * pair, 2 * pair + 1
        last = odd_layer == DEPTH - 1
        q, kn, vn_, sga, mixb, gv = _even_sample_a(ys, gains, even_layer, pair, even_win, ws4, bs4, lng_all,
                                                   steps=T)
        attn, nk, nv = _even_sample_b(q, kn, vn_, ck_all, cv_all, pair, new_caches, swa_sinks, table_flat, steps=T)
        new_caches = (nk, nv)
        ys = _even_sample_c(ys, sga, attn, mixb, even_wout, pair)
        gv_l.append(gv.reshape(DB, T, GMLP_WIDTH))
        q, kdt, v, intra, gate = _odd_sample_a(ys, gains, odd_layer, pair, odd_win, cos8, sin8, steps=T)
        yp, kp, vp, cross, new_state = _even_prompt(
            yp, gains, even_layer, pair, even_win, even_wout, swa_sinks, table_flat, gmlp_ws, bst_all, lng_all,
            (q, kdt, v, state_ret, pair, new_state), tb=PROMPT_TB_EVEN, steps=T)
        kp_l.append(kp.reshape(B, WINDOW, SWA_KV_HEADS, SWA_HEAD_DIM))
        vp_l.append(vp.reshape(B, WINDOW, SWA_KV_HEADS, SWA_HEAD_DIM))
        ys = _odd_sample_c(ys, intra, cross, gate, odd_wout, pair, fg, final_norm=last)
        yp, sp = _odd_prompt(yp, gains, odd_layer, pair, odd_win, odd_wout, prompt_tables, fg,
                             tb=tb, final_norm=last)
        sp_l.append(sp)
    cache_shape = (n_even, DB, WINDOW, SWA_KV_HEADS, SWA_HEAD_DIM)
    return (yp, ys.reshape(DB, T, D), jnp.stack(kp_l), jnp.stack(vp_l),
            new_caches[0].reshape(cache_shape), new_caches[1].reshape(cache_shape),
            jnp.stack(sp_l), new_state, jnp.stack(gv_l))
```

```python
import functools
import math

import jax
import jax.numpy as jnp
from jax import lax
from jax.experimental import pallas as pl
from jax.experimental.pallas import tpu as pltpu

D_MODEL = 1024
DEPTH = 4
PAST_LEN = 8192
NORM_EPS = 1e-6
NEG_INF = -1e30

SWA_HEADS = 8
SWA_KV_HEADS = 2
SWA_HEAD_DIM = 64
SWA_GROUP = SWA_HEADS // SWA_KV_HEADS
SWA_WIDTH = SWA_HEADS * SWA_HEAD_DIM
SWA_KV_WIDTH = SWA_KV_HEADS * SWA_HEAD_DIM
WINDOW = 128
REL_BUCKETS = 32
REL_MAX_DIST = 128

GMLP_GROUPS = 4
GMLP_CHUNK = 128
GMLP_WIDTH = D_MODEL // 2
GMLP_GROUP_DIM = GMLP_WIDTH // GMLP_GROUPS

E_Q = 0
E_K = E_Q + SWA_WIDTH
E_V = E_K + SWA_KV_WIDTH
E_GA = E_V + SWA_KV_WIDTH
E_U = E_GA + SWA_WIDTH
E_VB = E_U + GMLP_WIDTH
E_GB = E_VB + GMLP_WIDTH
EVEN_IN = E_GB + GMLP_WIDTH
EVEN_MIX = SWA_WIDTH + GMLP_WIDTH

RET_HEADS = 4
RET_KEY_DIM = 256
RET_VALUE_DIM = 512
RET_QK_WIDTH = RET_HEADS * RET_KEY_DIM
RET_V_WIDTH = RET_HEADS * RET_VALUE_DIM
RET_CHUNK = 128
ODD_IN = 2 * RET_QK_WIDTH + 2 * RET_V_WIDTH

LANES = 128
BLK = 128
VMEM_LIMIT = 56 * 1024 * 1024

F32 = jnp.float32
BF16 = jnp.bfloat16


def _ret_log_gamma(h):
    return math.log(1.0 - 2.0 ** (-5.0 - h))


def _dot(a, b):
    return jnp.dot(a, b, preferred_element_type=F32)


def _dot_nt(a, b):
    return lax.dot_general(a, b, (((1,), (1,)), ((), ())), preferred_element_type=F32)


def _dot_tn(a, b):
    return lax.dot_general(a, b, (((0,), (0,)), ((), ())), preferred_element_type=F32)


def _silu(x):
    return x * (1.0 / (1.0 + jnp.exp(-x)))


def _rmsnorm(x, g):
    ms = jnp.mean(x * x, axis=-1, keepdims=True)
    return x * lax.rsqrt(ms + NORM_EPS) * g


def _layernorm_nogain(x):
    mu = jnp.mean(x, axis=-1, keepdims=True)
    d = x - mu
    var = jnp.mean(d * d, axis=-1, keepdims=True)
    return d * lax.rsqrt(var + NORM_EPS)


def _iota(shape, dim):
    return lax.broadcasted_iota(jnp.int32, shape, dim)


def _t5_bias(dist, table_ref, head):
    n = jnp.maximum(dist, 0)
    max_exact = REL_BUCKETS // 2
    nf = jnp.maximum(n, 1).astype(F32)
    large = max_exact + (jnp.log(nf / max_exact) / math.log(REL_MAX_DIST / max_exact)
                         * (REL_BUCKETS - max_exact)).astype(jnp.int32)
    large = jnp.minimum(large, REL_BUCKETS - 1)
    bucket = jnp.where(n < max_exact, n, large)
    acc = jnp.zeros(dist.shape, F32)
    for b in range(REL_BUCKETS):
        acc = jnp.where(bucket >= b, table_ref[b * SWA_HEADS + head], acc)
    return acc


def _split_heads_kv(x):
    lo = _iota(x.shape, 1) < SWA_HEAD_DIM
    xr = pltpu.roll(x, SWA_HEAD_DIM, 1)
    a0 = jnp.where(lo, x, 0.0).astype(BF16)
    b0 = jnp.where(lo, 0.0, xr).astype(BF16)
    a1 = jnp.where(lo, xr, 0.0).astype(BF16)
    b1 = jnp.where(lo, 0.0, x).astype(BF16)
    return (a0, b0), (a1, b1)


def _sink_softmax_parts(parts, sink):
    m = sink
    for t in parts:
        m = jnp.maximum(m, jnp.max(t, axis=-1, keepdims=True))
    es = [jnp.exp(t - m) for t in parts]
    den = jnp.exp(sink - m)
    for e in es:
        den = den + jnp.sum(e, axis=-1, keepdims=True)
    inv = 1.0 / den
    return [e * inv for e in es]


def _xpos_rotate(x, cos_f, sin_s):
    even = (_iota((x.shape[0], LANES), 1) % 2) == 0
    outs = []
    for c in range(x.shape[1] // LANES):
        sl = slice(c * LANES, (c + 1) * LANES)
        xs = x[:, sl]
        partner = jnp.where(even, pltpu.roll(xs, LANES - 1, 1), pltpu.roll(xs, 1, 1))
        outs.append(xs * cos_f[:, sl] + partner * sin_s[:, sl])
    return jnp.concatenate(outs, axis=1)


def _layer_spec(shape, layer, **kwargs):
    nd = len(shape)
    return pl.BlockSpec((None,) + tuple(shape), lambda *_: (layer,) + (0,) * nd, **kwargs)


def _const_spec(shape):
    nd = len(shape)
    return pl.BlockSpec(shape, lambda *_: (0,) * nd)


def _smem_spec():
    return pl.BlockSpec(memory_space=pltpu.SMEM)


def _even_prompt_kernel(sinks_ref, table_ref, x_ref, g_ref, win_ref, wout_ref, ws_ref, bst_ref, lng_ref,
                        y_ref, newk_ref, newv_ref,
                        z_ref, zga_ref, zvb_ref, qb_ref, vn_ref, mix_ref, ks_ref, vs_ref, bias_ref, wm_ref,
                        *, tb, layer, side_work=None):
    b = pl.program_id(0)
    j = pl.program_id(1)
    nsub = tb // BLK

    @pl.when(jnp.logical_and(b == 0, j == 0))
    def _init_tables():
        qi = _iota((BLK, BLK), 0)
        c = _iota((BLK, BLK), 1)
        own = c <= qi
        dist = jnp.where(own, qi - c, qi + BLK - c)
        for h in range(SWA_HEADS):
            bias = _t5_bias(dist, table_ref, h)
            bias_ref[0, h] = bias
            bias_ref[1, h] = jnp.where(own, bias, NEG_INF)
        causal = _iota((BLK, BLK), 0) >= _iota((BLK, BLK), 1)
        for g in range(GMLP_GROUPS):
            wm_ref[g] = jnp.where(causal, ws_ref[g], 0.0).astype(BF16)

    @pl.when(j == 0)
    def _zero_prev():
        zeros = jnp.zeros((BLK, LANES), BF16)
        for h in range(SWA_KV_HEADS):
            for ab in range(2):
                ks_ref[h, ab, 0:BLK, :] = zeros
                vs_ref[h, ab, 0:BLK, :] = zeros

    x = x_ref[0]
    xn = _rmsnorm(x, g_ref[...]).astype(BF16)
    z_ref[...] = _dot(xn, win_ref[:, 0:E_GA])
    zga_ref[...] = _dot(xn, win_ref[:, E_GA:E_VB])
    qb_ref[...] = (z_ref[:, E_Q:E_Q + SWA_WIDTH] * (SWA_HEAD_DIM ** -0.5)).astype(BF16)
    kparts = _split_heads_kv(z_ref[:, E_K:E_K + SWA_KV_WIDTH])
    vparts = _split_heads_kv(z_ref[:, E_V:E_V + SWA_KV_WIDTH])
    for h in range(SWA_KV_HEADS):
        for ab in range(2):
            ks_ref[h, ab, BLK:BLK + tb, :] = kparts[h][ab]
            vs_ref[h, ab, BLK:BLK + tb, :] = vparts[h][ab]

    own = _iota((BLK, BLK), 1) <= _iota((BLK, BLK), 0)
    col0 = lambda h, p: h * SWA_GROUP * SWA_HEAD_DIM + p * LANES
    sub_rows = [slice(s * BLK, (s + 1) * BLK) for s in range(nsub)]
    units = []
    for s in range(nsub):
        first = jnp.where(j * nsub + s == 0, 1, 0)
        for h in range(SWA_KV_HEADS):
            for p in range(SWA_GROUP // 2):
                units.append((sub_rows[s], slice(s * BLK, (s + 2) * BLK), first, h, p))
    stacked = lambda ref, h, pc: jnp.concatenate([ref[h, 0, pc, :], ref[h, 1, pc, :]], axis=0)
    logits, sinks = [], []
    for rows, prev_cur, first, h, p in units:
        c0 = col0(h, p)
        sc = _dot_nt(qb_ref[rows, c0:c0 + LANES], stacked(ks_ref, h, prev_cur))
        for gi in range(2):
            hd = h * SWA_GROUP + p * 2 + gi
            s_prev = sc[:, (2 * gi) * BLK:(2 * gi + 1) * BLK]
            s_cur = sc[:, (2 * gi + 1) * BLK:(2 * gi + 2) * BLK]
            logits.append(jnp.where(own, s_cur, s_prev) + bias_ref[first, hd])
            sinks.append(sinks_ref[layer, hd])

    zvb_ref[...] = _dot(xn, win_ref[:, E_VB:EVEN_IN])

    maxes = [jnp.maximum(jnp.max(t, axis=-1, keepdims=True), sk) for t, sk in zip(logits, sinks)]
    exps = [jnp.exp(t - m) for t, m in zip(logits, maxes)]
    dens = [jnp.sum(e, axis=-1, keepdims=True) + jnp.exp(sk - m) for e, sk, m in zip(exps, sinks, maxes)]
    probs = [e * (1.0 / d) for e, d in zip(exps, dens)]

    if side_work is not None:
        side_work()

    zga_ref[:, 0:SWA_WIDTH] = _silu(zga_ref[:, 0:SWA_WIDTH])
    zvb_ref[:, GMLP_WIDTH:2 * GMLP_WIDTH] = _silu(zvb_ref[:, GMLP_WIDTH:2 * GMLP_WIDTH])
    vn_ref[...] = (_layernorm_nogain(zvb_ref[:, 0:GMLP_WIDTH]) * lng_ref[...]).astype(BF16)

    for n, (rows, prev_cur, first, h, p) in enumerate(units):
        c0 = col0(h, p)
        parts = []
        for pr in probs[2 * n:2 * n + 2]:
            parts.append(jnp.where(own, 0.0, pr).astype(BF16))
            parts.append(jnp.where(own, pr, 0.0).astype(BF16))
        o = _dot(jnp.concatenate(parts, axis=1), stacked(vs_ref, h, prev_cur))
        mix_ref[rows, c0:c0 + LANES] = (zga_ref[rows, c0:c0 + LANES] * o).astype(BF16)

    for rows in sub_rows:
        for g in range(GMLP_GROUPS):
            gs = slice(g * GMLP_GROUP_DIM, (g + 1) * GMLP_GROUP_DIM)
            sp = _dot(wm_ref[g], vn_ref[rows, gs]) + bst_ref[g]
            u = zga_ref[rows, SWA_WIDTH + g * GMLP_GROUP_DIM:SWA_WIDTH + (g + 1) * GMLP_GROUP_DIM]
            sgb = zvb_ref[rows, GMLP_WIDTH + g * GMLP_GROUP_DIM:GMLP_WIDTH + (g + 1) * GMLP_GROUP_DIM]
            mix_ref[rows, SWA_WIDTH + g * GMLP_GROUP_DIM:SWA_WIDTH + (g + 1) * GMLP_GROUP_DIM] = (
                sgb * (u * sp)).astype(BF16)

    for h in range(SWA_KV_HEADS):
        for ab in range(2):
            ks_ref[h, ab, 0:BLK, :] = ks_ref[h, ab, tb:tb + BLK, :]
            vs_ref[h, ab, 0:BLK, :] = vs_ref[h, ab, tb:tb + BLK, :]

    y_ref[0] = x + _dot(mix_ref[...], wout_ref[...])

    @pl.when(j == pl.num_programs(1) - 1)
    def _emit_cache():
        newk_ref[0] = z_ref[tb - WINDOW:tb, E_K:E_K + SWA_KV_WIDTH]
        newv_ref[0] = z_ref[tb - WINDOW:tb, E_V:E_V + SWA_KV_WIDTH]


N_EVEN_PROMPT_INPUTS = 9
N_STATE_INPUTS = 4


def _even_prompt_state_kernel(*refs, tb, layer, steps, n_alias):
    prompt_in = refs[:N_EVEN_PROMPT_INPUTS]
    state_in = refs[N_EVEN_PROMPT_INPUTS:N_EVEN_PROMPT_INPUTS + N_STATE_INPUTS]
    rest = refs[N_EVEN_PROMPT_INPUTS + N_STATE_INPUTS + n_alias:]
    y_ref, newk_ref, newv_ref, cross_ref, snew_ref = rest[:5]
    lin = pl.program_id(0) * pl.num_programs(1) + pl.program_id(1)
    state_work = functools.partial(_retention_state_block, lin // STATE_HEAD_GROUPS, lin % STATE_HEAD_GROUPS,
                                   *state_in, cross_ref, snew_ref, steps=steps)
    _even_prompt_kernel(*prompt_in, y_ref, newk_ref, newv_ref, *rest[5:], tb=tb, layer=layer,
                        side_work=state_work)


def _even_prompt(x, gains, layer, e, win_all, wout_all, sinks, table_flat, ws_all, bst_all, lng_all,
                 state_job, *, tb, steps):
    B, S, D = x.shape
    nj = S // tb
    q, kdt, v, state_all, o, prev_state = state_job
    R = q.shape[0]
    DB = state_all.shape[1]
    assert B * nj == (DB // STATE_BB) * STATE_HEAD_GROUPS, "one retention-state block per prompt grid step"
    rows16 = STATE_BB * steps
    per_lane_blk = BLK // rows16
    sblk = lambda b, j: (b * nj + j) // STATE_HEAD_GROUPS
    shg = lambda b, j: (b * nj + j) % STATE_HEAD_GROUPS
    st_spec = pl.BlockSpec((None, STATE_BB, STATE_HB, RET_KEY_DIM, RET_VALUE_DIM),
                           lambda b, j: (o, sblk(b, j), shg(b, j), 0, 0))
    extra, extra_specs, aliases = _layer_slab_args(None if prev_state is None else (prev_state,), 4,
                                                   N_EVEN_PROMPT_INPUTS + N_STATE_INPUTS)
    once = dict(pipeline_mode=pl.Buffered(1))
    row_block = lambda: pl.BlockSpec((1, tb, D), lambda b, j: (b, j, 0))
    kern = functools.partial(_even_prompt_state_kernel, tb=tb, layer=e, steps=steps, n_alias=len(extra))
    return pl.pallas_call(
        kern,
        grid=(B, nj),
        in_specs=[
            _smem_spec(), _smem_spec(),
            row_block(),
            _layer_spec((1, D), layer),
            _layer_spec((D, EVEN_IN), e, **once),
            _layer_spec((EVEN_MIX, D), e, **once),
            _layer_spec((GMLP_GROUPS, GMLP_CHUNK, GMLP_CHUNK), e),
            _layer_spec((GMLP_GROUPS, GMLP_CHUNK, 1), e),
            _layer_spec((1, GMLP_WIDTH), e),
            pl.BlockSpec((rows16, STATE_HB * RET_KEY_DIM), lambda b, j: (sblk(b, j), shg(b, j))),
            pl.BlockSpec((STATE_HB, RET_KEY_DIM, BLK), lambda b, j: (shg(b, j), 0, sblk(b, j) // per_lane_blk)),
            pl.BlockSpec((BLK, STATE_HB * RET_VALUE_DIM), lambda b, j: (sblk(b, j) // per_lane_blk, shg(b, j))),
            st_spec,
        ] + extra_specs,
        out_specs=[
            row_block(),
            pl.BlockSpec((1, WINDOW, SWA_KV_WIDTH), lambda b, j: (b, 0, 0)),
            pl.BlockSpec((1, WINDOW, SWA_KV_WIDTH), lambda b, j: (b, 0, 0)),
            pl.BlockSpec((rows16, STATE_HB * RET_VALUE_DIM), lambda b, j: (sblk(b, j), shg(b, j))),
            st_spec,
        ],
        out_shape=[
            jax.ShapeDtypeStruct((B, S, D), F32),
            jax.ShapeDtypeStruct((B, WINDOW, SWA_KV_WIDTH), F32),
            jax.ShapeDtypeStruct((B, WINDOW, SWA_KV_WIDTH), F32),
            jax.ShapeDtypeStruct((R, RET_V_WIDTH), F32),
            jax.ShapeDtypeStruct(state_all.shape, F32),
        ],
        input_output_aliases=aliases,
        scratch_shapes=[
            pltpu.VMEM((tb, E_GA), F32),
            pltpu.VMEM((tb, E_VB - E_GA), F32),
            pltpu.VMEM((tb, EVEN_IN - E_VB), F32),
            pltpu.VMEM((tb, SWA_WIDTH), BF16),
            pltpu.VMEM((tb, GMLP_WIDTH), BF16),
            pltpu.VMEM((tb, EVEN_MIX), BF16),
            pltpu.VMEM((SWA_KV_HEADS, 2, BLK + tb, LANES), BF16),
            pltpu.VMEM((SWA_KV_HEADS, 2, BLK + tb, LANES), BF16),
            pltpu.VMEM((2, SWA_HEADS, BLK, BLK), F32),
            pltpu.VMEM((GMLP_GROUPS, GMLP_CHUNK, GMLP_CHUNK), BF16),
        ],
        compiler_params=pltpu.CompilerParams(
            dimension_semantics=("arbitrary", "arbitrary"), vmem_limit_bytes=VMEM_LIMIT),
        name="even_prompt",
    )(sinks, table_flat, x, gains, win_all, wout_all, ws_all, bst_all, lng_all, q, kdt, v, state_all, *extra)


RET_PROMPT_CHUNK = 256


def _odd_prompt_kernel(x_ref, g_ref, wqk_ref, wv_ref, wg_ref, wout_ref, cblk_ref, sblk_ref, crow_ref, srow_ref,
                       fg_ref, y_ref, sout_ref,
                       xn_ref, zqk_ref, qb_ref, kb_ref, kd_ref, vb_ref, sg_ref, s_ref, gated_ref, cos_ref, sin_ref,
                       decay_ref, qdec_ref, kdec_ref, *, tb, final_norm):
    b = pl.program_id(0)
    j = pl.program_id(1)
    ch = RET_PROMPT_CHUNK
    nchunk = tb // ch

    @pl.when(jnp.logical_and(b == 0, j == 0))
    def _init_tables():
        ii = _iota((ch, ch), 0)
        jj = _iota((ch, ch), 1)
        diff = (ii - jj).astype(F32)
        idx = _iota((ch, 1), 0).astype(F32)
        idx_blk = (_iota((tb, 1), 0) % ch).astype(F32)
        for h in range(RET_HEADS):
            lg = _ret_log_gamma(h)
            decay_ref[h] = jnp.where(diff >= 0, jnp.exp(lg * jnp.maximum(diff, 0.0)), 0.0)
            qdec_ref[h] = jnp.exp(lg * (idx + 1.0))
            kdec_ref[h] = jnp.exp(lg * (ch - 1.0 - idx_blk))

    @pl.when(j == 0)
    def _zero_state():
        s_ref[...] = jnp.zeros(s_ref.shape, F32)

    cb = cblk_ref[pl.ds(j, 1), :]
    sb = sblk_ref[pl.ds(j, 1), :]
    cos_ref[...] = cb * crow_ref[...] - sb * srow_ref[...]
    sin_ref[...] = sb * crow_ref[...] + cb * srow_ref[...]

    heads = range(RET_HEADS)
    hrows = tb // 2
    xn_ref[0:hrows, :] = _rmsnorm(x_ref[0, 0:hrows, :], g_ref[...]).astype(BF16)
    zqk_ref[0:hrows, :] = _dot(xn_ref[0:hrows, :], wqk_ref[...])
    xn_ref[hrows:tb, :] = _rmsnorm(x_ref[0, hrows:tb, :], g_ref[...]).astype(BF16)
    zqk_ref[hrows:tb, :] = _dot(xn_ref[hrows:tb, :], wqk_ref[...])
    sg_ref[...] = _silu(_dot(xn_ref[...], wg_ref[...]))
    vb_ref[...] = _dot(xn_ref[...], wv_ref[...]).astype(BF16)
    cs = cos_ref[...]
    sn = sin_ref[...]
    for h in heads:
        qcols = slice(h * RET_KEY_DIM, (h + 1) * RET_KEY_DIM)
        kcols = slice(RET_QK_WIDTH + h * RET_KEY_DIM, RET_QK_WIDTH + (h + 1) * RET_KEY_DIM)
        qb_ref[:, qcols] = _xpos_rotate(zqk_ref[:, qcols], cs, sn).astype(BF16)
        kr = _xpos_rotate(zqk_ref[:, kcols], cs, sn) * (RET_KEY_DIM ** -0.5)
        kb_ref[:, qcols] = kr.astype(BF16)
        kd_ref[:, qcols] = (kr * kdec_ref[h]).astype(BF16)

    qcols = [slice(h * RET_KEY_DIM, (h + 1) * RET_KEY_DIM) for h in heads]
    vcols = [slice(h * RET_VALUE_DIM, (h + 1) * RET_VALUE_DIM) for h in heads]
    o_ref = zqk_ref

    def recurrence(c):
        rows = slice(c * ch, (c + 1) * ch)
        qb = [qb_ref[rows, qcols[h]] for h in heads]
        vb = [vb_ref[rows, vcols[h]] for h in heads]
        att = [(_dot_nt(qb[h], kb_ref[rows, qcols[h]]) * decay_ref[h]).astype(BF16) for h in heads]
        cross = [_dot(qb[h], s_ref[h].astype(BF16)) * qdec_ref[h] for h in heads]
        for h in heads:
            o_ref[rows, vcols[h]] = _dot(att[h], vb[h]) + cross[h]
        for h in heads:
            s_ref[h] = (math.exp(_ret_log_gamma(h) * ch) * s_ref[h]
                        + _dot_tn(kd_ref[rows, qcols[h]], vb[h]))

    def norm_gate(c, hs):
        tiles = [(slice(r, r + BLK), vcols[h]) for h in hs for r in range(c * ch, (c + 1) * ch, BLK)]
        normed = [_layernorm_nogain(o_ref[r, v]) for r, v in tiles]
        for (r, v), on in zip(tiles, normed):
            gated_ref[r, v] = (sg_ref[r, v] * on).astype(BF16)

    for c in range(nchunk):
        recurrence(c)
        if c > 0:
            norm_gate(c - 1, heads)
    y = x_ref[0]
    for h in heads:
        norm_gate(nchunk - 1, [h])
        y = y + _dot(gated_ref[:, vcols[h]], wout_ref[vcols[h], :])
    if final_norm:
        y = _rmsnorm(y, fg_ref[...])
    y_ref[0] = y

    @pl.when(j == pl.num_programs(1) - 1)
    def _emit_state():
        sout_ref[0] = s_ref[...]


N_ODD_PROMPT_INPUTS = 11


def _odd_prompt(x, gains, layer, o, win_all, wout_all, tables, fg, prev_state, *, tb, final_norm):
    B, S, D = x.shape
    n_odd = win_all.shape[0]
    cblk, sblk, crow, srow = tables
    extra, extra_specs, aliases = _layer_slab_args(None if prev_state is None else (prev_state,), 1,
                                                   N_ODD_PROMPT_INPUTS)
    kern = _ignoring_refs(functools.partial(_odd_prompt_kernel, tb=tb, final_norm=final_norm),
                          N_ODD_PROMPT_INPUTS, len(extra))
    once = dict(pipeline_mode=pl.Buffered(1))
    row_block = lambda: pl.BlockSpec((1, tb, D), lambda b, j: (b, j, 0))
    col_block = lambda c: pl.BlockSpec((None, D, RET_V_WIDTH), lambda b, j: (o, 0, c), **once)
    return pl.pallas_call(
        kern,
        grid=(B, S // tb),
        in_specs=[
            row_block(),
            _layer_spec((1, D), layer),
            col_block(0), col_block(1), col_block(2),
            _layer_spec((RET_V_WIDTH, D), o, **once),
            _const_spec((S // tb, RET_KEY_DIM)), _const_spec((S // tb, RET_KEY_DIM)),
            _const_spec((tb, RET_KEY_DIM)), _const_spec((tb, RET_KEY_DIM)),
            _const_spec((1, D)),
        ] + extra_specs,
        out_specs=[
            row_block(),
            pl.BlockSpec((None, 1, RET_HEADS, RET_KEY_DIM, RET_VALUE_DIM), lambda b, j: (o, b, 0, 0, 0)),
        ],
        out_shape=[
            jax.ShapeDtypeStruct((B, S, D), F32),
            jax.ShapeDtypeStruct((n_odd, B, RET_HEADS, RET_KEY_DIM, RET_VALUE_DIM), F32),
        ],
        input_output_aliases=aliases,
        scratch_shapes=[
            pltpu.VMEM((tb, D), BF16),
            pltpu.VMEM((tb, 2 * RET_QK_WIDTH), F32),
            pltpu.VMEM((tb, RET_QK_WIDTH), BF16),
            pltpu.VMEM((tb, RET_QK_WIDTH), BF16),
            pltpu.VMEM((tb, RET_QK_WIDTH), BF16),
            pltpu.VMEM((tb, RET_V_WIDTH), BF16),
            pltpu.VMEM((tb, RET_V_WIDTH), F32),
            pltpu.VMEM((RET_HEADS, RET_KEY_DIM, RET_VALUE_DIM), F32),
            pltpu.VMEM((tb, RET_V_WIDTH), BF16),
            pltpu.VMEM((tb, RET_KEY_DIM), F32),
            pltpu.VMEM((tb, RET_KEY_DIM), F32),
            pltpu.VMEM((RET_HEADS, RET_PROMPT_CHUNK, RET_PROMPT_CHUNK), F32),
            pltpu.VMEM((RET_HEADS, RET_PROMPT_CHUNK, 1), F32),
            pltpu.VMEM((RET_HEADS, tb, 1), F32),
        ],
        compiler_params=pltpu.CompilerParams(
            dimension_semantics=("arbitrary", "arbitrary"), vmem_limit_bytes=VMEM_LIMIT),
        name="odd_prompt",
    )(x, gains, win_all, win_all, win_all, wout_all, cblk, sblk, crow, srow, fg, *extra)


def _same_batch_causal(shape, row0=0, col0=0, steps=4):
    r = _iota(shape, 0) + row0
    c = _iota(shape, 1) + col0
    return (r // steps == c // steps) & (c % steps <= r % steps)


def _even_sample_a_kernel(ws4_ref, bs4_ref, x_ref, g_ref, win_ref, lng_ref,
                          q_ref, k_ref, v_ref, sga_ref, mixb_ref, vn_ref, *, steps, layer):
    x = x_ref[...]
    xn = _rmsnorm(x, g_ref[...]).astype(BF16)
    z = _dot(xn, win_ref[...])
    q_ref[...] = z[:, E_Q:E_Q + SWA_WIDTH] * (SWA_HEAD_DIM ** -0.5)
    k_ref[...] = z[:, E_K:E_K + SWA_KV_WIDTH]
    v_ref[...] = z[:, E_V:E_V + SWA_KV_WIDTH]
    sga_ref[...] = _silu(z[:, E_GA:E_GA + SWA_WIDTH])
    vn = _layernorm_nogain(z[:, E_VB:E_VB + GMLP_WIDTH]) * lng_ref[...]
    vn_ref[...] = vn
    vnb = vn.astype(BF16)
    same = _iota((BLK, BLK), 0) // steps == _iota((BLK, BLK), 1) // steps
    rstep = _iota((BLK, BLK), 0) % steps
    cstep = _iota((BLK, BLK), 1) % steps
    rstep_col = _iota((BLK, 1), 0) % steps
    for g in range(GMLP_GROUPS):
        gs = slice(g * GMLP_GROUP_DIM, (g + 1) * GMLP_GROUP_DIM)
        wk = jnp.zeros((BLK, BLK), F32)
        bcol = jnp.zeros((BLK, 1), F32)
        for p in range(steps):
            bcol = jnp.where(rstep_col == p, bs4_ref[(layer * GMLP_GROUPS + g) * steps + p], bcol)
            for q in range(p + 1):
                w = ws4_ref[((layer * GMLP_GROUPS + g) * steps + p) * steps + q]
                wk = jnp.where(same & (rstep == p) & (cstep == q), w, wk)
        sp = _dot(wk.astype(BF16), vnb[:, gs]) + bcol
        u = z[:, E_U + g * GMLP_GROUP_DIM:E_U + (g + 1) * GMLP_GROUP_DIM]
        gb = z[:, E_GB + g * GMLP_GROUP_DIM:E_GB + (g + 1) * GMLP_GROUP_DIM]
        mixb_ref[:, gs] = (_silu(gb) * (u * sp)).astype(BF16)


def _even_sample_a(x, gains, layer, e, win_all, ws4, bs4, lng_all, *, steps):
    R, D = x.shape
    row = lambda w: pl.BlockSpec((BLK, w), lambda i: (i, 0))
    kern = functools.partial(_even_sample_a_kernel, steps=steps, layer=e)
    return pl.pallas_call(
        kern,
        grid=(R // BLK,),
        in_specs=[_smem_spec(), _smem_spec(), row(D), _layer_spec((1, D), layer), _layer_spec((D, EVEN_IN), e),
                  _layer_spec((1, GMLP_WIDTH), e)],
        out_specs=[row(SWA_WIDTH), row(SWA_KV_WIDTH), row(SWA_KV_WIDTH), row(SWA_WIDTH),
                   row(GMLP_WIDTH), row(GMLP_WIDTH)],
        out_shape=[
            jax.ShapeDtypeStruct((R, SWA_WIDTH), F32),
            jax.ShapeDtypeStruct((R, SWA_KV_WIDTH), F32),
            jax.ShapeDtypeStruct((R, SWA_KV_WIDTH), F32),
            jax.ShapeDtypeStruct((R, SWA_WIDTH), F32),
            jax.ShapeDtypeStruct((R, GMLP_WIDTH), BF16),
            jax.ShapeDtypeStruct((R, GMLP_WIDTH), F32),
        ],
        compiler_params=pltpu.CompilerParams(
            dimension_semantics=("arbitrary",), vmem_limit_bytes=VMEM_LIMIT),
        name="even_sample_a",
    )(ws4, bs4, x, gains, win_all, lng_all)


SAMPLE_BB = 32
PAIR_ROWS = 8
SAMPLE_PAIRS_PER_ITER = 2


def _even_sample_b_kernel(sinks_ref, table_ref, q_ref, kn_ref, vn_ref, ck_ref, cv_ref,
                          attn_ref, newk_ref, newv_ref,
                          sn_ref, kns_ref, vns_ref, biasc_ref, biasn_ref, *, steps, layer):
    kparts = _split_heads_kv(kn_ref[...])
    vparts = _split_heads_kv(vn_ref[...])
    for h in range(SWA_KV_HEADS):
        kns_ref[h] = jnp.concatenate(kparts[h], axis=0)
        vns_ref[h] = jnp.concatenate(vparts[h], axis=0)
        c0 = h * SWA_GROUP * SWA_HEAD_DIM
        lhs = jnp.concatenate([q_ref[:, c0:c0 + LANES], q_ref[:, c0 + LANES:c0 + 2 * LANES]], axis=0)
        sn_ref[h] = _dot_nt(lhs.astype(BF16), kns_ref[h])

    step_r = _iota((PAIR_ROWS, BLK), 0) % steps
    lane = _iota((PAIR_ROWS, BLK), 1)
    dist_c = step_r + WINDOW - lane
    valid_c = (dist_c >= 0) & (dist_c < WINDOW)
    dist_n = step_r - lane % steps
    first_local = _iota((PAIR_ROWS, 1), 0) < steps
    for hd in range(SWA_HEADS):
        biasc_ref[hd] = _t5_bias(dist_c, table_ref, hd)
        biasn_ref[hd] = _t5_bias(dist_n, table_ref, hd)

    first16 = jnp.concatenate([first_local, first_local], axis=0)

    def pair_group(it, carry):
        units = []
        for k in range(SAMPLE_PAIRS_PER_ITER):
            i = it * SAMPLE_PAIRS_PER_ITER + k
            r0 = pl.multiple_of(i * PAIR_ROWS, PAIR_ROWS)
            valid_n = _same_batch_causal((PAIR_ROWS, BLK), row0=r0, steps=steps)
            kc = [_split_heads_kv(ck_ref[2 * i + e]) for e in range(2)]
            vc = [_split_heads_kv(cv_ref[2 * i + e]) for e in range(2)]
            for h in range(SWA_KV_HEADS):
                units.append((r0, valid_n, h, [kc[e][h] for e in range(2)], [vc[e][h] for e in range(2)]))
            for e in range(2):
                be = 2 * i + e
                newk_ref[be, 0:WINDOW - steps, :] = ck_ref[be, steps:WINDOW, :]
                newv_ref[be, 0:WINDOW - steps, :] = cv_ref[be, steps:WINDOW, :]
                newk_ref[be, WINDOW - steps:WINDOW, :] = kn_ref[pl.ds(r0 + e * steps, steps), :]
                newv_ref[be, WINDOW - steps:WINDOW, :] = vn_ref[pl.ds(r0 + e * steps, steps), :]

        scores = []
        for r0, valid_n, h, kch, vch in units:
            rows = pl.ds(r0, PAIR_ROWS)
            c0 = h * SWA_GROUP * SWA_HEAD_DIM
            lhs = jnp.concatenate([q_ref[rows, c0:c0 + LANES], q_ref[rows, c0 + LANES:c0 + 2 * LANES]],
                                  axis=0).astype(BF16)
            sc_e = [_dot_nt(lhs, jnp.concatenate(kch[e], axis=0)) for e in range(2)]
            scores.append(jnp.where(first16, sc_e[0], sc_e[1]))

        tcs, tns, sks = [], [], []
        for (r0, valid_n, h, kch, vch), sc in zip(units, scores):
            for p in range(2):
                pr = slice(p * PAIR_ROWS, (p + 1) * PAIR_ROWS)
                for gi in range(2):
                    hd = h * SWA_GROUP + p * 2 + gi
                    gc = slice(gi * BLK, (gi + 1) * BLK)
                    tcs.append(jnp.where(valid_c, sc[pr, gc] + biasc_ref[hd], NEG_INF))
                    tn_raw = sn_ref[h, pl.ds(p * BLK + r0, PAIR_ROWS), gc]
                    tns.append(jnp.where(valid_n, tn_raw + biasn_ref[hd], NEG_INF))
                    sks.append(sinks_ref[layer, hd])
        ms = [jnp.maximum(jnp.maximum(jnp.max(tc, axis=-1, keepdims=True), jnp.max(tn, axis=-1, keepdims=True)), sk)
              for tc, tn, sk in zip(tcs, tns, sks)]
        ecs = [jnp.exp(tc - m) for tc, m in zip(tcs, ms)]
        ens = [jnp.exp(tn - m) for tn, m in zip(tns, ms)]
        invs = [1.0 / (jnp.exp(sk - m) + jnp.sum(ec, axis=-1, keepdims=True) + jnp.sum(en, axis=-1, keepdims=True))
                for sk, m, ec, en in zip(sks, ms, ecs, ens)]
        pcs = [ec * inv for ec, inv in zip(ecs, invs)]
        pns = [en * inv for en, inv in zip(ens, invs)]

        for n, (r0, valid_n, h, kch, vch) in enumerate(units):
            rows = pl.ds(r0, PAIR_ROWS)
            c0 = h * SWA_GROUP * SWA_HEAD_DIM
            grid16 = lambda ps: jnp.concatenate(
                [jnp.concatenate(ps[4 * n + 2 * p:4 * n + 2 * p + 2], axis=1) for p in range(2)], axis=0).astype(BF16)
            pc16 = grid16(pcs)
            pn16 = grid16(pns)
            o_e = [_dot(pc16, jnp.concatenate(vch[e], axis=0)) for e in range(2)]
            o = jnp.where(first16, o_e[0], o_e[1]) + _dot(pn16, vns_ref[h])
            attn_ref[rows, c0:c0 + LANES] = o[0:PAIR_ROWS]
            attn_ref[rows, c0 + LANES:c0 + 2 * LANES] = o[PAIR_ROWS:2 * PAIR_ROWS]
        return carry

    lax.fori_loop(0, SAMPLE_BB // (2 * SAMPLE_PAIRS_PER_ITER), pair_group, 0)


def _ignoring_refs(kernel_fn, start, count):
    if count == 0:
        return kernel_fn

    def wrapped(*refs):
        return kernel_fn(*refs[:start], *refs[start + count:])
    return wrapped


def _layer_slab_args(prev_outs, first_out_index, n_fixed_inputs):
    if prev_outs is None:
        return [], [], {}
    specs = [pl.BlockSpec(memory_space=pl.ANY) for _ in prev_outs]
    aliases = {n_fixed_inputs + k: first_out_index + k for k in range(len(prev_outs))}
    return list(prev_outs), specs, aliases


def _even_sample_b(q, kn, vn, ck_all, cv_all, layer, prev_outs, sinks, table_flat, *, steps):
    R = q.shape[0]
    row = lambda w: pl.BlockSpec((BLK, w), lambda i: (i, 0))
    cache = pl.BlockSpec((None, SAMPLE_BB, WINDOW, SWA_KV_WIDTH), lambda i: (layer, i, 0, 0))
    extra, extra_specs, aliases = _layer_slab_args(prev_outs, 1, 7)
    kern = _ignoring_refs(functools.partial(_even_sample_b_kernel, steps=steps, layer=layer), 7, len(extra))
    return pl.pallas_call(
        kern,
        grid=(R // BLK,),
        in_specs=[_smem_spec(), _smem_spec(), row(SWA_WIDTH), row(SWA_KV_WIDTH), row(SWA_KV_WIDTH), cache, cache]
        + extra_specs,
        out_specs=[row(SWA_WIDTH), cache, cache],
        out_shape=[
            jax.ShapeDtypeStruct((R, SWA_WIDTH), F32),
            jax.ShapeDtypeStruct(ck_all.shape, F32),
            jax.ShapeDtypeStruct(cv_all.shape, F32),
        ],
        input_output_aliases=aliases,
        scratch_shapes=[
            pltpu.VMEM((SWA_KV_HEADS, 2 * BLK, 2 * BLK), F32),
            pltpu.VMEM((SWA_KV_HEADS, 2 * BLK, LANES), BF16),
            pltpu.VMEM((SWA_KV_HEADS, 2 * BLK, LANES), BF16),
            pltpu.VMEM((SWA_HEADS, PAIR_ROWS, BLK), F32),
            pltpu.VMEM((SWA_HEADS, PAIR_ROWS, BLK), F32),
        ],
        compiler_params=pltpu.CompilerParams(
            dimension_semantics=("arbitrary",), vmem_limit_bytes=VMEM_LIMIT),
        name="even_sample_b",
    )(sinks, table_flat, q, kn, vn, ck_all, cv_all, *extra)


def _even_sample_c_kernel(x_ref, sga_ref, attn_ref, mixb_ref, wout_ref, y_ref):
    mixa = (sga_ref[...] * attn_ref[...]).astype(BF16)
    mix = jnp.concatenate([mixa, mixb_ref[...]], axis=1)
    y_ref[...] = x_ref[...] + _dot(mix, wout_ref[...])


def _even_sample_c(x, sga, attn, mixb, wout_all, e):
    R, D = x.shape
    row = lambda w: pl.BlockSpec((BLK, w), lambda i: (i, 0))
    return pl.pallas_call(
        _even_sample_c_kernel,
        grid=(R // BLK,),
        in_specs=[row(D), row(SWA_WIDTH), row(SWA_WIDTH), row(GMLP_WIDTH), _layer_spec((EVEN_MIX, D), e)],
        out_specs=row(D),
        out_shape=jax.ShapeDtypeStruct((R, D), F32),
        compiler_params=pltpu.CompilerParams(
            dimension_semantics=("arbitrary",), vmem_limit_bytes=VMEM_LIMIT),
        name="even_sample_c",
    )(x, sga, attn, mixb, wout_all)


def _odd_sample_a_kernel(x_ref, g_ref, wqk_ref, wv_ref, wg_ref, cos8_ref, sin8_ref,
                         q_ref, kdt_ref, v_ref, intra_ref, gate_ref, *, steps):
    x = x_ref[...]
    xn = _rmsnorm(x, g_ref[...]).astype(BF16)
    reps = BLK // cos8_ref.shape[0]
    cs = jnp.concatenate([cos8_ref[...]] * reps, axis=0)
    sn = jnp.concatenate([sin8_ref[...]] * reps, axis=0)
    rr = _iota((BLK, BLK), 0)
    cc = _iota((BLK, BLK), 1)
    mask = _same_batch_causal((BLK, BLK), steps=steps)
    diff = (rr % steps - cc % steps).astype(F32)
    step_col = (_iota((BLK, 1), 0) % steps).astype(F32)
    zqk = _dot(xn, wqk_ref[...])
    zv = _dot(xn, wv_ref[...])
    gate_ref[...] = _silu(_dot(xn, wg_ref[...]))
    v_ref[...] = zv.astype(BF16)
    for h in range(RET_HEADS):
        lg = _ret_log_gamma(h)
        qcols = slice(h * RET_KEY_DIM, (h + 1) * RET_KEY_DIM)
        kcols = slice(RET_QK_WIDTH + h * RET_KEY_DIM, RET_QK_WIDTH + (h + 1) * RET_KEY_DIM)
        vcols = slice(h * RET_VALUE_DIM, (h + 1) * RET_VALUE_DIM)
        qr = _xpos_rotate(zqk[:, qcols], cs, sn)
        kr = _xpos_rotate(zqk[:, kcols], cs, sn) * (RET_KEY_DIM ** -0.5)
        qb = qr.astype(BF16)
        decay = jnp.where(mask, jnp.exp(lg * jnp.maximum(diff, 0.0)), 0.0)
        att = _dot_nt(qb, kr.astype(BF16)) * decay
        intra_ref[:, vcols] = _dot(att.astype(BF16), zv[:, vcols].astype(BF16))
        kd = kr * jnp.exp(lg * (steps - 1.0 - step_col))
        q_ref[:, qcols] = qb
        kdt_ref[h] = kd.T.astype(BF16)


def _odd_sample_a(x, gains, layer, o, win_all, cos8, sin8, *, steps):
    R, D = x.shape
    row = lambda w: pl.BlockSpec((BLK, w), lambda i: (i, 0))
    kern = functools.partial(_odd_sample_a_kernel, steps=steps)
    once = dict(pipeline_mode=pl.Buffered(1))
    col_block = lambda c: pl.BlockSpec((None, D, RET_V_WIDTH), lambda i: (o, 0, c), **once)
    return pl.pallas_call(
        kern,
        grid=(R // BLK,),
        in_specs=[row(D), _layer_spec((1, D), layer),
                  col_block(0), col_block(1), col_block(2),
                  _const_spec(cos8.shape), _const_spec(sin8.shape)],
        out_specs=[row(RET_QK_WIDTH),
                   pl.BlockSpec((RET_HEADS, RET_KEY_DIM, BLK), lambda i: (0, 0, i)),
                   row(RET_V_WIDTH), row(RET_V_WIDTH), row(RET_V_WIDTH)],
        out_shape=[
            jax.ShapeDtypeStruct((R, RET_QK_WIDTH), BF16),
            jax.ShapeDtypeStruct((RET_HEADS, RET_KEY_DIM, R), BF16),
            jax.ShapeDtypeStruct((R, RET_V_WIDTH), BF16),
            jax.ShapeDtypeStruct((R, RET_V_WIDTH), F32),
            jax.ShapeDtypeStruct((R, RET_V_WIDTH), F32),
        ],
        compiler_params=pltpu.CompilerParams(
            dimension_semantics=("arbitrary",), vmem_limit_bytes=VMEM_LIMIT),
        name="odd_sample_a",
    )(x, gains, win_all, win_all, win_all, cos8, sin8)


STATE_BB = 4
STATE_HB = 2
STATE_HEAD_GROUPS = RET_HEADS // STATE_HB


def _retention_state_block(blk, hg, q_ref, kdt_ref, v_ref, s_ref, cross_ref, snew_ref, *, steps):
    rows16 = STATE_BB * steps
    lane_base = (blk % (BLK // rows16)) * rows16
    lane = _iota((RET_KEY_DIM, BLK), 1)
    row16 = _iota((rows16, 1), 0)
    v = v_ref[...]
    for hl in range(STATE_HB):
        lg = jnp.float32(_ret_log_gamma(hl))
        for g in range(1, STATE_HEAD_GROUPS):
            lg = jnp.where(hg == g, jnp.float32(_ret_log_gamma(g * STATE_HB + hl)), lg)
        qdec = jnp.exp(lg * ((row16 % steps).astype(F32) + 1.0))
        cdec = jnp.exp(jnp.full((1, 1), lg * steps, F32))
        q = q_ref[:, hl * RET_KEY_DIM:(hl + 1) * RET_KEY_DIM]
        kdt = kdt_ref[hl]
        vh = v[:, hl * RET_VALUE_DIM:(hl + 1) * RET_VALUE_DIM]
        cross = jnp.zeros((rows16, RET_VALUE_DIM), F32)
        for bl in range(STATE_BB):
            st = s_ref[bl, hl]
            cr = _dot(q, st.astype(BF16))
            cross = jnp.where(row16 // steps == bl, cr, cross)
            mine = (lane - lane_base) // steps == bl
            upd = _dot(jnp.where(mine, kdt, jnp.zeros_like(kdt)), vh)
            snew_ref[bl, hl] = cdec * st + upd
        cross_ref[:, hl * RET_VALUE_DIM:(hl + 1) * RET_VALUE_DIM] = cross * qdec


def _odd_sample_c_kernel(x_ref, intra_ref, cross_ref, gate_ref, wout_ref, fg_ref, y_ref, *, final_norm):
    gated = []
    for h in range(RET_HEADS):
        hs = slice(h * RET_VALUE_DIM, (h + 1) * RET_VALUE_DIM)
        on = _layernorm_nogain(intra_ref[:, hs] + cross_ref[:, hs])
        gated.append((gate_ref[:, hs] * on).astype(BF16))
    y = x_ref[...] + _dot(jnp.concatenate(gated, axis=1), wout_ref[...])
    if final_norm:
        y = _rmsnorm(y, fg_ref[...])
    y_ref[...] = y


def _odd_sample_c(x, intra, cross, gate, wout_all, o, fg, *, final_norm):
    R, D = x.shape
    row = lambda w: pl.BlockSpec((BLK, w), lambda i: (i, 0))
    kern = functools.partial(_odd_sample_c_kernel, final_norm=final_norm)
    return pl.pallas_call(
        kern,
        grid=(R // BLK,),
        in_specs=[row(D), row(RET_V_WIDTH), row(RET_V_WIDTH), row(RET_V_WIDTH),
                  _layer_spec((RET_V_WIDTH, D), o), _const_spec((1, D))],
        out_specs=row(D),
        out_shape=jax.ShapeDtypeStruct((R, D), F32),
        compiler_params=pltpu.CompilerParams(
            dimension_semantics=("arbitrary",), vmem_limit_bytes=VMEM_LIMIT),
        name="odd_sample_c",
    )(x, intra, cross, gate, wout_all, fg)


def _xpos_tables(pos):
    angle = 1.0 / (10000.0 ** jnp.linspace(0.0, 1.0, RET_KEY_DIM // 2, dtype=F32))
    ang = pos.astype(F32)[:, None] * angle[None, :]
    sin = jnp.sin(ang)
    cos_f = jnp.repeat(jnp.cos(ang), 2, axis=-1)
    sin_s = jnp.stack([-sin, sin], axis=-1).reshape(pos.shape[0], RET_KEY_DIM)
    return cos_f, sin_s


PROMPT_TB_EVEN = 256
PROMPT_TB_ODD = 512


def kernel(x_prompt, x_sample, cache_swa_k, cache_swa_v, state_ret, norm_gain, final_norm_gain,
           rel_bias_table, even_w_in, even_w_out, swa_sinks, gmlp_ws, gmlp_bs, gmlp_ln_gain,
           odd_w_in, odd_w_out):
    B, S, D = x_prompt.shape
    DB, T, _ = x_sample.shape
    R = DB * T
    n_even = even_w_in.shape[0]
    tb = PROMPT_TB_ODD

    table_flat = rel_bias_table.reshape(-1)
    gains = norm_gain.reshape(DEPTH, 1, D)
    fg = final_norm_gain.reshape(1, D)
    even_win = even_w_in.astype(BF16)
    even_wout = even_w_out.astype(BF16)
    odd_win = odd_w_in.astype(BF16)
    odd_wout = odd_w_out.astype(BF16)
    bst_all = gmlp_bs[:, :, :, None]
    lng_all = gmlp_ln_gain.reshape(n_even, 1, GMLP_WIDTH)
    ws4 = gmlp_ws[:, :, :T, :T].reshape(-1)
    bs4 = gmlp_bs[:, :, :T].reshape(-1)

    prompt_tables = (_xpos_tables(jnp.arange(0, S, tb, dtype=jnp.int32))
                     + _xpos_tables(jnp.arange(tb, dtype=jnp.int32)))
    cos4, sin4 = _xpos_tables(PAST_LEN + jnp.arange(T, dtype=jnp.int32))
    cos8 = jnp.tile(cos4, (PAIR_ROWS // T, 1))
    sin8 = jnp.tile(sin4, (PAIR_ROWS // T, 1))

    yp = x_prompt
    ys = x_sample.reshape(R, D)
    ck_all = cache_swa_k.reshape(n_even, DB, WINDOW, SWA_KV_WIDTH)
    cv_all = cache_swa_v.reshape(n_even, DB, WINDOW, SWA_KV_WIDTH)
    new_caches = None
    new_state = None
    prompt_state = None
    kp_l, vp_l, gv_l = [], [], []
    assert DEPTH % 2 == 0
    for pair in range(DEPTH // 2):
        even_layer, odd_layer = 2 * pair, 2 * pair + 1
        last = odd_layer == DEPTH - 1
        q, kn, vn_, sga, mixb, gv = _even_sample_a(ys, gains, even_layer, pair, even_win, ws4, bs4, lng_all,
                                                   steps=T)
        attn, nk, nv = _even_sample_b(q, kn, vn_, ck_all, cv_all, pair, new_caches, swa_sinks, table_flat, steps=T)
        new_caches = (nk, nv)
        ys = _even_sample_c(ys, sga, attn, mixb, even_wout, pair)
        gv_l.append(gv.reshape(DB, T, GMLP_WIDTH))
        q, kdt, v, intra, gate = _odd_sample_a(ys, gains, odd_layer, pair, odd_win, cos8, sin8, steps=T)
        yp, kp, vp, cross, new_state = _even_prompt(
            yp, gains, even_layer, pair, even_win, even_wout, swa_sinks, table_flat, gmlp_ws, bst_all, lng_all,
            (q, kdt, v, state_ret, pair, new_state), tb=PROMPT_TB_EVEN, steps=T)
        kp_l.append(kp.reshape(B, WINDOW, SWA_KV_HEADS, SWA_HEAD_DIM))
        vp_l.append(vp.reshape(B, WINDOW, SWA_KV_HEADS, SWA_HEAD_DIM))
        ys = _odd_sample_c(ys, intra, cross, gate, odd_wout, pair, fg, final_norm=last)
        yp, prompt_state = _odd_prompt(yp, gains, odd_layer, pair, odd_win, odd_wout, prompt_tables, fg,
                                       prompt_state, tb=tb, final_norm=last)
    cache_shape = (n_even, DB, WINDOW, SWA_KV_HEADS, SWA_HEAD_DIM)
    return (yp, ys.reshape(DB, T, D), jnp.stack(kp_l), jnp.stack(vp_l),
            new_caches[0].reshape(cache_shape), new_caches[1].reshape(cache_shape),
            prompt_state, new_state, jnp.stack(gv_l))
```

```python
import functools
import math

import jax
import jax.numpy as jnp
from jax import lax
from jax.experimental import pallas as pl
from jax.experimental.pallas import tpu as pltpu

D_MODEL = 1024
DEPTH = 4
PAST_LEN = 8192
NORM_EPS = 1e-6
NEG_INF = -1e30

SWA_HEADS = 8
SWA_KV_HEADS = 2
SWA_HEAD_DIM = 64
SWA_GROUP = SWA_HEADS // SWA_KV_HEADS
SWA_WIDTH = SWA_HEADS * SWA_HEAD_DIM
SWA_KV_WIDTH = SWA_KV_HEADS * SWA_HEAD_DIM
WINDOW = 128
REL_BUCKETS = 32
REL_MAX_DIST = 128

GMLP_GROUPS = 4
GMLP_CHUNK = 128
GMLP_WIDTH = D_MODEL // 2
GMLP_GROUP_DIM = GMLP_WIDTH // GMLP_GROUPS

E_Q = 0
E_K = E_Q + SWA_WIDTH
E_V = E_K + SWA_KV_WIDTH
E_GA = E_V + SWA_KV_WIDTH
E_U = E_GA + SWA_WIDTH
E_VB = E_U + GMLP_WIDTH
E_GB = E_VB + GMLP_WIDTH
EVEN_IN = E_GB + GMLP_WIDTH
EVEN_MIX = SWA_WIDTH + GMLP_WIDTH

RET_HEADS = 4
RET_KEY_DIM = 256
RET_VALUE_DIM = 512
RET_QK_WIDTH = RET_HEADS * RET_KEY_DIM
RET_V_WIDTH = RET_HEADS * RET_VALUE_DIM
RET_CHUNK = 128

LANES = 128
BLK = 128
VMEM_LIMIT = 56 * 1024 * 1024

F32 = jnp.float32
BF16 = jnp.bfloat16


def _ret_log_gamma(h):
    return math.log(1.0 - 2.0 ** (-5.0 - h))


def _dot(a, b):
    return jnp.dot(a, b, preferred_element_type=F32)


def _dot_nt(a, b):
    return lax.dot_general(a, b, (((1,), (1,)), ((), ())), preferred_element_type=F32)


def _dot_tn(a, b):
    return lax.dot_general(a, b, (((0,), (0,)), ((), ())), preferred_element_type=F32)


def _silu(x):
    return x * (1.0 / (1.0 + jnp.exp(-x)))


def _rmsnorm(x, g):
    ms = jnp.mean(x * x, axis=-1, keepdims=True)
    return x * lax.rsqrt(ms + NORM_EPS) * g


def _layernorm_nogain(x):
    mu = jnp.mean(x, axis=-1, keepdims=True)
    d = x - mu
    var = jnp.mean(d * d, axis=-1, keepdims=True)
    return d * lax.rsqrt(var + NORM_EPS)


def _iota(shape, dim):
    return lax.broadcasted_iota(jnp.int32, shape, dim)


def _t5_bias(dist, table_ref, head):
    n = jnp.maximum(dist, 0)
    max_exact = REL_BUCKETS // 2
    nf = jnp.maximum(n, 1).astype(F32)
    large = max_exact + (jnp.log(nf / max_exact) / math.log(REL_MAX_DIST / max_exact)
                         * (REL_BUCKETS - max_exact)).astype(jnp.int32)
    large = jnp.minimum(large, REL_BUCKETS - 1)
    bucket = jnp.where(n < max_exact, n, large)
    acc = jnp.zeros(dist.shape, F32)
    for b in range(REL_BUCKETS):
        acc = jnp.where(bucket >= b, table_ref[b * SWA_HEADS + head], acc)
    return acc


def _split_heads_kv(x):
    lo = _iota(x.shape, 1) < SWA_HEAD_DIM
    xr = pltpu.roll(x, SWA_HEAD_DIM, 1)
    a0 = jnp.where(lo, x, 0.0).astype(BF16)
    b0 = jnp.where(lo, 0.0, xr).astype(BF16)
    a1 = jnp.where(lo, xr, 0.0).astype(BF16)
    b1 = jnp.where(lo, 0.0, x).astype(BF16)
    return (a0, b0), (a1, b1)


def _xpos_rotate(x, cos_f, sin_s):
    even = (_iota((x.shape[0], LANES), 1) % 2) == 0
    outs = []
    for c in range(x.shape[1] // LANES):
        sl = slice(c * LANES, (c + 1) * LANES)
        xs = x[:, sl]
        partner = jnp.where(even, pltpu.roll(xs, LANES - 1, 1), pltpu.roll(xs, 1, 1))
        outs.append(xs * cos_f[:, sl] + partner * sin_s[:, sl])
    return jnp.concatenate(outs, axis=1)


def _layer_spec(shape, layer, **kwargs):
    nd = len(shape)
    return pl.BlockSpec((None,) + tuple(shape), lambda *_: (layer,) + (0,) * nd, **kwargs)


def _const_spec(shape):
    nd = len(shape)
    return pl.BlockSpec(shape, lambda *_: (0,) * nd)


def _smem_spec():
    return pl.BlockSpec(memory_space=pltpu.SMEM)


def _even_prompt_kernel(sinks_ref, table_ref, x_ref, g_ref, win_ref, wout_ref, ws_ref, bst_ref, lng_ref,
                        y_ref, newk_ref, newv_ref,
                        z_ref, zga_ref, zvb_ref, qb_ref, vn_ref, mix_ref, ks_ref, vs_ref, bias_ref, wm_ref,
                        *, tb, layer, side_work=None):
    b = pl.program_id(0)
    j = pl.program_id(1)
    nsub = tb // BLK

    @pl.when(jnp.logical_and(b == 0, j == 0))
    def _init_tables():
        qi = _iota((BLK, BLK), 0)
        c = _iota((BLK, BLK), 1)
        own = c <= qi
        dist = jnp.where(own, qi - c, qi + BLK - c)
        for h in range(SWA_HEADS):
            bias = _t5_bias(dist, table_ref, h)
            bias_ref[0, h] = bias
            bias_ref[1, h] = jnp.where(own, bias, NEG_INF)
        causal = _iota((BLK, BLK), 0) >= _iota((BLK, BLK), 1)
        for g in range(GMLP_GROUPS):
            wm_ref[g] = jnp.where(causal, ws_ref[g], 0.0).astype(BF16)

    @pl.when(j == 0)
    def _zero_prev():
        zeros = jnp.zeros((BLK, LANES), BF16)
        for h in range(SWA_KV_HEADS):
            for ab in range(2):
                ks_ref[h, ab, 0:BLK, :] = zeros
                vs_ref[h, ab, 0:BLK, :] = zeros

    if side_work is not None:
        side_work()

    x = x_ref[0]
    xn = _rmsnorm(x, g_ref[...]).astype(BF16)
    z_ref[...] = _dot(xn, win_ref[:, 0:E_GA])
    zga_ref[...] = _dot(xn, win_ref[:, E_GA:E_VB])
    qb_ref[...] = (z_ref[:, E_Q:E_Q + SWA_WIDTH] * (SWA_HEAD_DIM ** -0.5)).astype(BF16)
    kparts = _split_heads_kv(z_ref[:, E_K:E_K + SWA_KV_WIDTH])
    vparts = _split_heads_kv(z_ref[:, E_V:E_V + SWA_KV_WIDTH])
    for h in range(SWA_KV_HEADS):
        for ab in range(2):
            ks_ref[h, ab, BLK:BLK + tb, :] = kparts[h][ab]
            vs_ref[h, ab, BLK:BLK + tb, :] = vparts[h][ab]

    own = _iota((BLK, BLK), 1) <= _iota((BLK, BLK), 0)
    col0 = lambda h, p: h * SWA_GROUP * SWA_HEAD_DIM + p * LANES
    sub_rows = [slice(s * BLK, (s + 1) * BLK) for s in range(nsub)]
    units = []
    for s in range(nsub):
        first = jnp.where(j * nsub + s == 0, 1, 0)
        for h in range(SWA_KV_HEADS):
            for p in range(SWA_GROUP // 2):
                units.append((sub_rows[s], slice(s * BLK, (s + 2) * BLK), first, h, p))
    stacked = lambda ref, h, pc: jnp.concatenate([ref[h, 0, pc, :], ref[h, 1, pc, :]], axis=0)
    logits, sinks = [], []
    for rows, prev_cur, first, h, p in units:
        c0 = col0(h, p)
        sc = _dot_nt(qb_ref[rows, c0:c0 + LANES], stacked(ks_ref, h, prev_cur))
        for gi in range(2):
            hd = h * SWA_GROUP + p * 2 + gi
            s_prev = sc[:, (2 * gi) * BLK:(2 * gi + 1) * BLK]
            s_cur = sc[:, (2 * gi + 1) * BLK:(2 * gi + 2) * BLK]
            logits.append(jnp.where(own, s_cur, s_prev) + bias_ref[first, hd])
            sinks.append(sinks_ref[layer, hd])

    zvb_ref[...] = _dot(xn, win_ref[:, E_VB:EVEN_IN])

    maxes = [jnp.maximum(jnp.max(t, axis=-1, keepdims=True), sk) for t, sk in zip(logits, sinks)]
    exps = [jnp.exp(t - m) for t, m in zip(logits, maxes)]
    dens = [jnp.sum(e, axis=-1, keepdims=True) + jnp.exp(sk - m) for e, sk, m in zip(exps, sinks, maxes)]
    probs = [e * (1.0 / d) for e, d in zip(exps, dens)]

    zga_ref[:, 0:SWA_WIDTH] = _silu(zga_ref[:, 0:SWA_WIDTH])
    zvb_ref[:, GMLP_WIDTH:2 * GMLP_WIDTH] = _silu(zvb_ref[:, GMLP_WIDTH:2 * GMLP_WIDTH])
    vn_ref[...] = (_layernorm_nogain(zvb_ref[:, 0:GMLP_WIDTH]) * lng_ref[...]).astype(BF16)

    for n, (rows, prev_cur, first, h, p) in enumerate(units):
        c0 = col0(h, p)
        parts = []
        for pr in probs[2 * n:2 * n + 2]:
            parts.append(jnp.where(own, 0.0, pr).astype(BF16))
            parts.append(jnp.where(own, pr, 0.0).astype(BF16))
        o = _dot(jnp.concatenate(parts, axis=1), stacked(vs_ref, h, prev_cur))
        mix_ref[rows, c0:c0 + LANES] = (zga_ref[rows, c0:c0 + LANES] * o).astype(BF16)

    for rows in sub_rows:
        for g in range(GMLP_GROUPS):
            gs = slice(g * GMLP_GROUP_DIM, (g + 1) * GMLP_GROUP_DIM)
            sp = _dot(wm_ref[g], vn_ref[rows, gs]) + bst_ref[g]
            u = zga_ref[rows, SWA_WIDTH + g * GMLP_GROUP_DIM:SWA_WIDTH + (g + 1) * GMLP_GROUP_DIM]
            sgb = zvb_ref[rows, GMLP_WIDTH + g * GMLP_GROUP_DIM:GMLP_WIDTH + (g + 1) * GMLP_GROUP_DIM]
            mix_ref[rows, SWA_WIDTH + g * GMLP_GROUP_DIM:SWA_WIDTH + (g + 1) * GMLP_GROUP_DIM] = (
                sgb * (u * sp)).astype(BF16)

    for h in range(SWA_KV_HEADS):
        for ab in range(2):
            ks_ref[h, ab, 0:BLK, :] = ks_ref[h, ab, tb:tb + BLK, :]
            vs_ref[h, ab, 0:BLK, :] = vs_ref[h, ab, tb:tb + BLK, :]

    y_ref[0] = x + _dot(mix_ref[...], wout_ref[...])

    @pl.when(j == pl.num_programs(1) - 1)
    def _emit_cache():
        newk_ref[0] = z_ref[tb - WINDOW:tb, E_K:E_K + SWA_KV_WIDTH]
        newv_ref[0] = z_ref[tb - WINDOW:tb, E_V:E_V + SWA_KV_WIDTH]


N_EVEN_PROMPT_INPUTS = 9
N_STATE_INPUTS = 4


def _even_prompt_state_kernel(*refs, tb, layer, steps, n_alias):
    prompt_in = refs[:N_EVEN_PROMPT_INPUTS]
    state_in = refs[N_EVEN_PROMPT_INPUTS:N_EVEN_PROMPT_INPUTS + N_STATE_INPUTS]
    rest = refs[N_EVEN_PROMPT_INPUTS + N_STATE_INPUTS + n_alias:]
    y_ref, newk_ref, newv_ref, cross_ref, snew_ref = rest[:5]
    lin = pl.program_id(0) * pl.num_programs(1) + pl.program_id(1)
    state_work = functools.partial(_retention_state_block, lin // STATE_HEAD_GROUPS, lin % STATE_HEAD_GROUPS,
                                   *state_in, cross_ref, snew_ref, steps=steps)
    _even_prompt_kernel(*prompt_in, y_ref, newk_ref, newv_ref, *rest[5:], tb=tb, layer=layer,
                        side_work=state_work)


def _even_prompt(x, gains, layer, e, win_all, wout_all, sinks, table_flat, ws_all, bst_all, lng_all,
                 state_job, *, tb, steps):
    B, S, D = x.shape
    nj = S // tb
    q, kdt, v, state_all, o, prev_state = state_job
    R = q.shape[0]
    DB = state_all.shape[1]
    assert B * nj == (DB // STATE_BB) * STATE_HEAD_GROUPS, "one retention-state block per prompt grid step"
    rows16 = STATE_BB * steps
    per_lane_blk = BLK // rows16
    sblk = lambda b, j: (b * nj + j) // STATE_HEAD_GROUPS
    shg = lambda b, j: (b * nj + j) % STATE_HEAD_GROUPS
    st_spec = pl.BlockSpec((None, STATE_BB, STATE_HB, RET_KEY_DIM, RET_VALUE_DIM),
                           lambda b, j: (o, sblk(b, j), shg(b, j), 0, 0))
    extra, extra_specs, aliases = _layer_slab_args(None if prev_state is None else (prev_state,), 4,
                                                   N_EVEN_PROMPT_INPUTS + N_STATE_INPUTS)
    once = dict(pipeline_mode=pl.Buffered(1))
    row_block = lambda: pl.BlockSpec((1, tb, D), lambda b, j: (b, j, 0))
    kern = functools.partial(_even_prompt_state_kernel, tb=tb, layer=e, steps=steps, n_alias=len(extra))
    return pl.pallas_call(
        kern,
        grid=(B, nj),
        in_specs=[
            _smem_spec(), _smem_spec(),
            row_block(),
            _layer_spec((1, D), layer),
            _layer_spec((D, EVEN_IN), e, **once),
            _layer_spec((EVEN_MIX, D), e, **once),
            _layer_spec((GMLP_GROUPS, GMLP_CHUNK, GMLP_CHUNK), e),
            _layer_spec((GMLP_GROUPS, GMLP_CHUNK, 1), e),
            _layer_spec((1, GMLP_WIDTH), e),
            pl.BlockSpec((rows16, STATE_HB * RET_KEY_DIM), lambda b, j: (sblk(b, j), shg(b, j))),
            pl.BlockSpec((STATE_HB, RET_KEY_DIM, BLK), lambda b, j: (shg(b, j), 0, sblk(b, j) // per_lane_blk)),
            pl.BlockSpec((BLK, STATE_HB * RET_VALUE_DIM), lambda b, j: (sblk(b, j) // per_lane_blk, shg(b, j))),
            st_spec,
        ] + extra_specs,
        out_specs=[
            row_block(),
            pl.BlockSpec((1, WINDOW, SWA_KV_WIDTH), lambda b, j: (b, 0, 0)),
            pl.BlockSpec((1, WINDOW, SWA_KV_WIDTH), lambda b, j: (b, 0, 0)),
            pl.BlockSpec((rows16, STATE_HB * RET_VALUE_DIM), lambda b, j: (sblk(b, j), shg(b, j))),
            st_spec,
        ],
        out_shape=[
            jax.ShapeDtypeStruct((B, S, D), F32),
            jax.ShapeDtypeStruct((B, WINDOW, SWA_KV_WIDTH), F32),
            jax.ShapeDtypeStruct((B, WINDOW, SWA_KV_WIDTH), F32),
            jax.ShapeDtypeStruct((R, RET_V_WIDTH), F32),
            jax.ShapeDtypeStruct(state_all.shape, F32),
        ],
        input_output_aliases=aliases,
        scratch_shapes=[
            pltpu.VMEM((tb, E_GA), F32),
            pltpu.VMEM((tb, E_VB - E_GA), F32),
            pltpu.VMEM((tb, EVEN_IN - E_VB), F32),
            pltpu.VMEM((tb, SWA_WIDTH), BF16),
            pltpu.VMEM((tb, GMLP_WIDTH), BF16),
            pltpu.VMEM((tb, EVEN_MIX), BF16),
            pltpu.VMEM((SWA_KV_HEADS, 2, BLK + tb, LANES), BF16),
            pltpu.VMEM((SWA_KV_HEADS, 2, BLK + tb, LANES), BF16),
            pltpu.VMEM((2, SWA_HEADS, BLK, BLK), F32),
            pltpu.VMEM((GMLP_GROUPS, GMLP_CHUNK, GMLP_CHUNK), BF16),
        ],
        compiler_params=pltpu.CompilerParams(
            dimension_semantics=("arbitrary", "arbitrary"), vmem_limit_bytes=VMEM_LIMIT),
        name="even_prompt",
    )(sinks, table_flat, x, gains, win_all, wout_all, ws_all, bst_all, lng_all, q, kdt, v, state_all, *extra)


RET_PROMPT_CHUNK = 256


def _odd_prompt_kernel(x_ref, g_ref, wqk_ref, wv_ref, wg_ref, wout_ref, cblk_ref, sblk_ref, crow_ref, srow_ref,
                       fg_ref, y_ref, sout_ref,
                       xn_ref, zqk_ref, qb_ref, kb_ref, kd_ref, vb_ref, sg_ref, s_ref, gated_ref, cos_ref, sin_ref,
                       decay_ref, qdec_ref, kdec_ref, *, tb, final_norm):
    b = pl.program_id(0)
    j = pl.program_id(1)
    ch = RET_PROMPT_CHUNK
    nchunk = tb // ch

    @pl.when(jnp.logical_and(b == 0, j == 0))
    def _init_tables():
        ii = _iota((ch, ch), 0)
        jj = _iota((ch, ch), 1)
        diff = (ii - jj).astype(F32)
        idx = _iota((ch, 1), 0).astype(F32)
        idx_blk = (_iota((tb, 1), 0) % ch).astype(F32)
        for h in range(RET_HEADS):
            lg = _ret_log_gamma(h)
            decay_ref[h] = jnp.where(diff >= 0, jnp.exp(lg * jnp.maximum(diff, 0.0)), 0.0)
            qdec_ref[h] = jnp.exp(lg * (idx + 1.0))
            kdec_ref[h] = jnp.exp(lg * (ch - 1.0 - idx_blk))

    @pl.when(j == 0)
    def _zero_state():
        s_ref[...] = jnp.zeros(s_ref.shape, F32)

    cb = cblk_ref[pl.ds(j, 1), :]
    sb = sblk_ref[pl.ds(j, 1), :]
    cos_ref[...] = cb * crow_ref[...] - sb * srow_ref[...]
    sin_ref[...] = sb * crow_ref[...] + cb * srow_ref[...]

    heads = range(RET_HEADS)
    hrows = tb // 2
    xn_ref[0:hrows, :] = _rmsnorm(x_ref[0, 0:hrows, :], g_ref[...]).astype(BF16)
    zqk_ref[0:hrows, :] = _dot(xn_ref[0:hrows, :], wqk_ref[...])
    xn_ref[hrows:tb, :] = _rmsnorm(x_ref[0, hrows:tb, :], g_ref[...]).astype(BF16)
    zqk_ref[hrows:tb, :] = _dot(xn_ref[hrows:tb, :], wqk_ref[...])
    sg_ref[...] = _silu(_dot(xn_ref[...], wg_ref[...]))
    vb_ref[...] = _dot(xn_ref[...], wv_ref[...]).astype(BF16)
    cs = cos_ref[...]
    sn = sin_ref[...]
    for h in heads:
        qcols = slice(h * RET_KEY_DIM, (h + 1) * RET_KEY_DIM)
        kcols = slice(RET_QK_WIDTH + h * RET_KEY_DIM, RET_QK_WIDTH + (h + 1) * RET_KEY_DIM)
        qb_ref[:, qcols] = _xpos_rotate(zqk_ref[:, qcols], cs, sn).astype(BF16)
        kr = _xpos_rotate(zqk_ref[:, kcols], cs, sn) * (RET_KEY_DIM ** -0.5)
        kb_ref[:, qcols] = kr.astype(BF16)
        kd_ref[:, qcols] = (kr * kdec_ref[h]).astype(BF16)

    qcols = [slice(h * RET_KEY_DIM, (h + 1) * RET_KEY_DIM) for h in heads]
    vcols = [slice(h * RET_VALUE_DIM, (h + 1) * RET_VALUE_DIM) for h in heads]
    o_ref = zqk_ref

    def recurrence(c):
        rows = slice(c * ch, (c + 1) * ch)
        qb = [qb_ref[rows, qcols[h]] for h in heads]
        vb = [vb_ref[rows, vcols[h]] for h in heads]
        att = [(_dot_nt(qb[h], kb_ref[rows, qcols[h]]) * decay_ref[h]).astype(BF16) for h in heads]
        cross = [_dot(qb[h], s_ref[h].astype(BF16)) * qdec_ref[h] for h in heads]
        for h in heads:
            o_ref[rows, vcols[h]] = _dot(att[h], vb[h]) + cross[h]
        for h in heads:
            s_ref[h] = (math.exp(_ret_log_gamma(h) * ch) * s_ref[h]
                        + _dot_tn(kd_ref[rows, qcols[h]], vb[h]))

    def norm_gate(c, hs):
        tiles = [(slice(r, r + BLK), vcols[h]) for h in hs for r in range(c * ch, (c + 1) * ch, BLK)]
        normed = [_layernorm_nogain(o_ref[r, v]) for r, v in tiles]
        for (r, v), on in zip(tiles, normed):
            gated_ref[r, v] = (sg_ref[r, v] * on).astype(BF16)

    for c in range(nchunk):
        recurrence(c)
        if c > 0:
            norm_gate(c - 1, heads)
    y = x_ref[0]
    for h in heads:
        norm_gate(nchunk - 1, [h])
        y = y + _dot(gated_ref[:, vcols[h]], wout_ref[vcols[h], :])
    if final_norm:
        y = _rmsnorm(y, fg_ref[...])
    y_ref[0] = y

    @pl.when(j == pl.num_programs(1) - 1)
    def _emit_state():
        sout_ref[0] = s_ref[...]


N_ODD_PROMPT_INPUTS = 11


def _odd_prompt(x, gains, layer, o, win_all, wout_all, tables, fg, prev_state, *, tb, final_norm):
    B, S, D = x.shape
    n_odd = win_all.shape[0]
    cblk, sblk, crow, srow = tables
    extra, extra_specs, aliases = _layer_slab_args(None if prev_state is None else (prev_state,), 1,
                                                   N_ODD_PROMPT_INPUTS)
    kern = _ignoring_refs(functools.partial(_odd_prompt_kernel, tb=tb, final_norm=final_norm),
                          N_ODD_PROMPT_INPUTS, len(extra))
    once = dict(pipeline_mode=pl.Buffered(1))
    row_block = lambda: pl.BlockSpec((1, tb, D), lambda b, j: (b, j, 0))
    col_block = lambda c: pl.BlockSpec((None, D, RET_V_WIDTH), lambda b, j: (o, 0, c), **once)
    return pl.pallas_call(
        kern,
        grid=(B, S // tb),
        in_specs=[
            row_block(),
            _layer_spec((1, D), layer),
            col_block(0), col_block(1), col_block(2),
            _layer_spec((RET_V_WIDTH, D), o, **once),
            _const_spec((S // tb, RET_KEY_DIM)), _const_spec((S // tb, RET_KEY_DIM)),
            _const_spec((tb, RET_KEY_DIM)), _const_spec((tb, RET_KEY_DIM)),
            _const_spec((1, D)),
        ] + extra_specs,
        out_specs=[
            row_block(),
            pl.BlockSpec((None, 1, RET_HEADS, RET_KEY_DIM, RET_VALUE_DIM), lambda b, j: (o, b, 0, 0, 0)),
        ],
        out_shape=[
            jax.ShapeDtypeStruct((B, S, D), F32),
            jax.ShapeDtypeStruct((n_odd, B, RET_HEADS, RET_KEY_DIM, RET_VALUE_DIM), F32),
        ],
        input_output_aliases=aliases,
        scratch_shapes=[
            pltpu.VMEM((tb, D), BF16),
            pltpu.VMEM((tb, 2 * RET_QK_WIDTH), F32),
            pltpu.VMEM((tb, RET_QK_WIDTH), BF16),
            pltpu.VMEM((tb, RET_QK_WIDTH), BF16),
            pltpu.VMEM((tb, RET_QK_WIDTH), BF16),
            pltpu.VMEM((tb, RET_V_WIDTH), BF16),
            pltpu.VMEM((tb, RET_V_WIDTH), F32),
            pltpu.VMEM((RET_HEADS, RET_KEY_DIM, RET_VALUE_DIM), F32),
            pltpu.VMEM((tb, RET_V_WIDTH), BF16),
            pltpu.VMEM((tb, RET_KEY_DIM), F32),
            pltpu.VMEM((tb, RET_KEY_DIM), F32),
            pltpu.VMEM((RET_HEADS, RET_PROMPT_CHUNK, RET_PROMPT_CHUNK), F32),
            pltpu.VMEM((RET_HEADS, RET_PROMPT_CHUNK, 1), F32),
            pltpu.VMEM((RET_HEADS, tb, 1), F32),
        ],
        compiler_params=pltpu.CompilerParams(
            dimension_semantics=("arbitrary", "arbitrary"), vmem_limit_bytes=VMEM_LIMIT),
        name="odd_prompt",
    )(x, gains, win_all, win_all, win_all, wout_all, cblk, sblk, crow, srow, fg, *extra)


def _same_batch_causal(shape, row0=0, col0=0, steps=4):
    r = _iota(shape, 0) + row0
    c = _iota(shape, 1) + col0
    return (r // steps == c // steps) & (c % steps <= r % steps)


def _even_sample_a_kernel(ws4_ref, bs4_ref, x_ref, g_ref, win_ref, lng_ref,
                          q_ref, k_ref, v_ref, sga_ref, mixb_ref, vn_ref, *, steps, layer):
    x = x_ref[...]
    xn = _rmsnorm(x, g_ref[...]).astype(BF16)
    z = _dot(xn, win_ref[...])
    q_ref[...] = z[:, E_Q:E_Q + SWA_WIDTH] * (SWA_HEAD_DIM ** -0.5)
    k_ref[...] = z[:, E_K:E_K + SWA_KV_WIDTH]
    v_ref[...] = z[:, E_V:E_V + SWA_KV_WIDTH]
    sga_ref[...] = _silu(z[:, E_GA:E_GA + SWA_WIDTH])
    vn = _layernorm_nogain(z[:, E_VB:E_VB + GMLP_WIDTH]) * lng_ref[...]
    vn_ref[...] = vn
    vnb = vn.astype(BF16)
    same = _iota((BLK, BLK), 0) // steps == _iota((BLK, BLK), 1) // steps
    rstep = _iota((BLK, BLK), 0) % steps
    cstep = _iota((BLK, BLK), 1) % steps
    rstep_col = _iota((BLK, 1), 0) % steps
    for g in range(GMLP_GROUPS):
        gs = slice(g * GMLP_GROUP_DIM, (g + 1) * GMLP_GROUP_DIM)
        wk = jnp.zeros((BLK, BLK), F32)
        bcol = jnp.zeros((BLK, 1), F32)
        for p in range(steps):
            bcol = jnp.where(rstep_col == p, bs4_ref[(layer * GMLP_GROUPS + g) * steps + p], bcol)
            for q in range(p + 1):
                w = ws4_ref[((layer * GMLP_GROUPS + g) * steps + p) * steps + q]
                wk = jnp.where(same & (rstep == p) & (cstep == q), w, wk)
        sp = _dot(wk.astype(BF16), vnb[:, gs]) + bcol
        u = z[:, E_U + g * GMLP_GROUP_DIM:E_U + (g + 1) * GMLP_GROUP_DIM]
        gb = z[:, E_GB + g * GMLP_GROUP_DIM:E_GB + (g + 1) * GMLP_GROUP_DIM]
        mixb_ref[:, gs] = (_silu(gb) * (u * sp)).astype(BF16)


def _even_sample_a(x, gains, layer, e, win_all, ws4, bs4, lng_all, *, steps):
    R, D = x.shape
    row = lambda w: pl.BlockSpec((BLK, w), lambda i: (i, 0))
    kern = functools.partial(_even_sample_a_kernel, steps=steps, layer=e)
    return pl.pallas_call(
        kern,
        grid=(R // BLK,),
        in_specs=[_smem_spec(), _smem_spec(), row(D), _layer_spec((1, D), layer), _layer_spec((D, EVEN_IN), e),
                  _layer_spec((1, GMLP_WIDTH), e)],
        out_specs=[row(SWA_WIDTH), row(SWA_KV_WIDTH), row(SWA_KV_WIDTH), row(SWA_WIDTH),
                   row(GMLP_WIDTH), row(GMLP_WIDTH)],
        out_shape=[
            jax.ShapeDtypeStruct((R, SWA_WIDTH), F32),
            jax.ShapeDtypeStruct((R, SWA_KV_WIDTH), F32),
            jax.ShapeDtypeStruct((R, SWA_KV_WIDTH), F32),
            jax.ShapeDtypeStruct((R, SWA_WIDTH), F32),
            jax.ShapeDtypeStruct((R, GMLP_WIDTH), BF16),
            jax.ShapeDtypeStruct((R, GMLP_WIDTH), F32),
        ],
        compiler_params=pltpu.CompilerParams(
            dimension_semantics=("arbitrary",), vmem_limit_bytes=VMEM_LIMIT),
        name="even_sample_a",
    )(ws4, bs4, x, gains, win_all, lng_all)


SAMPLE_BB = 32
PAIR_ROWS = 8
SAMPLE_PAIRS_PER_ITER = 2


def _even_sample_b_kernel(sinks_ref, table_ref, q_ref, kn_ref, vn_ref, ck_ref, cv_ref,
                          attn_ref, newk_ref, newv_ref,
                          sn_ref, kns_ref, vns_ref, biasc_ref, biasn_ref, *, steps, layer):
    kparts = _split_heads_kv(kn_ref[...])
    vparts = _split_heads_kv(vn_ref[...])
    for h in range(SWA_KV_HEADS):
        kns_ref[h] = jnp.concatenate(kparts[h], axis=0)
        vns_ref[h] = jnp.concatenate(vparts[h], axis=0)
        c0 = h * SWA_GROUP * SWA_HEAD_DIM
        lhs = jnp.concatenate([q_ref[:, c0:c0 + LANES], q_ref[:, c0 + LANES:c0 + 2 * LANES]], axis=0)
        sn_ref[h] = _dot_nt(lhs.astype(BF16), kns_ref[h])

    step_r = _iota((PAIR_ROWS, BLK), 0) % steps
    lane = _iota((PAIR_ROWS, BLK), 1)
    dist_c = step_r + WINDOW - lane
    valid_c = (dist_c >= 0) & (dist_c < WINDOW)
    dist_n = step_r - lane % steps
    first_local = _iota((PAIR_ROWS, 1), 0) < steps
    for hd in range(SWA_HEADS):
        biasc_ref[hd] = _t5_bias(dist_c, table_ref, hd)
        biasn_ref[hd] = _t5_bias(dist_n, table_ref, hd)

    first16 = jnp.concatenate([first_local, first_local], axis=0)

    def pair_group(it, carry):
        units = []
        for k in range(SAMPLE_PAIRS_PER_ITER):
            i = it * SAMPLE_PAIRS_PER_ITER + k
            r0 = pl.multiple_of(i * PAIR_ROWS, PAIR_ROWS)
            valid_n = _same_batch_causal((PAIR_ROWS, BLK), row0=r0, steps=steps)
            kc = [_split_heads_kv(ck_ref[2 * i + e]) for e in range(2)]
            vc = [_split_heads_kv(cv_ref[2 * i + e]) for e in range(2)]
            for h in range(SWA_KV_HEADS):
                units.append((r0, valid_n, h, [kc[e][h] for e in range(2)], [vc[e][h] for e in range(2)]))
            for e in range(2):
                be = 2 * i + e
                newk_ref[be, 0:WINDOW - steps, :] = ck_ref[be, steps:WINDOW, :]
                newv_ref[be, 0:WINDOW - steps, :] = cv_ref[be, steps:WINDOW, :]
                newk_ref[be, WINDOW - steps:WINDOW, :] = kn_ref[pl.ds(r0 + e * steps, steps), :]
                newv_ref[be, WINDOW - steps:WINDOW, :] = vn_ref[pl.ds(r0 + e * steps, steps), :]

        scores = []
        for r0, valid_n, h, kch, vch in units:
            rows = pl.ds(r0, PAIR_ROWS)
            c0 = h * SWA_GROUP * SWA_HEAD_DIM
            lhs = jnp.concatenate([q_ref[rows, c0:c0 + LANES], q_ref[rows, c0 + LANES:c0 + 2 * LANES]],
                                  axis=0).astype(BF16)
            sc_e = [_dot_nt(lhs, jnp.concatenate(kch[e], axis=0)) for e in range(2)]
            scores.append(jnp.where(first16, sc_e[0], sc_e[1]))

        tcs, tns, sks = [], [], []
        for (r0, valid_n, h, kch, vch), sc in zip(units, scores):
            for p in range(2):
                pr = slice(p * PAIR_ROWS, (p + 1) * PAIR_ROWS)
                for gi in range(2):
                    hd = h * SWA_GROUP + p * 2 + gi
                    gc = slice(gi * BLK, (gi + 1) * BLK)
                    tcs.append(jnp.where(valid_c, sc[pr, gc] + biasc_ref[hd], NEG_INF))
                    tn_raw = sn_ref[h, pl.ds(p * BLK + r0, PAIR_ROWS), gc]
                    tns.append(jnp.where(valid_n, tn_raw + biasn_ref[hd], NEG_INF))
                    sks.append(sinks_ref[layer, hd])
        ms = [jnp.maximum(jnp.maximum(jnp.max(tc, axis=-1, keepdims=True), jnp.max(tn, axis=-1, keepdims=True)), sk)
              for tc, tn, sk in zip(tcs, tns, sks)]
        ecs = [jnp.exp(tc - m) for tc, m in zip(tcs, ms)]
        ens = [jnp.exp(tn - m) for tn, m in zip(tns, ms)]
        invs = [1.0 / (jnp.exp(sk - m) + jnp.sum(ec, axis=-1, keepdims=True) + jnp.sum(en, axis=-1, keepdims=True))
                for sk, m, ec, en in zip(sks, ms, ecs, ens)]
        pcs = [ec * inv for ec, inv in zip(ecs, invs)]
        pns = [en * inv for en, inv in zip(ens, invs)]

        for n, (r0, valid_n, h, kch, vch) in enumerate(units):
            rows = pl.ds(r0, PAIR_ROWS)
            c0 = h * SWA_GROUP * SWA_HEAD_DIM
            grid16 = lambda ps: jnp.concatenate(
                [jnp.concatenate(ps[4 * n + 2 * p:4 * n + 2 * p + 2], axis=1) for p in range(2)], axis=0).astype(BF16)
            pc16 = grid16(pcs)
            pn16 = grid16(pns)
            o_e = [_dot(pc16, jnp.concatenate(vch[e], axis=0)) for e in range(2)]
            o = jnp.where(first16, o_e[0], o_e[1]) + _dot(pn16, vns_ref[h])
            attn_ref[rows, c0:c0 + LANES] = o[0:PAIR_ROWS]
            attn_ref[rows, c0 + LANES:c0 + 2 * LANES] = o[PAIR_ROWS:2 * PAIR_ROWS]
        return carry

    lax.fori_loop(0, SAMPLE_BB // (2 * SAMPLE_PAIRS_PER_ITER), pair_group, 0)


def _ignoring_refs(kernel_fn, start, count):
    if count == 0:
        return kernel_fn

    def wrapped(*refs):
        return kernel_fn(*refs[:start], *refs[start + count:])
    return wrapped


def _layer_slab_args(prev_outs, first_out_index, n_fixed_inputs):
    if prev_outs is None:
        return [], [], {}
    specs = [pl.BlockSpec(memory_space=pl.ANY) for _ in prev_outs]
    aliases = {n_fixed_inputs + k: first_out_index + k for k in range(len(prev_outs))}
    return list(prev_outs), specs, aliases


def _even_sample_b(q, kn, vn, ck_all, cv_all, layer, prev_outs, sinks, table_flat, *, steps):
    R = q.shape[0]
    row = lambda w: pl.BlockSpec((BLK, w), lambda i: (i, 0))
    cache = pl.BlockSpec((None, SAMPLE_BB, WINDOW, SWA_KV_WIDTH), lambda i: (layer, i, 0, 0))
    extra, extra_specs, aliases = _layer_slab_args(prev_outs, 1, 7)
    kern = _ignoring_refs(functools.partial(_even_sample_b_kernel, steps=steps, layer=layer), 7, len(extra))
    return pl.pallas_call(
        kern,
        grid=(R // BLK,),
        in_specs=[_smem_spec(), _smem_spec(), row(SWA_WIDTH), row(SWA_KV_WIDTH), row(SWA_KV_WIDTH), cache, cache]
        + extra_specs,
        out_specs=[row(SWA_WIDTH), cache, cache],
        out_shape=[
            jax.ShapeDtypeStruct((R, SWA_WIDTH), F32),
            jax.ShapeDtypeStruct(ck_all.shape, F32),
            jax.ShapeDtypeStruct(cv_all.shape, F32),
        ],
        input_output_aliases=aliases,
        scratch_shapes=[
            pltpu.VMEM((SWA_KV_HEADS, 2 * BLK, 2 * BLK), F32),
            pltpu.VMEM((SWA_KV_HEADS, 2 * BLK, LANES), BF16),
            pltpu.VMEM((SWA_KV_HEADS, 2 * BLK, LANES), BF16),
            pltpu.VMEM((SWA_HEADS, PAIR_ROWS, BLK), F32),
            pltpu.VMEM((SWA_HEADS, PAIR_ROWS, BLK), F32),
        ],
        compiler_params=pltpu.CompilerParams(
            dimension_semantics=("arbitrary",), vmem_limit_bytes=VMEM_LIMIT),
        name="even_sample_b",
    )(sinks, table_flat, q, kn, vn, ck_all, cv_all, *extra)


def _even_sample_c_kernel(x_ref, sga_ref, attn_ref, mixb_ref, wout_ref, y_ref):
    mixa = (sga_ref[...] * attn_ref[...]).astype(BF16)
    mix = jnp.concatenate([mixa, mixb_ref[...]], axis=1)
    y_ref[...] = x_ref[...] + _dot(mix, wout_ref[...])


def _even_sample_c(x, sga, attn, mixb, wout_all, e):
    R, D = x.shape
    row = lambda w: pl.BlockSpec((BLK, w), lambda i: (i, 0))
    return pl.pallas_call(
        _even_sample_c_kernel,
        grid=(R // BLK,),
        in_specs=[row(D), row(SWA_WIDTH), row(SWA_WIDTH), row(GMLP_WIDTH), _layer_spec((EVEN_MIX, D), e)],
        out_specs=row(D),
        out_shape=jax.ShapeDtypeStruct((R, D), F32),
        compiler_params=pltpu.CompilerParams(
            dimension_semantics=("arbitrary",), vmem_limit_bytes=VMEM_LIMIT),
        name="even_sample_c",
    )(x, sga, attn, mixb, wout_all)


def _odd_sample_a_kernel(x_ref, g_ref, wqk_ref, wv_ref, wg_ref, cos8_ref, sin8_ref,
                         q_ref, kdt_ref, v_ref, intra_ref, gate_ref, *, steps):
    x = x_ref[...]
    xn = _rmsnorm(x, g_ref[...]).astype(BF16)
    reps = BLK // cos8_ref.shape[0]
    cs = jnp.concatenate([cos8_ref[...]] * reps, axis=0)
    sn = jnp.concatenate([sin8_ref[...]] * reps, axis=0)
    rr = _iota((BLK, BLK), 0)
    cc = _iota((BLK, BLK), 1)
    mask = _same_batch_causal((BLK, BLK), steps=steps)
    diff = (rr % steps - cc % steps).astype(F32)
    step_col = (_iota((BLK, 1), 0) % steps).astype(F32)
    zqk = _dot(xn, wqk_ref[...])
    zv = _dot(xn, wv_ref[...])
    gate_ref[...] = _silu(_dot(xn, wg_ref[...]))
    v_ref[...] = zv.astype(BF16)
    for h in range(RET_HEADS):
        lg = _ret_log_gamma(h)
        qcols = slice(h * RET_KEY_DIM, (h + 1) * RET_KEY_DIM)
        kcols = slice(RET_QK_WIDTH + h * RET_KEY_DIM, RET_QK_WIDTH + (h + 1) * RET_KEY_DIM)
        vcols = slice(h * RET_VALUE_DIM, (h + 1) * RET_VALUE_DIM)
        qr = _xpos_rotate(zqk[:, qcols], cs, sn)
        kr = _xpos_rotate(zqk[:, kcols], cs, sn) * (RET_KEY_DIM ** -0.5)
        qb = qr.astype(BF16)
        decay = jnp.where(mask, jnp.exp(lg * jnp.maximum(diff, 0.0)), 0.0)
        att = _dot_nt(qb, kr.astype(BF16)) * decay
        intra_ref[:, vcols] = _dot(att.astype(BF16), zv[:, vcols].astype(BF16))
        kd = kr * jnp.exp(lg * (steps - 1.0 - step_col))
        q_ref[:, qcols] = qb
        kdt_ref[h] = kd.T.astype(BF16)


def _odd_sample_a(x, gains, layer, o, win_all, cos8, sin8, *, steps):
    R, D = x.shape
    row = lambda w: pl.BlockSpec((BLK, w), lambda i: (i, 0))
    kern = functools.partial(_odd_sample_a_kernel, steps=steps)
    once = dict(pipeline_mode=pl.Buffered(1))
    col_block = lambda c: pl.BlockSpec((None, D, RET_V_WIDTH), lambda i: (o, 0, c), **once)
    return pl.pallas_call(
        kern,
        grid=(R // BLK,),
        in_specs=[row(D), _layer_spec((1, D), layer),
                  col_block(0), col_block(1), col_block(2),
                  _const_spec(cos8.shape), _const_spec(sin8.shape)],
        out_specs=[row(RET_QK_WIDTH),
                   pl.BlockSpec((RET_HEADS, RET_KEY_DIM, BLK), lambda i: (0, 0, i)),
                   row(RET_V_WIDTH), row(RET_V_WIDTH), row(RET_V_WIDTH)],
        out_shape=[
            jax.ShapeDtypeStruct((R, RET_QK_WIDTH), BF16),
            jax.ShapeDtypeStruct((RET_HEADS, RET_KEY_DIM, R), BF16),
            jax.ShapeDtypeStruct((R, RET_V_WIDTH), BF16),
            jax.ShapeDtypeStruct((R, RET_V_WIDTH), F32),
            jax.ShapeDtypeStruct((R, RET_V_WIDTH), F32),
        ],
        compiler_params=pltpu.CompilerParams(
            dimension_semantics=("arbitrary",), vmem_limit_bytes=VMEM_LIMIT),
        name="odd_sample_a",
    )(x, gains, win_all, win_all, win_all, cos8, sin8)


STATE_BB = 4
STATE_HB = 2
STATE_HEAD_GROUPS = RET_HEADS // STATE_HB


def _retention_state_block(blk, hg, q_ref, kdt_ref, v_ref, s_ref, cross_ref, snew_ref, *, steps):
    rows16 = STATE_BB * steps
    lane_base = (blk % (BLK // rows16)) * rows16
    lane = _iota((RET_KEY_DIM, BLK), 1)
    row16 = _iota((rows16, 1), 0)
    v = v_ref[...]
    for hl in range(STATE_HB):
        lg = jnp.float32(_ret_log_gamma(hl))
        for g in range(1, STATE_HEAD_GROUPS):
            lg = jnp.where(hg == g, jnp.float32(_ret_log_gamma(g * STATE_HB + hl)), lg)
        qdec = jnp.exp(lg * ((row16 % steps).astype(F32) + 1.0))
        cdec = jnp.exp(jnp.full((1, 1), lg * steps, F32))
        q = q_ref[:, hl * RET_KEY_DIM:(hl + 1) * RET_KEY_DIM]
        kdt = kdt_ref[hl]
        vh = v[:, hl * RET_VALUE_DIM:(hl + 1) * RET_VALUE_DIM]
        cross = jnp.zeros((rows16, RET_VALUE_DIM), F32)
        for bl in range(STATE_BB):
            st = s_ref[bl, hl]
            cr = _dot(q, st.astype(BF16))
            cross = jnp.where(row16 // steps == bl, cr, cross)
            mine = (lane - lane_base) // steps == bl
            upd = _dot(jnp.where(mine, kdt, jnp.zeros_like(kdt)), vh)
            snew_ref[bl, hl] = cdec * st + upd
        cross_ref[:, hl * RET_VALUE_DIM:(hl + 1) * RET_VALUE_DIM] = cross * qdec


def _odd_sample_c_kernel(x_ref, intra_ref, cross_ref, gate_ref, wout_ref, fg_ref, y_ref, *, final_norm):
    gated = []
    for h in range(RET_HEADS):
        hs = slice(h * RET_VALUE_DIM, (h + 1) * RET_VALUE_DIM)
        on = _layernorm_nogain(intra_ref[:, hs] + cross_ref[:, hs])
        gated.append((gate_ref[:, hs] * on).astype(BF16))
    y = x_ref[...] + _dot(jnp.concatenate(gated, axis=1), wout_ref[...])
    if final_norm:
        y = _rmsnorm(y, fg_ref[...])
    y_ref[...] = y


def _odd_sample_c(x, intra, cross, gate, wout_all, o, fg, *, final_norm):
    R, D = x.shape
    row = lambda w: pl.BlockSpec((BLK, w), lambda i: (i, 0))
    kern = functools.partial(_odd_sample_c_kernel, final_norm=final_norm)
    return pl.pallas_call(
        kern,
        grid=(R // BLK,),
        in_specs=[row(D), row(RET_V_WIDTH), row(RET_V_WIDTH), row(RET_V_WIDTH),
                  _layer_spec((RET_V_WIDTH, D), o), _const_spec((1, D))],
        out_specs=row(D),
        out_shape=jax.ShapeDtypeStruct((R, D), F32),
        compiler_params=pltpu.CompilerParams(
            dimension_semantics=("arbitrary",), vmem_limit_bytes=VMEM_LIMIT),
        name="odd_sample_c",
    )(x, intra, cross, gate, wout_all, fg)


def _xpos_tables(pos):
    angle = 1.0 / (10000.0 ** jnp.linspace(0.0, 1.0, RET_KEY_DIM // 2, dtype=F32))
    ang = pos.astype(F32)[:, None] * angle[None, :]
    sin = jnp.sin(ang)
    cos_f = jnp.repeat(jnp.cos(ang), 2, axis=-1)
    sin_s = jnp.stack([-sin, sin], axis=-1).reshape(pos.shape[0], RET_KEY_DIM)
    return cos_f, sin_s


PROMPT_TB_EVEN = 256
PROMPT_TB_ODD = 512


def kernel(x_prompt, x_sample, cache_swa_k, cache_swa_v, state_ret, norm_gain, final_norm_gain,
           rel_bias_table, even_w_in, even_w_out, swa_sinks, gmlp_ws, gmlp_bs, gmlp_ln_gain,
           odd_w_in, odd_w_out):
    B, S, D = x_prompt.shape
    DB, T, _ = x_sample.shape
    R = DB * T
    n_even = even_w_in.shape[0]
    tb = PROMPT_TB_ODD

    table_flat = rel_bias_table.reshape(-1)
    gains = norm_gain.reshape(DEPTH, 1, D)
    fg = final_norm_gain.reshape(1, D)
    even_win = even_w_in.astype(BF16)
    even_wout = even_w_out.astype(BF16)
    odd_win = odd_w_in.astype(BF16)
    odd_wout = odd_w_out.astype(BF16)
    bst_all = gmlp_bs[:, :, :, None]
    lng_all = gmlp_ln_gain.reshape(n_even, 1, GMLP_WIDTH)
    ws4 = gmlp_ws[:, :, :T, :T].reshape(-1)
    bs4 = gmlp_bs[:, :, :T].reshape(-1)

    prompt_tables = (_xpos_tables(jnp.arange(0, S, tb, dtype=jnp.int32))
                     + _xpos_tables(jnp.arange(tb, dtype=jnp.int32)))
    cos4, sin4 = _xpos_tables(PAST_LEN + jnp.arange(T, dtype=jnp.int32))
    cos8 = jnp.tile(cos4, (PAIR_ROWS // T, 1))
    sin8 = jnp.tile(sin4, (PAIR_ROWS // T, 1))

    yp = x_prompt
    ys = x_sample.reshape(R, D)
    ck_all = cache_swa_k.reshape(n_even, DB, WINDOW, SWA_KV_WIDTH)
    cv_all = cache_swa_v.reshape(n_even, DB, WINDOW, SWA_KV_WIDTH)
    new_caches = None
    new_state = None
    prompt_state = None
    kp_l, vp_l, gv_l = [], [], []
    assert DEPTH % 2 == 0
    for pair in range(DEPTH // 2):
        even_layer, odd_layer = 2 * pair, 2 * pair + 1
        last = odd_layer == DEPTH - 1
        q, kn, vn_, sga, mixb, gv = _even_sample_a(ys, gains, even_layer, pair, even_win, ws4, bs4, lng_all,
                                                   steps=T)
        attn, nk, nv = _even_sample_b(q, kn, vn_, ck_all, cv_all, pair, new_caches, swa_sinks, table_flat, steps=T)
        new_caches = (nk, nv)
        ys = _even_sample_c(ys, sga, attn, mixb, even_wout, pair)
        gv_l.append(gv.reshape(DB, T, GMLP_WIDTH))
        q, kdt, v, intra, gate = _odd_sample_a(ys, gains, odd_layer, pair, odd_win, cos8, sin8, steps=T)
        yp, kp, vp, cross, new_state = _even_prompt(
            yp, gains, even_layer, pair, even_win, even_wout, swa_sinks, table_flat, gmlp_ws, bst_all, lng_all,
            (q, kdt, v, state_ret, pair, new_state), tb=PROMPT_TB_EVEN, steps=T)
        kp_l.append(kp.reshape(B, WINDOW, SWA_KV_HEADS, SWA_HEAD_DIM))
        vp_l.append(vp.reshape(B, WINDOW, SWA_KV_HEADS, SWA_HEAD_DIM))
        ys = _odd_sample_c(ys, intra, cross, gate, odd_wout, pair, fg, final_norm=last)
        yp, prompt_state = _odd_prompt(yp, gains, odd_layer, pair, odd_win, odd_wout, prompt_tables, fg,
                                       prompt_state, tb=tb, final_norm=last)
    cache_shape = (n_even, DB, WINDOW, SWA_KV_HEADS, SWA_HEAD_DIM)
    return (yp, ys.reshape(DB, T, D), jnp.stack(kp_l), jnp.stack(vp_l),
            new_caches[0].reshape(cache_shape), new_caches[1].reshape(cache_shape),
            prompt_state, new_state, jnp.stack(gv_l))
```
